```python
import math
import jax, jax.numpy as jnp
from jax import lax
import numpy as np

D_MODEL = 1024
BATCH = 8
SEQ = 2048
DEPTH = 4

GRID_W = 64
CTX_LEN = 256
N_MIXERS = 2
BRANCH = D_MODEL
GLA_HEADS = 4
GLA_DK = D_MODEL // 2 // GLA_HEADS
GLA_DV = BRANCH // GLA_HEADS
GLA_KEY_W = GLA_HEADS * GLA_DK
GLA_GATE_RANK = 16
GLA_TAU = 16.0
GLA_CHUNK = 64
NA_HEADS = 16
NA_DH = BRANCH // NA_HEADS
NA_KH = 8
NA_KW = 16
ROPE_BASE = 10000.0
LN_EPS = 1e-5
NORM_EPS = 1e-6
ALPHA = (2 * DEPTH) ** 0.25
BETA = (8 * DEPTH) ** -0.25
N_GLA_LAYERS = (DEPTH + 1) // N_MIXERS
N_NA_LAYERS = DEPTH // N_MIXERS

kernel_name = 'hybrid_gla_natten_prefix_dit'


def layer_norm(x, g, b):
    xf = x.astype(jnp.float32)
    mu = jnp.mean(xf, -1, keepdims=True)
    var = jnp.mean(jnp.square(xf - mu), -1, keepdims=True)
    y = (xf - mu) * lax.rsqrt(var + LN_EPS)
    return (y * g + b).astype(x.dtype)


def adaln(cond, w, b):
    m = jax.nn.silu(cond) @ w + b
    return jnp.split(m, 3, axis=-1)


def to_heads(t, n_heads):
    B, L, _ = t.shape
    return t.reshape(B, L, n_heads, -1).transpose(0, 2, 1, 3)


def from_heads(t):
    B, H, L, d = t.shape
    return t.transpose(0, 2, 1, 3).reshape(B, L, H * d)


def axial_rope_tables(n_tokens, dim):
    pos = jnp.arange(n_tokens, dtype=jnp.int32)
    row = (pos // GRID_W).astype(jnp.float32)
    col = (pos % GRID_W).astype(jnp.float32)
    quarter = dim // 4
    inv = ROPE_BASE ** (-jnp.arange(quarter, dtype=jnp.float32) / quarter)
    ang = jnp.concatenate([row[:, None] * inv, col[:, None] * inv], -1)
    return jnp.cos(ang), jnp.sin(ang)


def apply_rope(x, cos, sin):
    x1, x2 = jnp.split(x, 2, axis=-1)
    cos = cos.astype(x.dtype)
    sin = sin.astype(x.dtype)
    return jnp.concatenate([x1 * cos - x2 * sin, x1 * sin + x2 * cos], -1)


def gla_chunked(q, k, v, log_a, s0, strict):
    B, H, L, dk = q.shape
    dv = v.shape[-1]
    C = GLA_CHUNK
    n = L // C
    qc = q.reshape(B, H, n, C, dk)
    kc = k.reshape(B, H, n, C, dk)
    vc = v.reshape(B, H, n, C, dv)
    la = log_a.astype(jnp.float32).reshape(B, H, n, C, dk)
    b = jnp.cumsum(la, axis=3)
    b_last = b[:, :, :, -1:, :]
    qb = b - la if strict else b
    q_t = qc * jnp.exp(qb).astype(q.dtype)
    k_t = kc * jnp.exp(-b).astype(q.dtype)
    idx = jnp.arange(C)
    mask = (idx[None, :] < idx[:, None]) if strict else (idx[None, :] <= idx[:, None])
    att = jnp.where(mask, jnp.einsum('bhnqd,bhnsd->bhnqs', q_t, k_t), 0)
    o_intra = jnp.einsum('bhnqs,bhnse->bhnqe', att, vc)
    k_end = kc * jnp.exp(b_last - b).astype(q.dtype)
    kv = jnp.einsum('bhnsd,bhnse->bhnde', k_end, vc)
    dec = jnp.exp(b_last[:, :, :, 0, :])

    def step(s, inp):
        d_n, kv_n = inp
        return d_n[..., None] * s + kv_n, s

    s_fin, s_prev = lax.scan(step, s0, (jnp.moveaxis(dec, 2, 0), jnp.moveaxis(kv, 2, 0)))
    s_prev = jnp.moveaxis(s_prev, 0, 2)
    o_inter = jnp.einsum('bhnqd,bhnde->bhnqe', q_t, s_prev)
    o = (o_intra + o_inter).reshape(B, H, L, dv).astype(v.dtype)
    return o, s_fin


def gla_bidir(q, k, v, la_f, la_b, s0_f, s0_b):
    o_f, s_f = gla_chunked(q, k, v, la_f, s0_f, strict=False)
    flip = lambda t: jnp.flip(t, axis=2)
    o_b, s_b = gla_chunked(flip(q), flip(k), flip(v), flip(la_b), s0_b, strict=True)
    return o_f + flip(o_b), s_f, s_b


def gla_project(h, w_in, dec_w1, dec_w2, dec_b, rope):
    p = h @ w_in
    q, k, v, g = jnp.split(p, [GLA_KEY_W, 2 * GLA_KEY_W, 2 * GLA_KEY_W + BRANCH], axis=-1)
    q = to_heads(q, GLA_HEADS) * GLA_DK ** -0.5
    k = to_heads(k, GLA_HEADS)
    if rope is not None:
        cos, sin = rope
        q = apply_rope(q, cos, sin)
        k = apply_rope(k, cos, sin)
    v = to_heads(v, GLA_HEADS)
    low = jnp.einsum('bld,zdr->zblr', h, dec_w1)
    logits = jnp.einsum('zblr,zrk->zblk', low, dec_w2) + dec_b[:, None, None, :]
    log_a = jax.nn.log_sigmoid(logits.astype(jnp.float32)) / GLA_TAU
    return q, k, v, to_heads(log_a[0], GLA_HEADS), to_heads(log_a[1], GLA_HEADS), g


def gla_merge(o, g, norm_g, w_out):
    of = o.astype(jnp.float32)
    of = of * lax.rsqrt(jnp.mean(of * of, -1, keepdims=True) + NORM_EPS) * norm_g
    of = from_heads(of).astype(g.dtype)
    return (of * jax.nn.silu(g)) @ w_out


def gla_layer(h_lat, h_ctx, w_in, dec_w1, dec_w2, dec_b, norm_g, w_out, rope, need_ctx):
    qc, kc, vc, lfc, lbc, gc = gla_project(h_ctx, w_in, dec_w1, dec_w2, dec_b, None)
    ql, kl, vl, lfl, lbl, gl = gla_project(h_lat, w_in, dec_w1, dec_w2, dec_b, rope)
    s0 = jnp.zeros((h_lat.shape[0], GLA_HEADS, GLA_DK, GLA_DV), jnp.float32)
    o_c, s_f, s_b = gla_bidir(qc, kc, vc, lfc, lbc, s0, s0)
    o_l, _, _ = gla_bidir(ql, kl, vl, lfl, lbl, s_f, s_b)
    y_lat = gla_merge(o_l, gl, norm_g, w_out)
    y_ctx = gla_merge(o_c, gc, norm_g, w_out) if need_ctx else None
    return y_lat, y_ctx


def na_latent(q, k, v, kc, vc, rpb):
    B, H, S, dh = q.shape
    rows = S // GRID_W
    kh = min(NA_KH, rows)
    kw = NA_KW
    col = jnp.arange(GRID_W)
    col_start = jnp.clip(col - kw // 2, 0, GRID_W - kw)
    key_col = col_start[:, None] + jnp.arange(kw)[None, :]
    band_idx = (jnp.arange(kh)[None, :, None] * GRID_W + key_col[:, None, :]).reshape(GRID_W, kh * kw)
    col_off = key_col - col[:, None] + (NA_KW - 1)
    q_rows = jnp.moveaxis(q.reshape(B, H, rows, GRID_W, dh), 2, 0)

    def one_row(inp):
        r, q_r = inp
        r_start = jnp.clip(r - kh // 2, 0, rows - kh)
        k_band = lax.dynamic_slice_in_dim(k, r_start * GRID_W, kh * GRID_W, axis=2)
        v_band = lax.dynamic_slice_in_dim(v, r_start * GRID_W, kh * GRID_W, axis=2)
        k_g = k_band[:, :, band_idx]
        v_g = v_band[:, :, band_idx]
        row_off = r_start + jnp.arange(kh) - r + (NA_KH - 1)
        bias = rpb[:, row_off[None, :, None], col_off[:, None, :]].reshape(H, GRID_W, kh * kw)
        s_loc = jnp.einsum('bhqd,bhqkd->bhqk', q_r, k_g) + bias
        s_ctx = jnp.einsum('bhqd,bhcd->bhqc', q_r, kc)
        p = jax.nn.softmax(jnp.concatenate([s_loc, s_ctx], -1).astype(jnp.float32), axis=-1).astype(v.dtype)
        return (jnp.einsum('bhqk,bhqkd->bhqd', p[..., :kh * kw], v_g)
                + jnp.einsum('bhqc,bhcd->bhqd', p[..., kh * kw:], vc))

    o = lax.map(one_row, (jnp.arange(rows), q_rows))
    return jnp.moveaxis(o, 0, 2).reshape(B, H, S, dh)


def na_merge(o, g, w_out):
    return (from_heads(o) * jax.nn.silu(g)) @ w_out


def na_layer(h_lat, h_ctx, w_in, rpb, w_out, need_ctx):
    q, k, v, g = jnp.split(h_lat @ w_in, 4, axis=-1)
    q = to_heads(q, NA_HEADS) * NA_DH ** -0.5
    k = to_heads(k, NA_HEADS)
    v = to_heads(v, NA_HEADS)
    if need_ctx:
        qc, kc, vc, gc = jnp.split(h_ctx @ w_in, 4, axis=-1)
    else:
        kc, vc = jnp.split(h_ctx @ w_in[:, BRANCH:3 * BRANCH], 2, axis=-1)
    kc = to_heads(kc, NA_HEADS)
    vc = to_heads(vc, NA_HEADS)
    y_lat = na_merge(na_latent(q, k, v, kc, vc, rpb), g, w_out)
    y_ctx = None
    if need_ctx:
        qc = to_heads(qc, NA_HEADS) * NA_DH ** -0.5
        p = jax.nn.softmax(jnp.einsum('bhqd,bhkd->bhqk', qc, kc).astype(jnp.float32), axis=-1).astype(vc.dtype)
        y_ctx = na_merge(jnp.einsum('bhqk,bhkd->bhqd', p, vc), gc, w_out)
    return y_lat, y_ctx


def setup_inputs(seed: int = 0) -> dict:
    key = jax.random.key(seed)
    ks = jax.random.split(key, 17)
    nrm = lambda k, shape, s: jax.random.normal(k, shape, jnp.float32) * s
    D = D_MODEL
    return {
        'x': nrm(ks[0], (BATCH, SEQ, D), 1.0),
        'c': nrm(ks[1], (BATCH, D), 1.0),
        'ctx': nrm(ks[2], (BATCH, CTX_LEN, D), 1.0),
        'c_ctx': nrm(ks[3], (D,), 1.0),
        'ada_w': nrm(ks[4], (DEPTH, D, 3 * D), 0.5 * D ** -0.5),
        'ada_b': nrm(ks[5], (DEPTH, 3 * D), 0.02),
        'ln_g': 1.0 + nrm(ks[6], (DEPTH, D), 0.02),
        'ln_b': nrm(ks[7], (DEPTH, D), 0.02),
        'w_out': nrm(ks[8], (DEPTH, BRANCH, D), BETA * BRANCH ** -0.5),
        'gla_w_in': nrm(ks[9], (N_GLA_LAYERS, D, 2 * GLA_KEY_W + 2 * BRANCH), D ** -0.5),
        'gla_dec_w1': nrm(ks[10], (N_GLA_LAYERS, 2, D, GLA_GATE_RANK), D ** -0.5),
        'gla_dec_w2': nrm(ks[11], (N_GLA_LAYERS, 2, GLA_GATE_RANK, GLA_KEY_W), GLA_GATE_RANK ** -0.5),
        'gla_dec_b': nrm(ks[12], (N_GLA_LAYERS, 2, GLA_KEY_W), 0.1),
        'gla_norm_g': 1.0 + nrm(ks[13], (N_GLA_LAYERS, GLA_DV), 0.02),
        'na_w_in': nrm(ks[14], (N_NA_LAYERS, D, 4 * BRANCH), D ** -0.5),
        'na_rpb': nrm(ks[15], (N_NA_LAYERS, NA_HEADS, 2 * NA_KH - 1, 2 * NA_KW - 1), 0.1),
    }


def reference(x, c, ctx, c_ctx, ada_w, ada_b, ln_g, ln_b, w_out, gla_w_in, gla_dec_w1,
              gla_dec_w2, gla_dec_b, gla_norm_g, na_w_in, na_rpb):
    S = x.shape[1]
    rope = axial_rope_tables(S, GLA_DK)
    cond_lat = c[:, None, :]
    cond_ctx = c_ctx[None, None, :]
    for i in range(DEPTH):
        need_ctx = i < DEPTH - 1
        j = i // N_MIXERS
        sh_l, sc_l, gt_l = adaln(cond_lat, ada_w[i], ada_b[i])
        sh_c, sc_c, gt_c = adaln(cond_ctx, ada_w[i], ada_b[i])
        h_lat = x * (1 + sc_l) + sh_l
        h_ctx = ctx * (1 + sc_c) + sh_c
        if i % N_MIXERS == 0:
            y_lat, y_ctx = gla_layer(h_lat, h_ctx, gla_w_in[j], gla_dec_w1[j], gla_dec_w2[j],
                                     gla_dec_b[j], gla_norm_g[j], w_out[i], rope, need_ctx)
        else:
            y_lat, y_ctx = na_layer(h_lat, h_ctx, na_w_in[j], na_rpb[j], w_out[i], need_ctx)
        x = layer_norm(ALPHA * x + gt_l * y_lat, ln_g[i], ln_b[i])
        if need_ctx:
            ctx = layer_norm(ALPHA * ctx + gt_c * y_ctx, ln_g[i], ln_b[i])
    return x
```

```python
import functools

import numpy as np
import jax
import jax.numpy as jnp
from jax import lax
from jax.experimental import pallas as pl
from jax.experimental.pallas import tpu as pltpu

D_MODEL = 1024
DEPTH = 4
GRID_W = 64
CTX_LEN = 256
N_MIXERS = 2
BRANCH = D_MODEL
GLA_HEADS = 4
GLA_DK = 128
GLA_DV = 256
GLA_KEY_W = GLA_HEADS * GLA_DK
GLA_GATE_RANK = 16
GLA_TAU = 16.0
GLA_CHUNK = 64
NA_HEADS = 16
NA_DH = 64
NA_KH = 8
NA_KW = 16
ROPE_BASE = 10000.0
LN_EPS = 1e-5
NORM_EPS = 1e-6
ALPHA = (2 * DEPTH) ** 0.25

TOKEN_TILE = 256
CHUNKS_PER_TILE = TOKEN_TILE // GLA_CHUNK
COND_ROWS = 16
CTX_COND_ROW = 8
LOW_PAD = 128
NEG_BIG = -1e30
VMEM_LIMIT = 56 * 1024 * 1024

_F32 = jnp.float32
_BF16 = jnp.bfloat16


def _dot(a, b):
    return jnp.dot(a, b, preferred_element_type=_F32)


def _dot_nt(a, b):
    return lax.dot_general(a, b, (((1,), (1,)), ((), ())), preferred_element_type=_F32)


def _dot_tn(a, b):
    return lax.dot_general(a, b, (((0,), (0,)), ((), ())), preferred_element_type=_F32)


def _silu(x):
    return x * (1.0 / (1.0 + jnp.exp(-x)))


def _split_bf16(x):
    hi = x.astype(_BF16)
    lo = (x - hi.astype(_F32)).astype(_BF16)
    return hi, lo


def _mod_kernel(cond_ref, w_ref, b_ref, out_ref):
    s = _silu(cond_ref[...]).astype(_BF16)
    out_ref[0] = _dot(s, w_ref[0].astype(_BF16)) + b_ref[0]


def _modulation(cond, ada_w, ada_b):
    n_col = 3 * D_MODEL // D_MODEL
    return pl.pallas_call(
        _mod_kernel,
        grid=(DEPTH, n_col),
        in_specs=[
            pl.BlockSpec((COND_ROWS, D_MODEL), lambda i, j: (0, 0)),
            pl.BlockSpec((1, D_MODEL, D_MODEL), lambda i, j: (i, 0, j)),
            pl.BlockSpec((1, 1, D_MODEL), lambda i, j: (i, 0, j)),
        ],
        out_specs=pl.BlockSpec((1, COND_ROWS, D_MODEL), lambda i, j: (i, 0, j)),
        out_shape=jax.ShapeDtypeStruct((DEPTH, COND_ROWS, 3 * D_MODEL), _F32),
        compiler_params=pltpu.CompilerParams(
            dimension_semantics=("arbitrary", "arbitrary"), vmem_limit_bytes=VMEM_LIMIT),
        name="modulation",
    )(cond, ada_w, ada_b.reshape(DEPTH, 1, 3 * D_MODEL))


def _mod_spec(tile_offset=0):
    return pl.BlockSpec((1, 1, 3 * D_MODEL),
                        lambda b, t: (jnp.where(t + tile_offset == 0, CTX_COND_ROW, b), 0, 0))


def _modulated_tile(x_ref, mod_ref):
    m = mod_ref[0]
    shift = m[:, 0:D_MODEL]
    scale = m[:, D_MODEL:2 * D_MODEL]
    return (x_ref[0] * (1.0 + scale) + shift).astype(_BF16)


def _gla_proj_kernel(x_ref, mod_ref, w_ref, w2_ref, decb_ref, cos_ref, sin_ref, trif_ref, trib_ref,
                     qf_ref, kf_ref, kef_ref, qb_ref, kb_ref, keb_ref, v_ref, g_ref, dec_ref):
    t = pl.program_id(1)
    hb = _modulated_tile(x_ref, mod_ref)
    kw = GLA_KEY_W
    q = _dot(hb, w_ref[:, 0:kw]) * (GLA_DK ** -0.5)
    k = _dot(hb, w_ref[:, kw:2 * kw])
    v_ref[0] = _dot(hb, w_ref[:, 2 * kw:2 * kw + BRANCH]).astype(_BF16)
    g_ref[0] = _dot(hb, w_ref[:, 2 * kw + BRANCH:2 * kw + 2 * BRANCH]).astype(_BF16)
    low = _dot(hb, w_ref[:, 2 * kw + 2 * BRANCH:2 * kw + 2 * BRANCH + LOW_PAD])
    low_hi, low_lo = _split_bf16(low)
    logits = _dot(low_hi, w2_ref[...]) + _dot(low_lo, w2_ref[...]) + decb_ref[...]
    la = (jnp.minimum(logits, 0.0) - jnp.log1p(jnp.exp(-jnp.abs(logits)))) * (1.0 / GLA_TAU)
    la_f = la[:, 0:kw]
    la_b = la[:, kw:2 * kw]
    lf_hi, lf_lo = _split_bf16(la_f)
    lb_hi, lb_lo = _split_bf16(la_b)
    cum_f = _dot(trif_ref[...], lf_hi) + _dot(trif_ref[...], lf_lo)
    cum_b = _dot(trib_ref[...], lb_hi) + _dot(trib_ref[...], lb_lo)
    pre_f = cum_f[0:TOKEN_TILE]
    post_f = cum_f[TOKEN_TILE:2 * TOKEN_TILE]
    suf_b = cum_b[0:TOKEN_TILE]
    ante_b = cum_b[TOKEN_TILE:2 * TOKEN_TILE]

    e_qf = jnp.exp(pre_f)
    e_kf = jnp.exp(-pre_f)
    e_kef = jnp.exp(post_f)
    e_qb = jnp.exp(suf_b - la_b)
    e_kb = jnp.exp(-suf_b)
    e_keb = jnp.exp(ante_b)

    is_lat = t > 0
    cos = jnp.where(is_lat, cos_ref[...], 1.0)
    sin = jnp.where(is_lat, sin_ref[...], 0.0)
    for h in range(GLA_HEADS):
        sl = slice(h * GLA_DK, (h + 1) * GLA_DK)
        qh = q[:, sl]
        kh = k[:, sl]
        qh = qh * cos + pltpu.roll(qh, GLA_DK // 2, 1) * sin
        kh = kh * cos + pltpu.roll(kh, GLA_DK // 2, 1) * sin
        qf_ref[0, :, sl] = (qh * e_qf[:, sl]).astype(_BF16)
        kf_ref[0, :, sl] = (kh * e_kf[:, sl]).astype(_BF16)
        kef_ref[0, :, sl] = (kh * e_kef[:, sl]).astype(_BF16)
        qb_ref[0, :, sl] = (qh * e_qb[:, sl]).astype(_BF16)
        kb_ref[0, :, sl] = (kh * e_kb[:, sl]).astype(_BF16)
        keb_ref[0, :, sl] = (kh * e_keb[:, sl]).astype(_BF16)

    tot_f = pre_f + post_f
    tot_b = suf_b + ante_b
    rows = [tot_f[c * GLA_CHUNK:c * GLA_CHUNK + 1] for c in range(CHUNKS_PER_TILE)]
    rows += [tot_b[c * GLA_CHUNK:c * GLA_CHUNK + 1] for c in range(CHUNKS_PER_TILE)]
    dec_ref[0, 0] = jnp.exp(jnp.concatenate(rows, axis=0))


def _chunk_sum_matrices():
    i = np.arange(TOKEN_TILE)
    same = (i[:, None] // GLA_CHUNK) == (i[None, :] // GLA_CHUNK)
    le = i[None, :] <= i[:, None]
    ge = i[None, :] >= i[:, None]
    tri_f = np.concatenate([same & le, same & ~le], 0)
    tri_b = np.concatenate([same & ge, same & ~ge], 0)
    return jnp.asarray(tri_f, _BF16), jnp.asarray(tri_b, _BF16)


def _gla_project(xs, mod_i, w_cat, w2_cat, dec_b, cos_t, sin_t):
    B, L, _ = xs.shape
    n_t = L // TOKEN_TILE
    n_w = w_cat.shape[1]
    tri_f, tri_b = _chunk_sum_matrices()
    key_spec = pl.BlockSpec((1, TOKEN_TILE, GLA_KEY_W), lambda b, t: (b, t, 0))
    val_spec = pl.BlockSpec((1, TOKEN_TILE, BRANCH), lambda b, t: (b, t, 0))
    key_shape = jax.ShapeDtypeStruct((B, L, GLA_KEY_W), _BF16)
    val_shape = jax.ShapeDtypeStruct((B, L, BRANCH), _BF16)
    const = lambda shape: pl.BlockSpec(shape, lambda b, t: (0,) * len(shape))
    rope_spec = pl.BlockSpec((TOKEN_TILE, GLA_DK), lambda b, t: (jnp.maximum(t - 1, 0), 0))
    return pl.pallas_call(
        _gla_proj_kernel,
        grid=(B, n_t),
        in_specs=[
            pl.BlockSpec((1, TOKEN_TILE, D_MODEL), lambda b, t: (b, t, 0)),
            _mod_spec(),
            const((D_MODEL, n_w)),
            const((LOW_PAD, 2 * GLA_KEY_W)),
            const((1, 2 * GLA_KEY_W)),
            rope_spec, rope_spec,
            const((2 * TOKEN_TILE, TOKEN_TILE)),
            const((2 * TOKEN_TILE, TOKEN_TILE)),
        ],
        out_specs=[key_spec] * 6 + [val_spec] * 2 + [
            pl.BlockSpec((1, 1, 2 * CHUNKS_PER_TILE, GLA_KEY_W), lambda b, t: (b, t, 0, 0))],
        out_shape=[key_shape] * 6 + [val_shape] * 2 + [
            jax.ShapeDtypeStruct((B, n_t, 2 * CHUNKS_PER_TILE, GLA_KEY_W), _F32)],
        compiler_params=pltpu.CompilerParams(
            dimension_semantics=("arbitrary", "arbitrary"), vmem_limit_bytes=VMEM_LIMIT),
        name="gla_project",
    )(xs, mod_i, w_cat, w2_cat, dec_b, cos_t, sin_t, tri_f, tri_b)


GLA_HEADS_PER_STEP = 2


def _gla_mix_kernel(qf_ref, kf_ref, kef_ref, qb_ref, kb_ref, keb_ref, v_ref, dec_ref, ng_ref,
                    out_ref, acc_ref, sf_ref, sb_ref):
    n_chunks = acc_ref.shape[0] // GLA_CHUNK
    n_ctx = CTX_LEN // GLA_CHUNK
    C = GLA_CHUNK
    acc_ref[...] = jnp.zeros_like(acc_ref)
    sf_ref[...] = jnp.zeros_like(sf_ref)
    sb_ref[...] = jnp.zeros_like(sb_ref)
    ti = lax.broadcasted_iota(jnp.int32, (C, C), 0)
    si = lax.broadcasted_iota(jnp.int32, (C, C), 1)
    mask_f = si <= ti
    mask_b = si > ti
    sub8 = lax.broadcasted_iota(jnp.int32, (2 * CHUNKS_PER_TILE, GLA_DK), 0)

    def one_chain(c, h, q_ref, k_ref, ke_ref, s_ref, mask, dec_row):
        rows = pl.ds(pl.multiple_of(c * C, C), C)
        ksl = slice(h * GLA_DK, (h + 1) * GLA_DK)
        vsl = slice(h * GLA_DV, (h + 1) * GLA_DV)
        q = q_ref[0, rows, ksl]
        k = k_ref[0, rows, ksl]
        ke = ke_ref[0, rows, ksl]
        v = v_ref[0, rows, vsl]
        s = s_ref[h]
        att = jnp.where(mask, _dot_nt(q, k), 0.0).astype(_BF16)
        o = _dot(att, v) + _dot(q, s.astype(_BF16))
        acc_ref[rows, vsl] += o
        kv = _dot_tn(ke, v)
        tile = lax.shift_right_logical(c, CHUNKS_PER_TILE.bit_length() - 1)
        dec8 = dec_ref[0, tile, :, ksl]
        pick = sub8 == dec_row + (c & (CHUNKS_PER_TILE - 1))
        dec = jnp.sum(jnp.where(pick, dec8, 0.0), axis=0, keepdims=True)
        dcol = jnp.broadcast_to(dec, (GLA_DK, GLA_DK)).T
        s_ref[h] = s * jnp.concatenate([dcol, dcol], axis=1) + kv

    def body(i, carry):
        cb = jnp.where(i < n_ctx, n_ctx - 1 - i, n_chunks + n_ctx - 1 - i)
        for h in range(GLA_HEADS_PER_STEP):
            one_chain(i, h, qf_ref, kf_ref, kef_ref, sf_ref, mask_f, 0)
            one_chain(cb, h, qb_ref, kb_ref, keb_ref, sb_ref, mask_b, CHUNKS_PER_TILE)
        return carry

    lax.fori_loop(0, n_chunks, body, 0)

    def norm_body(j, carry):
        rows = pl.ds(pl.multiple_of(j * TOKEN_TILE, TOKEN_TILE), TOKEN_TILE)
        for h in range(GLA_HEADS_PER_STEP):
            vsl = slice(h * GLA_DV, (h + 1) * GLA_DV)
            o = acc_ref[rows, vsl]
            ms = jnp.mean(o * o, axis=-1, keepdims=True)
            out_ref[0, rows, vsl] = (o * lax.rsqrt(ms + NORM_EPS) * ng_ref[...]).astype(_BF16)
        return carry

    lax.fori_loop(0, acc_ref.shape[0] // TOKEN_TILE, norm_body, 0)


def _gla_mix(qf, kf, kef, qb, kb, keb, v, dec, norm_g):
    B, L, _ = qf.shape
    hs = GLA_HEADS_PER_STEP
    n_t = dec.shape[1]
    key_spec = pl.BlockSpec((1, L, hs * GLA_DK), lambda b, p: (b, 0, p))
    val_spec = pl.BlockSpec((1, L, hs * GLA_DV), lambda b, p: (b, 0, p))
    return pl.pallas_call(
        _gla_mix_kernel,
        grid=(B, GLA_HEADS // hs),
        in_specs=[key_spec] * 6 + [
            val_spec,
            pl.BlockSpec((1, n_t, 2 * CHUNKS_PER_TILE, hs * GLA_DK), lambda b, p: (b, 0, 0, p)),
            pl.BlockSpec((1, GLA_DV), lambda b, p: (0, 0)),
        ],
        out_specs=val_spec,
        out_shape=jax.ShapeDtypeStruct((B, L, BRANCH), _BF16),
        scratch_shapes=[
            pltpu.VMEM((L, hs * GLA_DV), _F32),
            pltpu.VMEM((hs, GLA_DK, GLA_DV), _F32),
            pltpu.VMEM((hs, GLA_DK, GLA_DV), _F32),
        ],
        compiler_params=pltpu.CompilerParams(
            dimension_semantics=("arbitrary", "arbitrary"), vmem_limit_bytes=VMEM_LIMIT),
        name="gla_mix",
    )(qf, kf, kef, qb, kb, keb, v, dec, norm_g.reshape(1, GLA_DV))


def _na_proj_kernel(x_ref, mod_ref, w_ref, q_ref, k_ref, v_ref, g_ref):
    hb = _modulated_tile(x_ref, mod_ref)
    q_ref[0] = (_dot(hb, w_ref[:, 0:BRANCH]) * (NA_DH ** -0.5)).astype(_BF16)
    k_ref[0] = _dot(hb, w_ref[:, BRANCH:2 * BRANCH]).astype(_BF16)
    v_ref[0] = _dot(hb, w_ref[:, 2 * BRANCH:3 * BRANCH]).astype(_BF16)
    g_ref[0] = _dot(hb, w_ref[:, 3 * BRANCH:4 * BRANCH]).astype(_BF16)


def _na_project(xs, mod_i, w_in):
    B, L, _ = xs.shape
    spec = pl.BlockSpec((1, TOKEN_TILE, BRANCH), lambda b, t: (b, t, 0))
    shape = jax.ShapeDtypeStruct((B, L, BRANCH), _BF16)
    return pl.pallas_call(
        _na_proj_kernel,
        grid=(B, L // TOKEN_TILE),
        in_specs=[
            pl.BlockSpec((1, TOKEN_TILE, D_MODEL), lambda b, t: (b, t, 0)),
            _mod_spec(),
            pl.BlockSpec((D_MODEL, 4 * BRANCH), lambda b, t: (0, 0)),
        ],
        out_specs=[spec] * 4,
        out_shape=[shape] * 4,
        compiler_params=pltpu.CompilerParams(
            dimension_semantics=("arbitrary", "arbitrary"), vmem_limit_bytes=VMEM_LIMIT),
        name="na_project",
    )(xs, mod_i, w_in)


NA_ROWS = 2048 // GRID_W
NA_PAIR_W = 2 * NA_DH
NA_BAND = NA_KH * GRID_W


def _stack_heads(q2):
    lane = lax.broadcasted_iota(jnp.int32, q2.shape, 1)
    zero = jnp.zeros_like(q2)
    return jnp.concatenate([jnp.where(lane < NA_DH, q2, zero), jnp.where(lane >= NA_DH, q2, zero)], axis=0)


def _unstack_heads(o, n):
    lane = lax.broadcasted_iota(jnp.int32, (n, NA_PAIR_W), 1)
    return jnp.where(lane < NA_DH, o[0:n], o[n:2 * n])


def _na_mix_kernel(q_ref, k_ref, v_ref, bias_ref, out_ref):
    kc = k_ref[0, 0:CTX_LEN, :]
    vc = v_ref[0, 0:CTX_LEN, :]

    qs = _stack_heads(q_ref[0, 0:CTX_LEN, :])
    s = _dot_nt(qs, kc)
    m = jnp.max(s, axis=-1, keepdims=True)
    p = jnp.exp(s - m)
    l = jnp.sum(p, axis=-1, keepdims=True)
    o = _dot(p.astype(_BF16), vc) / l
    out_ref[0, 0:CTX_LEN, :] = _unstack_heads(o, CTX_LEN).astype(_BF16)

    def row_body(r, carry):
        r_start = jnp.clip(r - NA_KH // 2, 0, NA_ROWS - NA_KH)
        q_rows = pl.ds(pl.multiple_of(CTX_LEN + r * GRID_W, GRID_W), GRID_W)
        band = pl.ds(pl.multiple_of(CTX_LEN + r_start * GRID_W, GRID_W), NA_BAND)
        qs = _stack_heads(q_ref[0, q_rows, :])
        s_loc = _dot_nt(qs, k_ref[0, band, :]) + bias_ref[0, r - r_start]
        s_ctx = _dot_nt(qs, kc)
        m = jnp.maximum(jnp.max(s_loc, axis=-1, keepdims=True), jnp.max(s_ctx, axis=-1, keepdims=True))
        p_loc = jnp.exp(s_loc - m)
        p_ctx = jnp.exp(s_ctx - m)
        l = jnp.sum(p_loc, axis=-1, keepdims=True) + jnp.sum(p_ctx, axis=-1, keepdims=True)
        o = (_dot(p_loc.astype(_BF16), v_ref[0, band, :]) + _dot(p_ctx.astype(_BF16), vc)) / l
        out_ref[0, q_rows, :] = _unstack_heads(o, GRID_W).astype(_BF16)
        return carry

    lax.fori_loop(0, NA_ROWS, row_body, 0)


def _na_bias_table(rpb):
    col = np.arange(GRID_W)
    col_start = np.clip(col - NA_KW // 2, 0, GRID_W - NA_KW)
    kcol = np.arange(GRID_W)
    valid = (kcol[None, :] >= col_start[:, None]) & (kcol[None, :] < col_start[:, None] + NA_KW)
    col_off = np.clip(kcol[None, :] - col[:, None] + (NA_KW - 1), 0, 2 * NA_KW - 2)
    d = np.arange(NA_KH)
    j = np.arange(NA_KH)
    row_off = j[None, :] - d[:, None] + (NA_KH - 1)
    g = rpb[:, row_off[:, None, :, None], col_off[None, :, None, :]]
    g = jnp.where(valid[None, None, :, None, :], g, NEG_BIG)
    g = g.reshape(NA_HEADS // 2, 2, NA_KH, GRID_W, NA_BAND)
    return jnp.moveaxis(g, 1, 2).reshape(NA_HEADS // 2, NA_KH, 2 * GRID_W, NA_BAND)


def _na_mix(q, k, v, bias):
    B, L, _ = q.shape
    n_pairs = NA_HEADS // 2
    spec = pl.BlockSpec((1, L, NA_PAIR_W), lambda p, b: (b, 0, p))
    return pl.pallas_call(
        _na_mix_kernel,
        grid=(n_pairs, B),
        in_specs=[spec, spec, spec,
                  pl.BlockSpec((1, NA_KH, 2 * GRID_W, NA_BAND), lambda p, b: (p, 0, 0, 0))],
        out_specs=spec,
        out_shape=jax.ShapeDtypeStruct((B, L, BRANCH), _BF16),
        compiler_params=pltpu.CompilerParams(
            dimension_semantics=("arbitrary", "arbitrary"), vmem_limit_bytes=VMEM_LIMIT),
        name="na_mix",
    )(q, k, v, bias)


def _merge_kernel(o_ref, g_ref, x_ref, mod_ref, w_ref, lng_ref, lnb_ref, out_ref):
    gate = mod_ref[0][:, 2 * D_MODEL:3 * D_MODEL]
    a = (o_ref[0].astype(_F32) * _silu(g_ref[0].astype(_F32))).astype(_BF16)
    y = _dot(a, w_ref[...])
    z = ALPHA * x_ref[0] + gate * y
    mu = jnp.mean(z, axis=-1, keepdims=True)
    zc = z - mu
    var = jnp.mean(zc * zc, axis=-1, keepdims=True)
    out_ref[0] = zc * lax.rsqrt(var + LN_EPS) * lng_ref[...] + lnb_ref[...]


def _merge(o, g, xs, mod_i, w_out, ln_g, ln_b, latent_only):
    B, L, _ = xs.shape
    off = 1 if latent_only else 0
    n_t = L // TOKEN_TILE - off
    tile = lambda b, t: (b, t + off, 0)
    return pl.pallas_call(
        _merge_kernel,
        grid=(B, n_t),
        in_specs=[
            pl.BlockSpec((1, TOKEN_TILE, BRANCH), tile),
            pl.BlockSpec((1, TOKEN_TILE, BRANCH), tile),
            pl.BlockSpec((1, TOKEN_TILE, D_MODEL), tile),
            _mod_spec(off),
            pl.BlockSpec((BRANCH, D_MODEL), lambda b, t: (0, 0)),
            pl.BlockSpec((1, D_MODEL), lambda b, t: (0, 0)),
            pl.BlockSpec((1, D_MODEL), lambda b, t: (0, 0)),
        ],
        out_specs=pl.BlockSpec((1, TOKEN_TILE, D_MODEL), lambda b, t: (b, t, 0)),
        out_shape=jax.ShapeDtypeStruct((B, n_t * TOKEN_TILE, D_MODEL), _F32),
        compiler_params=pltpu.CompilerParams(
            dimension_semantics=("arbitrary", "arbitrary"), vmem_limit_bytes=VMEM_LIMIT),
        name="merge",
    )(o, g, xs, mod_i, w_out, ln_g.reshape(1, D_MODEL), ln_b.reshape(1, D_MODEL))


def _rope_tables(n_tokens):
    pos = jnp.arange(n_tokens, dtype=jnp.int32)
    row = (pos // GRID_W).astype(_F32)
    col = (pos % GRID_W).astype(_F32)
    quarter = GLA_DK // 4
    inv = ROPE_BASE ** (-jnp.arange(quarter, dtype=_F32) / quarter)
    ang = jnp.concatenate([row[:, None] * inv, col[:, None] * inv], -1)
    cos, sin = jnp.cos(ang), jnp.sin(ang)
    return jnp.concatenate([cos, cos], -1), jnp.concatenate([-sin, sin], -1)


def kernel(x, c, ctx, c_ctx, ada_w, ada_b, ln_g, ln_b, w_out, gla_w_in, gla_dec_w1, gla_dec_w2,
           gla_dec_b, gla_norm_g, na_w_in, na_rpb):
    B, S, D = x.shape
    assert (B, S, D) == (8, 2048, D_MODEL) and ctx.shape == (B, CTX_LEN, D)
    cond = jnp.zeros((COND_ROWS, D), _F32).at[0:B].set(c).at[CTX_COND_ROW].set(c_ctx)
    mod = _modulation(cond, ada_w, ada_b).reshape(DEPTH, COND_ROWS, 1, 3 * D_MODEL)
    cos_t, sin_t = _rope_tables(S)
    xs = jnp.concatenate([ctx, x], axis=1)

    for i in range(DEPTH):
        j = i // N_MIXERS
        last = i == DEPTH - 1
        w_o = w_out[i].astype(_BF16)
        if i % N_MIXERS == 0:
            w1 = jnp.concatenate([gla_dec_w1[j, 0], gla_dec_w1[j, 1]], axis=1)
            w1 = jnp.pad(w1, ((0, 0), (0, LOW_PAD - 2 * GLA_GATE_RANK)))
            w_cat = jnp.concatenate([gla_w_in[j], w1], axis=1).astype(_BF16)
            w2_cat = jnp.zeros((LOW_PAD, 2 * GLA_KEY_W), _F32)
            w2_cat = w2_cat.at[0:GLA_GATE_RANK, 0:GLA_KEY_W].set(gla_dec_w2[j, 0])
            w2_cat = w2_cat.at[GLA_GATE_RANK:2 * GLA_GATE_RANK, GLA_KEY_W:].set(gla_dec_w2[j, 1])
            dec_b = gla_dec_b[j].reshape(1, 2 * GLA_KEY_W)
            qf, kf, kef, qb, kb, keb, v, g, dec = _gla_project(
                xs, mod[i], w_cat, w2_cat.astype(_BF16), dec_b, cos_t, sin_t)
            o = _gla_mix(qf, kf, kef, qb, kb, keb, v, dec, gla_norm_g[j])
        else:
            q, k, v, g = _na_project(xs, mod[i], na_w_in[j].astype(_BF16))
            o = _na_mix(q, k, v, _na_bias_table(na_rpb[j]))
        xs = _merge(o, g, xs, mod[i], w_o, ln_g[i], ln_b[i], latent_only=last)
    return xs
```

```python
import functools

import numpy as np
import jax
import jax.numpy as jnp
from jax import lax
from jax.experimental import pallas as pl
from jax.experimental.pallas import tpu as pltpu

D_MODEL = 1024
DEPTH = 4
GRID_W = 64
CTX_LEN = 256
N_MIXERS = 2
BRANCH = D_MODEL
GLA_HEADS = 4
GLA_DK = 128
GLA_DV = 256
GLA_KEY_W = GLA_HEADS * GLA_DK
GLA_GATE_RANK = 16
GLA_TAU = 16.0
GLA_CHUNK = 64
NA_HEADS = 16
NA_DH = 64
NA_KH = 8
NA_KW = 16
ROPE_BASE = 10000.0
LN_EPS = 1e-5
NORM_EPS = 1e-6
ALPHA = (2 * DEPTH) ** 0.25

TOKEN_TILE = 256
CHUNKS_PER_TILE = TOKEN_TILE // GLA_CHUNK
COND_ROWS = 16
CTX_COND_ROW = 8
LOW_PAD = 128
NEG_BIG = -1e30
VMEM_LIMIT = 56 * 1024 * 1024

_F32 = jnp.float32
_BF16 = jnp.bfloat16


def _dot(a, b):
    return jnp.dot(a, b, preferred_element_type=_F32)


def _dot_nt(a, b):
    return lax.dot_general(a, b, (((1,), (1,)), ((), ())), preferred_element_type=_F32)


def _dot_tn(a, b):
    return lax.dot_general(a, b, (((0,), (0,)), ((), ())), preferred_element_type=_F32)


def _silu(x):
    return x * (1.0 / (1.0 + jnp.exp(-x)))


def _split_bf16(x):
    hi = x.astype(_BF16)
    lo = (x - hi.astype(_F32)).astype(_BF16)
    return hi, lo


def _mod_kernel(cond_ref, w_ref, b_ref, out_ref):
    s = _silu(cond_ref[...]).astype(_BF16)
    out_ref[0] = _dot(s, w_ref[0].astype(_BF16)) + b_ref[0]


def _modulation(cond, ada_w, ada_b):
    n_col = 3 * D_MODEL // D_MODEL
    return pl.pallas_call(
        _mod_kernel,
        grid=(DEPTH, n_col),
        in_specs=[
            pl.BlockSpec((COND_ROWS, D_MODEL), lambda i, j: (0, 0)),
            pl.BlockSpec((1, D_MODEL, D_MODEL), lambda i, j: (i, 0, j)),
            pl.BlockSpec((1, 1, D_MODEL), lambda i, j: (i, 0, j)),
        ],
        out_specs=pl.BlockSpec((1, COND_ROWS, D_MODEL), lambda i, j: (i, 0, j)),
        out_shape=jax.ShapeDtypeStruct((DEPTH, COND_ROWS, 3 * D_MODEL), _F32),
        compiler_params=pltpu.CompilerParams(
            dimension_semantics=("arbitrary", "arbitrary"), vmem_limit_bytes=VMEM_LIMIT),
        name="modulation",
    )(cond, ada_w, ada_b.reshape(DEPTH, 1, 3 * D_MODEL))


def _mod_spec(tile_offset=0):
    return pl.BlockSpec((1, 1, 3 * D_MODEL),
                        lambda b, t: (jnp.where(t + tile_offset == 0, CTX_COND_ROW, b), 0, 0))


def _modulated_tile(x_ref, mod_ref):
    m = mod_ref[0]
    shift = m[:, 0:D_MODEL]
    scale = m[:, D_MODEL:2 * D_MODEL]
    return (x_ref[0] * (1.0 + scale) + shift).astype(_BF16)


def _gla_proj_kernel(x_ref, mod_ref, w_ref, w2_ref, decb_ref, cos_ref, sin_ref, trif_ref, trib_ref,
                     qf_ref, kf_ref, kef_ref, qb_ref, kb_ref, keb_ref, v_ref, g_ref, dec_ref):
    t = pl.program_id(1)
    hb = _modulated_tile(x_ref, mod_ref)
    kw = GLA_KEY_W
    q = _dot(hb, w_ref[:, 0:kw]) * (GLA_DK ** -0.5)
    k = _dot(hb, w_ref[:, kw:2 * kw])
    v_ref[0] = _dot(hb, w_ref[:, 2 * kw:2 * kw + BRANCH]).astype(_BF16)
    g_ref[0] = _dot(hb, w_ref[:, 2 * kw + BRANCH:2 * kw + 2 * BRANCH]).astype(_BF16)
    low = _dot(hb, w_ref[:, 2 * kw + 2 * BRANCH:2 * kw + 2 * BRANCH + LOW_PAD])
    low_hi, low_lo = _split_bf16(low)
    logits = _dot(low_hi, w2_ref[...]) + _dot(low_lo, w2_ref[...]) + decb_ref[...]
    la = (jnp.minimum(logits, 0.0) - jnp.log1p(jnp.exp(-jnp.abs(logits)))) * (1.0 / GLA_TAU)
    la_f = la[:, 0:kw]
    la_b = la[:, kw:2 * kw]
    lf_hi, lf_lo = _split_bf16(la_f)
    lb_hi, lb_lo = _split_bf16(la_b)
    cum_f = _dot(trif_ref[...], lf_hi) + _dot(trif_ref[...], lf_lo)
    cum_b = _dot(trib_ref[...], lb_hi) + _dot(trib_ref[...], lb_lo)
    pre_f = cum_f[0:TOKEN_TILE]
    post_f = cum_f[TOKEN_TILE:2 * TOKEN_TILE]
    suf_b = cum_b[0:TOKEN_TILE]
    ante_b = cum_b[TOKEN_TILE:2 * TOKEN_TILE]

    e_qf = jnp.exp(pre_f)
    e_kf = jnp.exp(-pre_f)
    e_kef = jnp.exp(post_f)
    e_qb = jnp.exp(suf_b - la_b)
    e_kb = jnp.exp(-suf_b)
    e_keb = jnp.exp(ante_b)

    is_lat = t > 0
    cos = jnp.where(is_lat, cos_ref[...], 1.0)
    sin = jnp.where(is_lat, sin_ref[...], 0.0)
    for h in range(GLA_HEADS):
        sl = slice(h * GLA_DK, (h + 1) * GLA_DK)
        qh = q[:, sl]
        kh = k[:, sl]
        qh = qh * cos + pltpu.roll(qh, GLA_DK // 2, 1) * sin
        kh = kh * cos + pltpu.roll(kh, GLA_DK // 2, 1) * sin
        qf_ref[0, :, sl] = (qh * e_qf[:, sl]).astype(_BF16)
        kf_ref[0, :, sl] = (kh * e_kf[:, sl]).astype(_BF16)
        kef_ref[0, :, sl] = (kh * e_kef[:, sl]).astype(_BF16)
        qb_ref[0, :, sl] = (qh * e_qb[:, sl]).astype(_BF16)
        kb_ref[0, :, sl] = (kh * e_kb[:, sl]).astype(_BF16)
        keb_ref[0, :, sl] = (kh * e_keb[:, sl]).astype(_BF16)

    tot_f = pre_f + post_f
    tot_b = suf_b + ante_b
    rows = [tot_f[c * GLA_CHUNK:c * GLA_CHUNK + 1] for c in range(CHUNKS_PER_TILE)]
    rows += [tot_b[c * GLA_CHUNK:c * GLA_CHUNK + 1] for c in range(CHUNKS_PER_TILE)]
    dec_ref[0, 0] = jnp.exp(jnp.concatenate(rows, axis=0))


def _chunk_sum_matrices():
    i = np.arange(TOKEN_TILE)
    same = (i[:, None] // GLA_CHUNK) == (i[None, :] // GLA_CHUNK)
    le = i[None, :] <= i[:, None]
    ge = i[None, :] >= i[:, None]
    tri_f = np.concatenate([same & le, same & ~le], 0)
    tri_b = np.concatenate([same & ge, same & ~ge], 0)
    return jnp.asarray(tri_f, _BF16), jnp.asarray(tri_b, _BF16)


def _gla_project(xs, mod_i, w_cat, w2_cat, dec_b, cos_t, sin_t):
    B, L, _ = xs.shape
    n_t = L // TOKEN_TILE
    n_w = w_cat.shape[1]
    tri_f, tri_b = _chunk_sum_matrices()
    key_spec = pl.BlockSpec((1, TOKEN_TILE, GLA_KEY_W), lambda b, t: (b, t, 0))
    val_spec = pl.BlockSpec((1, TOKEN_TILE, BRANCH), lambda b, t: (b, t, 0))
    key_shape = jax.ShapeDtypeStruct((B, L, GLA_KEY_W), _BF16)
    val_shape = jax.ShapeDtypeStruct((B, L, BRANCH), _BF16)
    const = lambda shape: pl.BlockSpec(shape, lambda b, t: (0,) * len(shape))
    rope_spec = pl.BlockSpec((TOKEN_TILE, GLA_DK), lambda b, t: (jnp.maximum(t - 1, 0), 0))
    return pl.pallas_call(
        _gla_proj_kernel,
        grid=(B, n_t),
        in_specs=[
            pl.BlockSpec((1, TOKEN_TILE, D_MODEL), lambda b, t: (b, t, 0)),
            _mod_spec(),
            const((D_MODEL, n_w)),
            const((LOW_PAD, 2 * GLA_KEY_W)),
            const((1, 2 * GLA_KEY_W)),
            rope_spec, rope_spec,
            const((2 * TOKEN_TILE, TOKEN_TILE)),
            const((2 * TOKEN_TILE, TOKEN_TILE)),
        ],
        out_specs=[key_spec] * 6 + [val_spec] * 2 + [
            pl.BlockSpec((1, 1, 2 * CHUNKS_PER_TILE, GLA_KEY_W), lambda b, t: (b, t, 0, 0))],
        out_shape=[key_shape] * 6 + [val_shape] * 2 + [
            jax.ShapeDtypeStruct((B, n_t, 2 * CHUNKS_PER_TILE, GLA_KEY_W), _F32)],
        compiler_params=pltpu.CompilerParams(
            dimension_semantics=("arbitrary", "arbitrary"), vmem_limit_bytes=VMEM_LIMIT),
        name="gla_project",
    )(xs, mod_i, w_cat, w2_cat, dec_b, cos_t, sin_t, tri_f, tri_b)


GLA_HEADS_PER_STEP = 2
GLA_CHUNKS_PER_STEP = 2


def _gla_mix_kernel(qf_ref, kf_ref, kef_ref, qb_ref, kb_ref, keb_ref, v_ref, dec_ref, ng_ref,
                    out_ref, accf_ref, accb_ref, sf_ref, sb_ref):
    n_chunks = accf_ref.shape[0] // GLA_CHUNK
    n_ctx = CTX_LEN // GLA_CHUNK
    C = GLA_CHUNK
    sf_ref[...] = jnp.zeros_like(sf_ref)
    sb_ref[...] = jnp.zeros_like(sb_ref)
    ti = lax.broadcasted_iota(jnp.int32, (C, C), 0)
    si = lax.broadcasted_iota(jnp.int32, (C, C), 1)
    mask_f = si <= ti
    mask_b = si > ti
    sub8 = lax.broadcasted_iota(jnp.int32, (2 * CHUNKS_PER_TILE, GLA_DK), 0)

    fwd = (qf_ref, kf_ref, kef_ref, sf_ref, accf_ref, mask_f, 0)
    bwd = (qb_ref, kb_ref, keb_ref, sb_ref, accb_ref, mask_b, CHUNKS_PER_TILE)

    def body(i, carry):
        chains = []
        for u in range(GLA_CHUNKS_PER_STEP):
            cf = i * GLA_CHUNKS_PER_STEP + u
            cb = jnp.where(cf < n_ctx, n_ctx - 1 - cf, n_chunks + n_ctx - 1 - cf)
            for c, (q_ref, k_ref, ke_ref, s_ref, acc_ref, mask, dec_row) in ((cf, fwd), (cb, bwd)):
                for h in range(GLA_HEADS_PER_STEP):
                    chains.append(dict(u=u, c=c, h=h, q_ref=q_ref, k_ref=k_ref, ke_ref=ke_ref, s_ref=s_ref,
                                       acc_ref=acc_ref, mask=mask, dec_row=dec_row))
        for ch in chains:
            c, h = ch["c"], ch["h"]
            ch["rows"] = pl.ds(pl.multiple_of(c * C, C), C)
            ch["ksl"] = slice(h * GLA_DK, (h + 1) * GLA_DK)
            ch["vsl"] = slice(h * GLA_DV, (h + 1) * GLA_DV)
            ch["q"] = ch["q_ref"][0, ch["rows"], ch["ksl"]]
            ch["v"] = v_ref[0, ch["rows"], ch["vsl"]]
            ch["att"] = _dot_nt(ch["q"], ch["k_ref"][0, ch["rows"], ch["ksl"]])
        for ch in chains:
            ch["kv"] = _dot_tn(ch["ke_ref"][0, ch["rows"], ch["ksl"]], ch["v"])
            tile = lax.shift_right_logical(ch["c"], CHUNKS_PER_TILE.bit_length() - 1)
            dec8 = dec_ref[0, tile, :, ch["ksl"]]
            pick = sub8 == ch["dec_row"] + (ch["c"] & (CHUNKS_PER_TILE - 1))
            dec = jnp.sum(jnp.where(pick, dec8, 0.0), axis=0, keepdims=True)
            dcol = jnp.broadcast_to(dec, (GLA_DK, GLA_DK)).T
            ch["decay"] = jnp.concatenate([dcol, dcol], axis=1)
        state = {}
        for u in range(GLA_CHUNKS_PER_STEP):
            for ch in chains:
                if ch["u"] != u:
                    continue
                key = (id(ch["s_ref"]), ch["h"])
                s = state[key] if u else ch["s_ref"][ch["h"]]
                att = jnp.where(ch["mask"], ch["att"], 0.0).astype(_BF16)
                ch["acc_ref"][ch["rows"], ch["vsl"]] = _dot(att, ch["v"]) + _dot(ch["q"], s.astype(_BF16))
                state[key] = s * ch["decay"] + ch["kv"]
        for ch in chains:
            if ch["u"] == GLA_CHUNKS_PER_STEP - 1:
                ch["s_ref"][ch["h"]] = state[(id(ch["s_ref"]), ch["h"])]
        return carry

    lax.fori_loop(0, n_chunks // GLA_CHUNKS_PER_STEP, body, 0)

    def norm_body(j, carry):
        rows = pl.ds(pl.multiple_of(j * TOKEN_TILE, TOKEN_TILE), TOKEN_TILE)
        for h in range(GLA_HEADS_PER_STEP):
            vsl = slice(h * GLA_DV, (h + 1) * GLA_DV)
            o = accf_ref[rows, vsl] + accb_ref[rows, vsl]
            ms = jnp.mean(o * o, axis=-1, keepdims=True)
            out_ref[0, rows, vsl] = (o * lax.rsqrt(ms + NORM_EPS) * ng_ref[...]).astype(_BF16)
        return carry

    lax.fori_loop(0, accf_ref.shape[0] // TOKEN_TILE, norm_body, 0)


def _gla_mix(qf, kf, kef, qb, kb, keb, v, dec, norm_g):
    B, L, _ = qf.shape
    hs = GLA_HEADS_PER_STEP
    n_t = dec.shape[1]
    key_spec = pl.BlockSpec((1, L, hs * GLA_DK), lambda b, p: (b, 0, p))
    val_spec = pl.BlockSpec((1, L, hs * GLA_DV), lambda b, p: (b, 0, p))
    return pl.pallas_call(
        _gla_mix_kernel,
        grid=(B, GLA_HEADS // hs),
        in_specs=[key_spec] * 6 + [
            val_spec,
            pl.BlockSpec((1, n_t, 2 * CHUNKS_PER_TILE, hs * GLA_DK), lambda b, p: (b, 0, 0, p)),
            pl.BlockSpec((1, GLA_DV), lambda b, p: (0, 0)),
        ],
        out_specs=val_spec,
        out_shape=jax.ShapeDtypeStruct((B, L, BRANCH), _BF16),
        scratch_shapes=[
            pltpu.VMEM((L, hs * GLA_DV), _F32),
            pltpu.VMEM((L, hs * GLA_DV), _F32),
            pltpu.VMEM((hs, GLA_DK, GLA_DV), _F32),
            pltpu.VMEM((hs, GLA_DK, GLA_DV), _F32),
        ],
        compiler_params=pltpu.CompilerParams(
            dimension_semantics=("arbitrary", "arbitrary"), vmem_limit_bytes=VMEM_LIMIT),
        name="gla_mix",
    )(qf, kf, kef, qb, kb, keb, v, dec, norm_g.reshape(1, GLA_DV))


def _na_proj_kernel(x_ref, mod_ref, w_ref, q_ref, k_ref, v_ref, g_ref):
    hb = _modulated_tile(x_ref, mod_ref)
    q_ref[0] = (_dot(hb, w_ref[:, 0:BRANCH]) * (NA_DH ** -0.5)).astype(_BF16)
    k_ref[0] = _dot(hb, w_ref[:, BRANCH:2 * BRANCH]).astype(_BF16)
    v_ref[0] = _dot(hb, w_ref[:, 2 * BRANCH:3 * BRANCH]).astype(_BF16)
    g_ref[0] = _dot(hb, w_ref[:, 3 * BRANCH:4 * BRANCH]).astype(_BF16)


def _na_project(xs, mod_i, w_in):
    B, L, _ = xs.shape
    spec = pl.BlockSpec((1, TOKEN_TILE, BRANCH), lambda b, t: (b, t, 0))
    shape = jax.ShapeDtypeStruct((B, L, BRANCH), _BF16)
    return pl.pallas_call(
        _na_proj_kernel,
        grid=(B, L // TOKEN_TILE),
        in_specs=[
            pl.BlockSpec((1, TOKEN_TILE, D_MODEL), lambda b, t: (b, t, 0)),
            _mod_spec(),
            pl.BlockSpec((D_MODEL, 4 * BRANCH), lambda b, t: (0, 0)),
        ],
        out_specs=[spec] * 4,
        out_shape=[shape] * 4,
        compiler_params=pltpu.CompilerParams(
            dimension_semantics=("arbitrary", "arbitrary"), vmem_limit_bytes=VMEM_LIMIT),
        name="na_project",
    )(xs, mod_i, w_in)


NA_ROWS = 2048 // GRID_W
NA_PAIR_W = 2 * NA_DH
NA_BAND = NA_KH * GRID_W
NA_ROWS_PER_STEP = 4


def _stack_heads(q2):
    lane = lax.broadcasted_iota(jnp.int32, q2.shape, 1)
    zero = jnp.zeros_like(q2)
    return jnp.concatenate([jnp.where(lane < NA_DH, q2, zero), jnp.where(lane >= NA_DH, q2, zero)], axis=0)


def _unstack_heads(o, n):
    lane = lax.broadcasted_iota(jnp.int32, (n, NA_PAIR_W), 1)
    return jnp.where(lane < NA_DH, o[0:n], o[n:2 * n])


def _na_mix_kernel(q_ref, k_ref, v_ref, bias_ref, out_ref):
    kc = k_ref[0, 0:CTX_LEN, :]
    vc = v_ref[0, 0:CTX_LEN, :]

    qs = _stack_heads(q_ref[0, 0:CTX_LEN, :])
    s = _dot_nt(qs, kc)
    m = jnp.max(s, axis=-1, keepdims=True)
    p = jnp.exp(s - m)
    l = jnp.sum(p, axis=-1, keepdims=True)
    o = _dot(p.astype(_BF16), vc) / l
    out_ref[0, 0:CTX_LEN, :] = _unstack_heads(o, CTX_LEN).astype(_BF16)

    U = NA_ROWS_PER_STEP
    M = 2 * GRID_W

    def rows_body(i, carry):
        r0 = i * U
        q_rows, bands, qss, s_locs = [], [], [], []
        for u in range(U):
            r = r0 + u
            r_start = jnp.clip(r - NA_KH // 2, 0, NA_ROWS - NA_KH)
            q_rows.append(pl.ds(pl.multiple_of(CTX_LEN + r * GRID_W, GRID_W), GRID_W))
            bands.append(pl.ds(pl.multiple_of(CTX_LEN + r_start * GRID_W, GRID_W), NA_BAND))
            qss.append(_stack_heads(q_ref[0, q_rows[u], :]))
            s_locs.append(_dot_nt(qss[u], k_ref[0, bands[u], :]) + bias_ref[0, r - r_start])
        s_ctx_all = _dot_nt(jnp.concatenate(qss, axis=0), kc)
        p_locs, p_ctxs, ls = [], [], []
        for u in range(U):
            s_loc = s_locs[u]
            s_ctx = s_ctx_all[u * M:(u + 1) * M]
            m = jnp.maximum(jnp.max(s_loc, axis=-1, keepdims=True), jnp.max(s_ctx, axis=-1, keepdims=True))
            p_loc = jnp.exp(s_loc - m)
            p_ctx = jnp.exp(s_ctx - m)
            ls.append(jnp.sum(p_loc, axis=-1, keepdims=True) + jnp.sum(p_ctx, axis=-1, keepdims=True))
            p_locs.append(p_loc.astype(_BF16))
            p_ctxs.append(p_ctx.astype(_BF16))
        o_ctx_all = _dot(jnp.concatenate(p_ctxs, axis=0), vc)
        for u in range(U):
            o = (_dot(p_locs[u], v_ref[0, bands[u], :]) + o_ctx_all[u * M:(u + 1) * M]) / ls[u]
            out_ref[0, q_rows[u], :] = _unstack_heads(o, GRID_W).astype(_BF16)
        return carry

    lax.fori_loop(0, NA_ROWS // U, rows_body, 0)


def _na_bias_table(rpb):
    col = np.arange(GRID_W)
    col_start = np.clip(col - NA_KW // 2, 0, GRID_W - NA_KW)
    kcol = np.arange(GRID_W)
    valid = (kcol[None, :] >= col_start[:, None]) & (kcol[None, :] < col_start[:, None] + NA_KW)
    col_off = kcol[None, :] - col[:, None] + (NA_KW - 1)
    onehot = (col_off[None] == np.arange(2 * NA_KW - 1)[:, None, None]) & valid[None]
    toe = jnp.einsum('hro,ock->hrck', rpb, jnp.asarray(onehot, _F32), precision=lax.Precision.HIGHEST)
    toe = jnp.where(valid[None, None], toe, NEG_BIG)
    per_class = []
    for d in range(NA_KH):
        band = toe[:, NA_KH - 1 - d:2 * NA_KH - 1 - d]
        per_class.append(jnp.moveaxis(band, 1, 2).reshape(NA_HEADS, GRID_W, NA_BAND))
    g = jnp.stack(per_class, axis=1)
    g = g.reshape(NA_HEADS // 2, 2, NA_KH, GRID_W, NA_BAND)
    return jnp.moveaxis(g, 1, 2).reshape(NA_HEADS // 2, NA_KH, 2 * GRID_W, NA_BAND)


def _na_mix(q, k, v, bias):
    B, L, _ = q.shape
    n_pairs = NA_HEADS // 2
    spec = pl.BlockSpec((1, L, NA_PAIR_W), lambda p, b: (b, 0, p))
    return pl.pallas_call(
        _na_mix_kernel,
        grid=(n_pairs, B),
        in_specs=[spec, spec, spec,
                  pl.BlockSpec((1, NA_KH, 2 * GRID_W, NA_BAND), lambda p, b: (p, 0, 0, 0))],
        out_specs=spec,
        out_shape=jax.ShapeDtypeStruct((B, L, BRANCH), _BF16),
        compiler_params=pltpu.CompilerParams(
            dimension_semantics=("arbitrary", "arbitrary"), vmem_limit_bytes=VMEM_LIMIT),
        name="na_mix",
    )(q, k, v, bias)


def _merge_kernel(o_ref, g_ref, x_ref, mod_ref, w_ref, lng_ref, lnb_ref, out_ref):
    gate = mod_ref[0][:, 2 * D_MODEL:3 * D_MODEL]
    a = (o_ref[0].astype(_F32) * _silu(g_ref[0].astype(_F32))).astype(_BF16)
    y = _dot(a, w_ref[...])
    z = ALPHA * x_ref[0] + gate * y
    mu = jnp.mean(z, axis=-1, keepdims=True)
    zc = z - mu
    var = jnp.mean(zc * zc, axis=-1, keepdims=True)
    out_ref[0] = zc * lax.rsqrt(var + LN_EPS) * lng_ref[...] + lnb_ref[...]


def _merge(o, g, xs, mod_i, w_out, ln_g, ln_b, latent_only):
    B, L, _ = xs.shape
    off = 1 if latent_only else 0
    n_t = L // TOKEN_TILE - off
    tile = lambda b, t: (b, t + off, 0)
    return pl.pallas_call(
        _merge_kernel,
        grid=(B, n_t),
        in_specs=[
            pl.BlockSpec((1, TOKEN_TILE, BRANCH), tile),
            pl.BlockSpec((1, TOKEN_TILE, BRANCH), tile),
            pl.BlockSpec((1, TOKEN_TILE, D_MODEL), tile),
            _mod_spec(off),
            pl.BlockSpec((BRANCH, D_MODEL), lambda b, t: (0, 0)),
            pl.BlockSpec((1, D_MODEL), lambda b, t: (0, 0)),
            pl.BlockSpec((1, D_MODEL), lambda b, t: (0, 0)),
        ],
        out_specs=pl.BlockSpec((1, TOKEN_TILE, D_MODEL), lambda b, t: (b, t, 0)),
        out_shape=jax.ShapeDtypeStruct((B, n_t * TOKEN_TILE, D_MODEL), _F32),
        compiler_params=pltpu.CompilerParams(
            dimension_semantics=("arbitrary", "arbitrary"), vmem_limit_bytes=VMEM_LIMIT),
        name="merge",
    )(o, g, xs, mod_i, w_out, ln_g.reshape(1, D_MODEL), ln_b.reshape(1, D_MODEL))


def _rope_tables(n_tokens):
    pos = jnp.arange(n_tokens, dtype=jnp.int32)
    row = (pos // GRID_W).astype(_F32)
    col = (pos % GRID_W).astype(_F32)
    quarter = GLA_DK // 4
    inv = ROPE_BASE ** (-jnp.arange(quarter, dtype=_F32) / quarter)
    ang = jnp.concatenate([row[:, None] * inv, col[:, None] * inv], -1)
    cos, sin = jnp.cos(ang), jnp.sin(ang)
    return jnp.concatenate([cos, cos], -1), jnp.concatenate([-sin, sin], -1)


def kernel(x, c, ctx, c_ctx, ada_w, ada_b, ln_g, ln_b, w_out, gla_w_in, gla_dec_w1, gla_dec_w2,
           gla_dec_b, gla_norm_g, na_w_in, na_rpb):
    B, S, D = x.shape
    assert (B, S, D) == (8, 2048, D_MODEL) and ctx.shape == (B, CTX_LEN, D)
    cond = jnp.zeros((COND_ROWS, D), _F32).at[0:B].set(c).at[CTX_COND_ROW].set(c_ctx)
    mod = _modulation(cond, ada_w, ada_b).reshape(DEPTH, COND_ROWS, 1, 3 * D_MODEL)
    cos_t, sin_t = _rope_tables(S)
    xs = jnp.concatenate([ctx, x], axis=1)

    for i in range(DEPTH):
        j = i // N_MIXERS
        last = i == DEPTH - 1
        w_o = w_out[i].astype(_BF16)
        if i % N_MIXERS == 0:
            w1 = jnp.concatenate([gla_dec_w1[j, 0], gla_dec_w1[j, 1]], axis=1)
            w1 = jnp.pad(w1, ((0, 0), (0, LOW_PAD - 2 * GLA_GATE_RANK)))
            w_cat = jnp.concatenate([gla_w_in[j], w1], axis=1).astype(_BF16)
            w2_cat = jnp.zeros((LOW_PAD, 2 * GLA_KEY_W), _F32)
            w2_cat = w2_cat.at[0:GLA_GATE_RANK, 0:GLA_KEY_W].set(gla_dec_w2[j, 0])
            w2_cat = w2_cat.at[GLA_GATE_RANK:2 * GLA_GATE_RANK, GLA_KEY_W:].set(gla_dec_w2[j, 1])
            dec_b = gla_dec_b[j].reshape(1, 2 * GLA_KEY_W)
            qf, kf, kef, qb, kb, keb, v, g, dec = _gla_project(
                xs, mod[i], w_cat, w2_cat.astype(_BF16), dec_b, cos_t, sin_t)
            o = _gla_mix(qf, kf, kef, qb, kb, keb, v, dec, gla_norm_g[j])
        else:
            q, k, v, g = _na_project(xs, mod[i], na_w_in[j].astype(_BF16))
            o = _na_mix(q, k, v, _na_bias_table(na_rpb[j]))
        xs = _merge(o, g, xs, mod[i], w_o, ln_g[i], ln_b[i], latent_only=last)
    return xs
```

```python
import functools

import numpy as np
import jax
import jax.numpy as jnp
from jax import lax
from jax.experimental import pallas as pl
from jax.experimental.pallas import tpu as pltpu

D_MODEL = 1024
DEPTH = 4
GRID_W = 64
CTX_LEN = 256
N_MIXERS = 2
BRANCH = D_MODEL
GLA_HEADS = 4
GLA_DK = 128
GLA_DV = 256
GLA_KEY_W = GLA_HEADS * GLA_DK
GLA_GATE_RANK = 16
GLA_TAU = 16.0
GLA_CHUNK = 64
NA_HEADS = 16
NA_DH = 64
NA_KH = 8
NA_KW = 16
ROPE_BASE = 10000.0
LN_EPS = 1e-5
NORM_EPS = 1e-6
ALPHA = (2 * DEPTH) ** 0.25

TOKEN_TILE = 256
CHUNKS_PER_TILE = TOKEN_TILE // GLA_CHUNK
COND_ROWS = 16
CTX_COND_ROW = 8
LOW_PAD = 128
NEG_BIG = -1e30
LOG2E = 1.4426950408889634
VMEM_LIMIT = 56 * 1024 * 1024

_F32 = jnp.float32
_BF16 = jnp.bfloat16


def _dot(a, b):
    return jnp.dot(a, b, preferred_element_type=_F32)


def _dot_nt(a, b):
    return lax.dot_general(a, b, (((1,), (1,)), ((), ())), preferred_element_type=_F32)


def _dot_tn(a, b):
    return lax.dot_general(a, b, (((0,), (0,)), ((), ())), preferred_element_type=_F32)


def _silu(x):
    return x * (1.0 / (1.0 + jnp.exp(-x)))


def _split_bf16(x):
    hi = x.astype(_BF16)
    lo = (x - hi.astype(_F32)).astype(_BF16)
    return hi, lo


def _mod_kernel(cond_ref, w_ref, b_ref, out_ref):
    s = _silu(cond_ref[...]).astype(_BF16)
    out_ref[0] = _dot(s, w_ref[0].astype(_BF16)) + b_ref[0]


def _modulation(cond, ada_w, ada_b):
    n_col = 3 * D_MODEL // D_MODEL
    return pl.pallas_call(
        _mod_kernel,
        grid=(DEPTH, n_col),
        in_specs=[
            pl.BlockSpec((COND_ROWS, D_MODEL), lambda i, j: (0, 0)),
            pl.BlockSpec((1, D_MODEL, D_MODEL), lambda i, j: (i, 0, j)),
            pl.BlockSpec((1, 1, D_MODEL), lambda i, j: (i, 0, j)),
        ],
        out_specs=pl.BlockSpec((1, COND_ROWS, D_MODEL), lambda i, j: (i, 0, j)),
        out_shape=jax.ShapeDtypeStruct((DEPTH, COND_ROWS, 3 * D_MODEL), _F32),
        compiler_params=pltpu.CompilerParams(
            dimension_semantics=("arbitrary", "arbitrary"), vmem_limit_bytes=VMEM_LIMIT),
        name="modulation",
    )(cond, ada_w, ada_b.reshape(DEPTH, 1, 3 * D_MODEL))


def _mod_spec(tile_offset=0):
    return pl.BlockSpec((1, 1, 3 * D_MODEL),
                        lambda b, t: (jnp.where(t + tile_offset == 0, CTX_COND_ROW, b), 0, 0))


def _modulated_tile(x_ref, mod_ref):
    m = mod_ref[0]
    shift = m[:, 0:D_MODEL]
    scale = m[:, D_MODEL:2 * D_MODEL]
    return (x_ref[0] * (1.0 + scale) + shift).astype(_BF16)


def _gla_proj_kernel(x_ref, mod_ref, w_ref, w2_ref, decb_ref, cos_ref, sin_ref, tri_ref,
                     qf_ref, kf_ref, kef_ref, qb_ref, kb_ref, keb_ref, v_ref, g_ref, dec_ref):
    t = pl.program_id(1)
    hb = _modulated_tile(x_ref, mod_ref)
    kw = GLA_KEY_W
    q = _dot(hb, w_ref[:, 0:kw]) * (GLA_DK ** -0.5)
    k = _dot(hb, w_ref[:, kw:2 * kw])
    v_ref[0] = _dot(hb, w_ref[:, 2 * kw:2 * kw + BRANCH]).astype(_BF16)
    g_ref[0] = _dot(hb, w_ref[:, 2 * kw + BRANCH:2 * kw + 2 * BRANCH]).astype(_BF16)
    low = _dot(hb, w_ref[:, 2 * kw + 2 * BRANCH:2 * kw + 2 * BRANCH + LOW_PAD])
    logits = _dot(low.astype(_BF16), w2_ref[...]) + decb_ref[...]
    la = (jnp.minimum(logits, 0.0) - jnp.log1p(jnp.exp(-jnp.abs(logits)))) * (1.0 / GLA_TAU)
    la_hi, la_lo = _split_bf16(la)
    pre = _dot(tri_ref[...], la_hi) + _dot(tri_ref[...], la_lo)
    pre3 = pre.reshape(CHUNKS_PER_TILE, GLA_CHUNK, 2 * kw)
    tot4 = pre3[:, GLA_CHUNK - 1:GLA_CHUNK, :]
    tot = jnp.broadcast_to(tot4, pre3.shape).reshape(TOKEN_TILE, 2 * kw)
    pre_f, pre_b = pre[:, 0:kw], pre[:, kw:2 * kw]
    tot_f, tot_b = tot[:, 0:kw], tot[:, kw:2 * kw]
    ante_b = pre_b - la[:, kw:2 * kw]

    e_qf = jnp.exp(pre_f)
    e_kf = jnp.exp(-pre_f)
    e_kef = jnp.exp(tot_f - pre_f)
    e_qb = jnp.exp(tot_b - pre_b)
    e_kb = jnp.exp(ante_b - tot_b)
    e_keb = jnp.exp(ante_b)

    is_lat = t > 0
    cos = jnp.where(is_lat, cos_ref[...], 1.0)
    sin = jnp.where(is_lat, sin_ref[...], 0.0)
    for h in range(GLA_HEADS):
        sl = slice(h * GLA_DK, (h + 1) * GLA_DK)
        qh = q[:, sl]
        kh = k[:, sl]
        qh = qh * cos + pltpu.roll(qh, GLA_DK // 2, 1) * sin
        kh = kh * cos + pltpu.roll(kh, GLA_DK // 2, 1) * sin
        qf_ref[0, :, sl] = (qh * e_qf[:, sl]).astype(_BF16)
        kf_ref[0, :, sl] = (kh * e_kf[:, sl]).astype(_BF16)
        kef_ref[0, :, sl] = (kh * e_kef[:, sl]).astype(_BF16)
        qb_ref[0, :, sl] = (qh * e_qb[:, sl]).astype(_BF16)
        kb_ref[0, :, sl] = (kh * e_kb[:, sl]).astype(_BF16)
        keb_ref[0, :, sl] = (kh * e_keb[:, sl]).astype(_BF16)

    tot4 = tot4.reshape(CHUNKS_PER_TILE, 2 * kw)
    dec_ref[0, 0] = jnp.exp(jnp.concatenate([tot4[:, 0:kw], tot4[:, kw:2 * kw]], axis=0))


def _chunk_prefix_matrix():
    i = np.arange(TOKEN_TILE)
    same = (i[:, None] // GLA_CHUNK) == (i[None, :] // GLA_CHUNK)
    return jnp.asarray(same & (i[None, :] <= i[:, None]), _BF16)


def _gla_project(xs, mod_i, w_cat, w2_cat, dec_b, cos_t, sin_t):
    B, L, _ = xs.shape
    n_t = L // TOKEN_TILE
    n_w = w_cat.shape[1]
    tri = _chunk_prefix_matrix()
    key_spec = pl.BlockSpec((1, TOKEN_TILE, GLA_KEY_W), lambda b, t: (b, t, 0))
    val_spec = pl.BlockSpec((1, TOKEN_TILE, BRANCH), lambda b, t: (b, t, 0))
    key_shape = jax.ShapeDtypeStruct((B, L, GLA_KEY_W), _BF16)
    val_shape = jax.ShapeDtypeStruct((B, L, BRANCH), _BF16)
    const = lambda shape: pl.BlockSpec(shape, lambda b, t: (0,) * len(shape))
    rope_spec = pl.BlockSpec((TOKEN_TILE, GLA_DK), lambda b, t: (jnp.maximum(t - 1, 0), 0))
    return pl.pallas_call(
        _gla_proj_kernel,
        grid=(B, n_t),
        in_specs=[
            pl.BlockSpec((1, TOKEN_TILE, D_MODEL), lambda b, t: (b, t, 0)),
            _mod_spec(),
            const((D_MODEL, n_w)),
            const((LOW_PAD, 2 * GLA_KEY_W)),
            const((1, 2 * GLA_KEY_W)),
            rope_spec, rope_spec,
            const((TOKEN_TILE, TOKEN_TILE)),
        ],
        out_specs=[key_spec] * 6 + [val_spec] * 2 + [
            pl.BlockSpec((1, 1, 2 * CHUNKS_PER_TILE, GLA_KEY_W), lambda b, t: (b, t, 0, 0))],
        out_shape=[key_shape] * 6 + [val_shape] * 2 + [
            jax.ShapeDtypeStruct((B, n_t, 2 * CHUNKS_PER_TILE, GLA_KEY_W), _F32)],
        compiler_params=pltpu.CompilerParams(
            dimension_semantics=("arbitrary", "arbitrary"), vmem_limit_bytes=VMEM_LIMIT),
        name="gla_project",
    )(xs, mod_i, w_cat, w2_cat, dec_b, cos_t, sin_t, tri)


GLA_HEADS_PER_STEP = 2
GLA_CHUNKS_PER_STEP = 2


def _gla_mix_kernel(qf_ref, kf_ref, kef_ref, qb_ref, kb_ref, keb_ref, v_ref, dec_ref, ng_ref,
                    out_ref, accf_ref, accb_ref, sf_ref, sb_ref):
    n_chunks = accf_ref.shape[0] // GLA_CHUNK
    n_ctx = CTX_LEN // GLA_CHUNK
    C = GLA_CHUNK
    sf_ref[...] = jnp.zeros_like(sf_ref)
    sb_ref[...] = jnp.zeros_like(sb_ref)
    ti = lax.broadcasted_iota(jnp.int32, (C, C), 0)
    si = lax.broadcasted_iota(jnp.int32, (C, C), 1)
    mask_f = si <= ti
    mask_b = si > ti
    sub8 = lax.broadcasted_iota(jnp.int32, (2 * CHUNKS_PER_TILE, GLA_DK), 0)

    fwd = (qf_ref, kf_ref, kef_ref, sf_ref, accf_ref, mask_f, 0)
    bwd = (qb_ref, kb_ref, keb_ref, sb_ref, accb_ref, mask_b, CHUNKS_PER_TILE)

    def body(i, carry):
        chains = []
        for u in range(GLA_CHUNKS_PER_STEP):
            cf = i * GLA_CHUNKS_PER_STEP + u
            cb = jnp.where(cf < n_ctx, n_ctx - 1 - cf, n_chunks + n_ctx - 1 - cf)
            for c, (q_ref, k_ref, ke_ref, s_ref, acc_ref, mask, dec_row) in ((cf, fwd), (cb, bwd)):
                for h in range(GLA_HEADS_PER_STEP):
                    chains.append(dict(u=u, c=c, h=h, q_ref=q_ref, k_ref=k_ref, ke_ref=ke_ref, s_ref=s_ref,
                                       acc_ref=acc_ref, mask=mask, dec_row=dec_row))
        for ch in chains:
            c, h = ch["c"], ch["h"]
            ch["rows"] = pl.ds(pl.multiple_of(c * C, C), C)
            ch["ksl"] = slice(h * GLA_DK, (h + 1) * GLA_DK)
            ch["vsl"] = slice(h * GLA_DV, (h + 1) * GLA_DV)
            ch["q"] = ch["q_ref"][0, ch["rows"], ch["ksl"]]
            ch["v"] = v_ref[0, ch["rows"], ch["vsl"]]
            ch["att"] = _dot_nt(ch["q"], ch["k_ref"][0, ch["rows"], ch["ksl"]])
        for ch in chains:
            ch["kv"] = _dot_tn(ch["ke_ref"][0, ch["rows"], ch["ksl"]], ch["v"])
            tile = lax.shift_right_logical(ch["c"], CHUNKS_PER_TILE.bit_length() - 1)
            dec8 = dec_ref[0, tile, :, ch["ksl"]]
            pick = sub8 == ch["dec_row"] + (ch["c"] & (CHUNKS_PER_TILE - 1))
            dec = jnp.sum(jnp.where(pick, dec8, 0.0), axis=0, keepdims=True)
            dcol = jnp.broadcast_to(dec, (GLA_DK, GLA_DK)).T
            ch["decay"] = jnp.concatenate([dcol, dcol], axis=1)
        state = {}
        for u in range(GLA_CHUNKS_PER_STEP):
            for ch in chains:
                if ch["u"] != u:
                    continue
                key = (id(ch["s_ref"]), ch["h"])
                s = state[key] if u else ch["s_ref"][ch["h"]]
                att = jnp.where(ch["mask"], ch["att"], 0.0).astype(_BF16)
                ch["acc_ref"][ch["rows"], ch["vsl"]] = _dot(att, ch["v"]) + _dot(ch["q"], s.astype(_BF16))
                state[key] = s * ch["decay"] + ch["kv"]
        for ch in chains:
            if ch["u"] == GLA_CHUNKS_PER_STEP - 1:
                ch["s_ref"][ch["h"]] = state[(id(ch["s_ref"]), ch["h"])]
        return carry

    lax.fori_loop(0, n_chunks // GLA_CHUNKS_PER_STEP, body, 0)

    def norm_body(j, carry):
        rows = pl.ds(pl.multiple_of(j * TOKEN_TILE, TOKEN_TILE), TOKEN_TILE)
        for h in range(GLA_HEADS_PER_STEP):
            vsl = slice(h * GLA_DV, (h + 1) * GLA_DV)
            o = accf_ref[rows, vsl] + accb_ref[rows, vsl]
            ms = jnp.mean(o * o, axis=-1, keepdims=True)
            out_ref[0, rows, vsl] = (o * lax.rsqrt(ms + NORM_EPS) * ng_ref[...]).astype(_BF16)
        return carry

    lax.fori_loop(0, accf_ref.shape[0] // TOKEN_TILE, norm_body, 0)


def _gla_mix(qf, kf, kef, qb, kb, keb, v, dec, norm_g):
    B, L, _ = qf.shape
    hs = GLA_HEADS_PER_STEP
    n_t = dec.shape[1]
    key_spec = pl.BlockSpec((1, L, hs * GLA_DK), lambda b, p: (b, 0, p))
    val_spec = pl.BlockSpec((1, L, hs * GLA_DV), lambda b, p: (b, 0, p))
    return pl.pallas_call(
        _gla_mix_kernel,
        grid=(B, GLA_HEADS // hs),
        in_specs=[key_spec] * 6 + [
            val_spec,
            pl.BlockSpec((1, n_t, 2 * CHUNKS_PER_TILE, hs * GLA_DK), lambda b, p: (b, 0, 0, p)),
            pl.BlockSpec((1, GLA_DV), lambda b, p: (0, 0)),
        ],
        out_specs=val_spec,
        out_shape=jax.ShapeDtypeStruct((B, L, BRANCH), _BF16),
        scratch_shapes=[
            pltpu.VMEM((L, hs * GLA_DV), _F32),
            pltpu.VMEM((L, hs * GLA_DV), _F32),
            pltpu.VMEM((hs, GLA_DK, GLA_DV), _F32),
            pltpu.VMEM((hs, GLA_DK, GLA_DV), _F32),
        ],
        compiler_params=pltpu.CompilerParams(
            dimension_semantics=("arbitrary", "arbitrary"), vmem_limit_bytes=VMEM_LIMIT),
        name="gla_mix",
    )(qf, kf, kef, qb, kb, keb, v, dec, norm_g.reshape(1, GLA_DV))


def _na_proj_kernel(x_ref, mod_ref, w_ref, q_ref, k_ref, v_ref, g_ref):
    hb = _modulated_tile(x_ref, mod_ref)
    q_ref[0] = (_dot(hb, w_ref[:, 0:BRANCH]) * (NA_DH ** -0.5 * LOG2E)).astype(_BF16)
    k_ref[0] = _dot(hb, w_ref[:, BRANCH:2 * BRANCH]).astype(_BF16)
    v_ref[0] = _dot(hb, w_ref[:, 2 * BRANCH:3 * BRANCH]).astype(_BF16)
    g_ref[0] = _dot(hb, w_ref[:, 3 * BRANCH:4 * BRANCH]).astype(_BF16)


def _na_project(xs, mod_i, w_in):
    B, L, _ = xs.shape
    spec = pl.BlockSpec((1, TOKEN_TILE, BRANCH), lambda b, t: (b, t, 0))
    shape = jax.ShapeDtypeStruct((B, L, BRANCH), _BF16)
    return pl.pallas_call(
        _na_proj_kernel,
        grid=(B, L // TOKEN_TILE),
        in_specs=[
            pl.BlockSpec((1, TOKEN_TILE, D_MODEL), lambda b, t: (b, t, 0)),
            _mod_spec(),
            pl.BlockSpec((D_MODEL, 4 * BRANCH), lambda b, t: (0, 0)),
        ],
        out_specs=[spec] * 4,
        out_shape=[shape] * 4,
        compiler_params=pltpu.CompilerParams(
            dimension_semantics=("arbitrary", "arbitrary"), vmem_limit_bytes=VMEM_LIMIT),
        name="na_project",
    )(xs, mod_i, w_in)


NA_ROWS = 2048 // GRID_W
NA_PAIR_W = 2 * NA_DH
NA_BAND = NA_KH * GRID_W
NA_ROWS_PER_STEP = 4


def _stack_heads(q2):
    lane = lax.broadcasted_iota(jnp.int32, q2.shape, 1)
    zero = jnp.zeros_like(q2)
    return jnp.concatenate([jnp.where(lane < NA_DH, q2, zero), jnp.where(lane >= NA_DH, q2, zero)], axis=0)


def _unstack_heads(o, n):
    lane = lax.broadcasted_iota(jnp.int32, (n, NA_PAIR_W), 1)
    return jnp.where(lane < NA_DH, o[0:n], o[n:2 * n])


def _na_mix_kernel(q_ref, k_ref, v_ref, bias_ref, out_ref, sl_a, sc_a, sl_b, sc_b, *, ctx_queries):
    kc = k_ref[0, 0:CTX_LEN, :]
    vc = v_ref[0, 0:CTX_LEN, :]

    if ctx_queries:
        qs = _stack_heads(q_ref[0, 0:CTX_LEN, :])
        s = _dot_nt(qs, kc)
        m = jnp.max(s, axis=-1, keepdims=True)
        p = jnp.exp2(s - m)
        l = jnp.sum(p, axis=-1, keepdims=True)
        o = _dot(p.astype(_BF16), vc) / l
        out_ref[0, 0:CTX_LEN, :] = _unstack_heads(o, CTX_LEN).astype(_BF16)
    else:
        out_ref[0, 0:CTX_LEN, :] = jnp.zeros((CTX_LEN, NA_PAIR_W), _BF16)

    U = NA_ROWS_PER_STEP
    M = 2 * GRID_W
    n_groups = NA_ROWS // U

    def row_slices(g, u):
        r = g * U + u
        r_start = jnp.clip(r - NA_KH // 2, 0, NA_ROWS - NA_KH)
        q_rows = pl.ds(pl.multiple_of(CTX_LEN + r * GRID_W, GRID_W), GRID_W)
        band = pl.ds(pl.multiple_of(CTX_LEN + r_start * GRID_W, GRID_W), NA_BAND)
        return q_rows, band, r - r_start

    def scores(g, sl_ref, sc_ref):
        qss = []
        for u in range(U):
            q_rows, band, row_class = row_slices(g, u)
            qss.append(_stack_heads(q_ref[0, q_rows, :]))
            sl_ref[u] = _dot_nt(qss[u], k_ref[0, band, :]) + bias_ref[0, row_class]
        sc_ref[...] = _dot_nt(jnp.concatenate(qss, axis=0), kc)

    def finish(g, sl_ref, sc_ref):
        p_locs, p_ctxs, ls = [], [], []
        for u in range(U):
            s_loc = sl_ref[u]
            s_ctx = sc_ref[u * M:(u + 1) * M, :]
            m = jnp.maximum(jnp.max(s_loc, axis=-1, keepdims=True), jnp.max(s_ctx, axis=-1, keepdims=True))
            p_loc = jnp.exp2(s_loc - m)
            p_ctx = jnp.exp2(s_ctx - m)
            ls.append(jnp.sum(p_loc, axis=-1, keepdims=True) + jnp.sum(p_ctx, axis=-1, keepdims=True))
            p_locs.append(p_loc.astype(_BF16))
            p_ctxs.append(p_ctx.astype(_BF16))
        o_ctx_all = _dot(jnp.concatenate(p_ctxs, axis=0), vc)
        for u in range(U):
            q_rows, band, _ = row_slices(g, u)
            o = (_dot(p_locs[u], v_ref[0, band, :]) + o_ctx_all[u * M:(u + 1) * M]) / ls[u]
            out_ref[0, q_rows, :] = _unstack_heads(o, GRID_W).astype(_BF16)

    scores(0, sl_a, sc_a)

    def pair_body(i, carry):
        g = 2 * i
        scores(g + 1, sl_b, sc_b)
        finish(g, sl_a, sc_a)
        scores(g + 2, sl_a, sc_a)
        finish(g + 1, sl_b, sc_b)
        return carry

    lax.fori_loop(0, n_groups // 2 - 1, pair_body, 0)
    scores(n_groups - 1, sl_b, sc_b)
    finish(n_groups - 2, sl_a, sc_a)
    finish(n_groups - 1, sl_b, sc_b)


RPB_ROWS_PAD = 16
LANES = 128


def _na_bias_kernel(rpb_ref, out_ref):
    c = lax.broadcasted_iota(jnp.int32, (GRID_W, LANES), 0)
    lane = lax.broadcasted_iota(jnp.int32, (GRID_W, LANES), 1)
    kc = lane & (GRID_W - 1)
    col_start = jnp.clip(c - NA_KW // 2, 0, GRID_W - NA_KW)
    valid = (kc >= col_start) & (kc < col_start + NA_KW)
    low_half = lane < GRID_W
    for a in range(2):
        toe = []
        for ro in range(2 * NA_KH - 1):
            w = jnp.broadcast_to(rpb_ref[0, a, ro:ro + 1, :], (GRID_W, LANES)) * LOG2E
            t_lo = pltpu.roll(w, LANES - (NA_KW - 1), 1, stride=1, stride_axis=0)
            t_hi = pltpu.roll(w, GRID_W - (NA_KW - 1), 1, stride=1, stride_axis=0)
            toe.append((t_lo, t_hi))
        for d in range(NA_KH):
            for jj in range(NA_KH // 2):
                ro = 2 * jj - d + NA_KH - 1
                tile = jnp.where(low_half, toe[ro][0], toe[ro + 1][1])
                out_ref[0, d, a * GRID_W:(a + 1) * GRID_W, jj * LANES:(jj + 1) * LANES] = (
                    jnp.where(valid, tile, NEG_BIG))


def _na_bias_table(rpb):
    n_pairs = NA_HEADS // 2
    n_ro, n_co = rpb.shape[1], rpb.shape[2]
    rpb = jnp.pad(rpb, ((0, 0), (0, RPB_ROWS_PAD - n_ro), (0, LANES - n_co)))
    return pl.pallas_call(
        _na_bias_kernel,
        grid=(n_pairs,),
        in_specs=[pl.BlockSpec((1, 2, RPB_ROWS_PAD, LANES), lambda p: (p, 0, 0, 0))],
        out_specs=pl.BlockSpec((1, NA_KH, 2 * GRID_W, NA_BAND), lambda p: (p, 0, 0, 0)),
        out_shape=jax.ShapeDtypeStruct((n_pairs, NA_KH, 2 * GRID_W, NA_BAND), _F32),
        compiler_params=pltpu.CompilerParams(dimension_semantics=("arbitrary",)),
        name="na_bias_table",
    )(rpb.reshape(n_pairs, 2, RPB_ROWS_PAD, LANES))


def _na_mix(q, k, v, bias, ctx_queries):
    B, L, _ = q.shape
    n_pairs = NA_HEADS // 2
    U = NA_ROWS_PER_STEP
    spec = pl.BlockSpec((1, L, NA_PAIR_W), lambda p, b: (b, 0, p))
    score_bufs = [pltpu.VMEM((U, 2 * GRID_W, NA_BAND), _F32), pltpu.VMEM((U * 2 * GRID_W, CTX_LEN), _F32)]
    return pl.pallas_call(
        functools.partial(_na_mix_kernel, ctx_queries=ctx_queries),
        grid=(n_pairs, B),
        in_specs=[spec, spec, spec,
                  pl.BlockSpec((1, NA_KH, 2 * GRID_W, NA_BAND), lambda p, b: (p, 0, 0, 0))],
        out_specs=spec,
        out_shape=jax.ShapeDtypeStruct((B, L, BRANCH), _BF16),
        scratch_shapes=score_bufs + score_bufs,
        compiler_params=pltpu.CompilerParams(
            dimension_semantics=("arbitrary", "arbitrary"), vmem_limit_bytes=VMEM_LIMIT),
        name="na_mix",
    )(q, k, v, bias)


def _merge_kernel(o_ref, g_ref, x_ref, mod_ref, w_ref, lng_ref, lnb_ref, out_ref):
    gate = mod_ref[0][:, 2 * D_MODEL:3 * D_MODEL]
    a = (o_ref[0].astype(_F32) * _silu(g_ref[0].astype(_F32))).astype(_BF16)
    y = _dot(a, w_ref[...])
    z = ALPHA * x_ref[0] + gate * y
    mu = jnp.mean(z, axis=-1, keepdims=True)
    zc = z - mu
    var = jnp.mean(zc * zc, axis=-1, keepdims=True)
    out_ref[0] = zc * lax.rsqrt(var + LN_EPS) * lng_ref[...] + lnb_ref[...]


def _merge(o, g, xs, mod_i, w_out, ln_g, ln_b, latent_only):
    B, L, _ = xs.shape
    off = 1 if latent_only else 0
    n_t = L // TOKEN_TILE - off
    tile = lambda b, t: (b, t + off, 0)
    return pl.pallas_call(
        _merge_kernel,
        grid=(B, n_t),
        in_specs=[
            pl.BlockSpec((1, TOKEN_TILE, BRANCH), tile),
            pl.BlockSpec((1, TOKEN_TILE, BRANCH), tile),
            pl.BlockSpec((1, TOKEN_TILE, D_MODEL), tile),
            _mod_spec(off),
            pl.BlockSpec((BRANCH, D_MODEL), lambda b, t: (0, 0)),
            pl.BlockSpec((1, D_MODEL), lambda b, t: (0, 0)),
            pl.BlockSpec((1, D_MODEL), lambda b, t: (0, 0)),
        ],
        out_specs=pl.BlockSpec((1, TOKEN_TILE, D_MODEL), lambda b, t: (b, t, 0)),
        out_shape=jax.ShapeDtypeStruct((B, n_t * TOKEN_TILE, D_MODEL), _F32),
        compiler_params=pltpu.CompilerParams(
            dimension_semantics=("arbitrary", "arbitrary"), vmem_limit_bytes=VMEM_LIMIT),
        name="merge",
    )(o, g, xs, mod_i, w_out, ln_g.reshape(1, D_MODEL), ln_b.reshape(1, D_MODEL))


def _rope_tables(n_tokens):
    pos = jnp.arange(n_tokens, dtype=jnp.int32)
    row = (pos // GRID_W).astype(_F32)
    col = (pos % GRID_W).astype(_F32)
    quarter = GLA_DK // 4
    inv = ROPE_BASE ** (-jnp.arange(quarter, dtype=_F32) / quarter)
    ang = jnp.concatenate([row[:, None] * inv, col[:, None] * inv], -1)
    cos, sin = jnp.cos(ang), jnp.sin(ang)
    return jnp.concatenate([cos, cos], -1), jnp.concatenate([-sin, sin], -1)


def kernel(x, c, ctx, c_ctx, ada_w, ada_b, ln_g, ln_b, w_out, gla_w_in, gla_dec_w1, gla_dec_w2,
           gla_dec_b, gla_norm_g, na_w_in, na_rpb):
    B, S, D = x.shape
    assert (B, S, D) == (8, 2048, D_MODEL) and ctx.shape == (B, CTX_LEN, D)
    cond = jnp.zeros((COND_ROWS, D), _F32).at[0:B].set(c).at[CTX_COND_ROW].set(c_ctx)
    mod = _modulation(cond, ada_w, ada_b).reshape(DEPTH, COND_ROWS, 1, 3 * D_MODEL)
    cos_t, sin_t = _rope_tables(S)
    xs = jnp.concatenate([ctx, x], axis=1)

    for i in range(DEPTH):
        j = i // N_MIXERS
        last = i == DEPTH - 1
        w_o = w_out[i].astype(_BF16)
        if i % N_MIXERS == 0:
            w1 = jnp.concatenate([gla_dec_w1[j, 0], gla_dec_w1[j, 1]], axis=1)
            w1 = jnp.pad(w1, ((0, 0), (0, LOW_PAD - 2 * GLA_GATE_RANK)))
            w_cat = jnp.concatenate([gla_w_in[j], w1], axis=1).astype(_BF16)
            w2_cat = jnp.zeros((LOW_PAD, 2 * GLA_KEY_W), _F32)
            w2_cat = w2_cat.at[0:GLA_GATE_RANK, 0:GLA_KEY_W].set(gla_dec_w2[j, 0])
            w2_cat = w2_cat.at[GLA_GATE_RANK:2 * GLA_GATE_RANK, GLA_KEY_W:].set(gla_dec_w2[j, 1])
            dec_b = gla_dec_b[j].reshape(1, 2 * GLA_KEY_W)
            qf, kf, kef, qb, kb, keb, v, g, dec = _gla_project(
                xs, mod[i], w_cat, w2_cat.astype(_BF16), dec_b, cos_t, sin_t)
            o = _gla_mix(qf, kf, kef, qb, kb, keb, v, dec, gla_norm_g[j])
        else:
            q, k, v, g = _na_project(xs, mod[i], na_w_in[j].astype(_BF16))
            o = _na_mix(q, k, v, _na_bias_table(na_rpb[j]), ctx_queries=not last)
        xs = _merge(o, g, xs, mod[i], w_o, ln_g[i], ln_b[i], latent_only=last)
    return xs
```

```python
import functools

import numpy as np
import jax
import jax.numpy as jnp
from jax import lax
from jax.experimental import pallas as pl
from jax.experimental.pallas import tpu as pltpu

D_MODEL = 1024
DEPTH = 4
GRID_W = 64
CTX_LEN = 256
N_MIXERS = 2
BRANCH = D_MODEL
GLA_HEADS = 4
GLA_DK = 128
GLA_DV = 256
GLA_KEY_W = GLA_HEADS * GLA_DK
GLA_GATE_RANK = 16
GLA_TAU = 16.0
GLA_CHUNK = 64
NA_HEADS = 16
NA_DH = 64
NA_KH = 8
NA_KW = 16
ROPE_BASE = 10000.0
LN_EPS = 1e-5
NORM_EPS = 1e-6
ALPHA = (2 * DEPTH) ** 0.25

TOKEN_TILE = 256
SUBS_PER_STEP = 3
STEP_ROWS = SUBS_PER_STEP * TOKEN_TILE
CHUNKS_PER_TILE = TOKEN_TILE // GLA_CHUNK
COND_ROWS = 16
CTX_COND_ROW = 8
LOW_PAD = 128
NEG_BIG = -1e30
LOG2E = 1.4426950408889634
VMEM_LIMIT = 56 * 1024 * 1024

_F32 = jnp.float32
_BF16 = jnp.bfloat16


def _dot(a, b):
    return jnp.dot(a, b, preferred_element_type=_F32)


def _dot_nt(a, b):
    return lax.dot_general(a, b, (((1,), (1,)), ((), ())), preferred_element_type=_F32)


def _dot_tn(a, b):
    return lax.dot_general(a, b, (((0,), (0,)), ((), ())), preferred_element_type=_F32)


def _silu(x):
    return x * (1.0 / (1.0 + jnp.exp(-x)))


def _split_bf16(x):
    hi = x.astype(_BF16)
    lo = (x - hi.astype(_F32)).astype(_BF16)
    return hi, lo


def _mod_kernel(cond_ref, w_ref, b_ref, out_ref):
    s = _silu(cond_ref[...]).astype(_BF16)
    out_ref[0] = _dot(s, w_ref[0].astype(_BF16)) + b_ref[0]


def _modulation(cond, ada_w, ada_b):
    n_col = 3 * D_MODEL // D_MODEL
    return pl.pallas_call(
        _mod_kernel,
        grid=(DEPTH, n_col),
        in_specs=[
            pl.BlockSpec((COND_ROWS, D_MODEL), lambda i, j: (0, 0)),
            pl.BlockSpec((1, D_MODEL, D_MODEL), lambda i, j: (i, 0, j)),
            pl.BlockSpec((1, 1, D_MODEL), lambda i, j: (i, 0, j)),
        ],
        out_specs=pl.BlockSpec((1, COND_ROWS, D_MODEL), lambda i, j: (i, 0, j)),
        out_shape=jax.ShapeDtypeStruct((DEPTH, COND_ROWS, 3 * D_MODEL), _F32),
        compiler_params=pltpu.CompilerParams(
            dimension_semantics=("arbitrary", "arbitrary"), vmem_limit_bytes=VMEM_LIMIT),
        name="modulation",
    )(cond, ada_w, ada_b.reshape(DEPTH, 1, 3 * D_MODEL))


def _mod_specs():
    return [pl.BlockSpec((1, 1, 3 * D_MODEL), lambda b, t: (b, 0, 0)),
            pl.BlockSpec((1, 1, 3 * D_MODEL), lambda b, t: (CTX_COND_ROW, 0, 0))]


def _sub_tile_mod(modb_ref, modc_ref, is_ctx):
    if is_ctx is False:
        return modb_ref[0]
    return jnp.where(is_ctx, modc_ref[0], modb_ref[0])


def _modulate(x, m):
    return (x * (1.0 + m[:, D_MODEL:2 * D_MODEL]) + m[:, 0:D_MODEL]).astype(_BF16)


def _gla_proj_kernel(x_ref, modb_ref, modc_ref, w_ref, w2_ref, decb_ref, cos_ref, sin_ref, tri_ref,
                     qf_ref, kf_ref, kef_ref, qb_ref, kb_ref, keb_ref, v_ref, g_ref, dec_ref):
    t = pl.program_id(1)
    for sub in range(SUBS_PER_STEP):
        rows = slice(sub * TOKEN_TILE, (sub + 1) * TOKEN_TILE)
        m = _sub_tile_mod(modb_ref, modc_ref, (t == 0) if sub == 0 else False)
        outs = [r.at[0, rows, :] for r in (qf_ref, kf_ref, kef_ref, qb_ref, kb_ref, keb_ref, v_ref, g_ref)]
        _gla_proj_sub_tile(_modulate(x_ref[0, rows, :], m), w_ref, w2_ref, decb_ref,
                           cos_ref[rows, :], sin_ref[rows, :], tri_ref, *outs, dec_ref.at[0, sub])


def _gla_proj_sub_tile(hb, w_ref, w2_ref, decb_ref, cos, sin, tri_ref,
                       qf_ref, kf_ref, kef_ref, qb_ref, kb_ref, keb_ref, v_ref, g_ref, dec_ref):
    kw = GLA_KEY_W
    q = _dot(hb, w_ref[:, 0:kw]) * (GLA_DK ** -0.5)
    k = _dot(hb, w_ref[:, kw:2 * kw])
    low = _dot(hb, w_ref[:, 2 * kw + 2 * BRANCH:2 * kw + 2 * BRANCH + LOW_PAD])
    logits = _dot(low.astype(_BF16), w2_ref[...]) + decb_ref[...]
    v_ref[...] = _dot(hb, w_ref[:, 2 * kw:2 * kw + BRANCH]).astype(_BF16)
    g_ref[...] = _dot(hb, w_ref[:, 2 * kw + BRANCH:2 * kw + 2 * BRANCH]).astype(_BF16)
    la = (jnp.minimum(logits, 0.0) - jnp.log1p(jnp.exp(-jnp.abs(logits)))) * (1.0 / GLA_TAU)
    la_hi, la_lo = _split_bf16(la)
    pre = _dot(tri_ref[...], la_hi) + _dot(tri_ref[...], la_lo)
    pre3 = pre.reshape(CHUNKS_PER_TILE, GLA_CHUNK, 2 * kw)
    tot4 = pre3[:, GLA_CHUNK - 1:GLA_CHUNK, :]
    tot = jnp.broadcast_to(tot4, pre3.shape).reshape(TOKEN_TILE, 2 * kw)
    pre_f, pre_b = pre[:, 0:kw], pre[:, kw:2 * kw]
    tot_f, tot_b = tot[:, 0:kw], tot[:, kw:2 * kw]
    ante_b = pre_b - la[:, kw:2 * kw]

    e_qf = jnp.exp(pre_f)
    e_kf = jnp.exp(-pre_f)
    e_kef = jnp.exp(tot_f - pre_f)
    e_qb = jnp.exp(tot_b - pre_b)
    e_kb = jnp.exp(ante_b - tot_b)
    e_keb = jnp.exp(ante_b)

    for h in range(GLA_HEADS):
        sl = slice(h * GLA_DK, (h + 1) * GLA_DK)
        qh = q[:, sl]
        kh = k[:, sl]
        qh = qh * cos + pltpu.roll(qh, GLA_DK // 2, 1) * sin
        kh = kh * cos + pltpu.roll(kh, GLA_DK // 2, 1) * sin
        qf_ref[:, sl] = (qh * e_qf[:, sl]).astype(_BF16)
        kf_ref[:, sl] = (kh * e_kf[:, sl]).astype(_BF16)
        kef_ref[:, sl] = (kh * e_kef[:, sl]).astype(_BF16)
        qb_ref[:, sl] = (qh * e_qb[:, sl]).astype(_BF16)
        kb_ref[:, sl] = (kh * e_kb[:, sl]).astype(_BF16)
        keb_ref[:, sl] = (kh * e_keb[:, sl]).astype(_BF16)

    tot4 = tot4.reshape(CHUNKS_PER_TILE, 2 * kw)
    dec_ref[...] = jnp.exp(jnp.concatenate([tot4[:, 0:kw], tot4[:, kw:2 * kw]], axis=0))


def _chunk_prefix_matrix():
    i = np.arange(TOKEN_TILE)
    same = (i[:, None] // GLA_CHUNK) == (i[None, :] // GLA_CHUNK)
    return jnp.asarray(same & (i[None, :] <= i[:, None]), _BF16)


def _gla_project(xs, mod_i, w_cat, w2_cat, dec_b, cos_t, sin_t):
    B, L, _ = xs.shape
    n_w = w_cat.shape[1]
    tri = _chunk_prefix_matrix()
    key_spec = pl.BlockSpec((1, STEP_ROWS, GLA_KEY_W), lambda b, t: (b, t, 0))
    val_spec = pl.BlockSpec((1, STEP_ROWS, BRANCH), lambda b, t: (b, t, 0))
    key_shape = jax.ShapeDtypeStruct((B, L, GLA_KEY_W), _BF16)
    val_shape = jax.ShapeDtypeStruct((B, L, BRANCH), _BF16)
    const = lambda shape: pl.BlockSpec(shape, lambda b, t: (0,) * len(shape))
    rope_spec = pl.BlockSpec((STEP_ROWS, GLA_DK), lambda b, t: (t, 0))
    return pl.pallas_call(
        _gla_proj_kernel,
        grid=(B, L // STEP_ROWS),
        in_specs=[
            pl.BlockSpec((1, STEP_ROWS, D_MODEL), lambda b, t: (b, t, 0)),
            *_mod_specs(),
            const((D_MODEL, n_w)),
            const((LOW_PAD, 2 * GLA_KEY_W)),
            const((1, 2 * GLA_KEY_W)),
            rope_spec, rope_spec,
            const((TOKEN_TILE, TOKEN_TILE)),
        ],
        out_specs=[key_spec] * 6 + [val_spec] * 2 + [
            pl.BlockSpec((1, SUBS_PER_STEP, 2 * CHUNKS_PER_TILE, GLA_KEY_W), lambda b, t: (b, t, 0, 0))],
        out_shape=[key_shape] * 6 + [val_shape] * 2 + [
            jax.ShapeDtypeStruct((B, L // TOKEN_TILE, 2 * CHUNKS_PER_TILE, GLA_KEY_W), _F32)],
        compiler_params=pltpu.CompilerParams(
            dimension_semantics=("arbitrary", "arbitrary"), vmem_limit_bytes=VMEM_LIMIT),
        name="gla_project",
    )(xs, mod_i, mod_i, w_cat, w2_cat, dec_b, cos_t, sin_t, tri)


GLA_HEADS_PER_STEP = 2
GLA_CHUNKS_PER_STEP = 2


def _gla_mix_kernel(qf_ref, kf_ref, kef_ref, qb_ref, kb_ref, keb_ref, v_ref, dec_ref, ng_ref,
                    out_ref, accf_ref, accb_ref, sf_ref, sb_ref):
    n_chunks = accf_ref.shape[0] // GLA_CHUNK
    n_ctx = CTX_LEN // GLA_CHUNK
    C = GLA_CHUNK
    sf_ref[...] = jnp.zeros_like(sf_ref)
    sb_ref[...] = jnp.zeros_like(sb_ref)
    ti = lax.broadcasted_iota(jnp.int32, (C, C), 0)
    si = lax.broadcasted_iota(jnp.int32, (C, C), 1)
    mask_f = si <= ti
    mask_b = si > ti
    sub8 = lax.broadcasted_iota(jnp.int32, (2 * CHUNKS_PER_TILE, GLA_DK), 0)

    fwd = (qf_ref, kf_ref, kef_ref, sf_ref, accf_ref, mask_f, 0)
    bwd = (qb_ref, kb_ref, keb_ref, sb_ref, accb_ref, mask_b, CHUNKS_PER_TILE)

    def body(i, carry):
        chains = []
        for u in range(GLA_CHUNKS_PER_STEP):
            cf = i * GLA_CHUNKS_PER_STEP + u
            cb = jnp.where(cf < n_ctx, n_ctx - 1 - cf, n_chunks + n_ctx - 1 - cf)
            for c, (q_ref, k_ref, ke_ref, s_ref, acc_ref, mask, dec_row) in ((cf, fwd), (cb, bwd)):
                for h in range(GLA_HEADS_PER_STEP):
                    chains.append(dict(u=u, c=c, h=h, q_ref=q_ref, k_ref=k_ref, ke_ref=ke_ref, s_ref=s_ref,
                                       acc_ref=acc_ref, mask=mask, dec_row=dec_row))
        for ch in chains:
            c, h = ch["c"], ch["h"]
            ch["rows"] = pl.ds(pl.multiple_of(c * C, C), C)
            ch["ksl"] = slice(h * GLA_DK, (h + 1) * GLA_DK)
            ch["vsl"] = slice(h * GLA_DV, (h + 1) * GLA_DV)
            ch["q"] = ch["q_ref"][0, ch["rows"], ch["ksl"]]
            ch["v"] = v_ref[0, ch["rows"], ch["vsl"]]
            ch["att"] = _dot_nt(ch["q"], ch["k_ref"][0, ch["rows"], ch["ksl"]])
        for ch in chains:
            ch["kv"] = _dot_tn(ch["ke_ref"][0, ch["rows"], ch["ksl"]], ch["v"])
            tile = lax.shift_right_logical(ch["c"], CHUNKS_PER_TILE.bit_length() - 1)
            dec8 = dec_ref[0, tile, :, ch["ksl"]]
            pick = sub8 == ch["dec_row"] + (ch["c"] & (CHUNKS_PER_TILE - 1))
            dec = jnp.sum(jnp.where(pick, dec8, 0.0), axis=0, keepdims=True)
            dcol = jnp.broadcast_to(dec, (GLA_DK, GLA_DK)).T
            ch["decay"] = jnp.concatenate([dcol, dcol], axis=1)
        state = {}
        for u in range(GLA_CHUNKS_PER_STEP):
            for ch in chains:
                if ch["u"] != u:
                    continue
                key = (id(ch["s_ref"]), ch["h"])
                s = state[key] if u else ch["s_ref"][ch["h"]]
                att = jnp.where(ch["mask"], ch["att"], 0.0).astype(_BF16)
                ch["acc_ref"][ch["rows"], ch["vsl"]] = _dot(att, ch["v"]) + _dot(ch["q"], s.astype(_BF16))
                state[key] = s * ch["decay"] + ch["kv"]
        for ch in chains:
            if ch["u"] == GLA_CHUNKS_PER_STEP - 1:
                ch["s_ref"][ch["h"]] = state[(id(ch["s_ref"]), ch["h"])]
        return carry

    lax.fori_loop(0, n_chunks // GLA_CHUNKS_PER_STEP, body, 0)

    def norm_body(j, carry):
        rows = pl.ds(pl.multiple_of(j * TOKEN_TILE, TOKEN_TILE), TOKEN_TILE)
        for h in range(GLA_HEADS_PER_STEP):
            vsl = slice(h * GLA_DV, (h + 1) * GLA_DV)
            o = accf_ref[rows, vsl] + accb_ref[rows, vsl]
            ms = jnp.mean(o * o, axis=-1, keepdims=True)
            out_ref[0, rows, vsl] = (o * lax.rsqrt(ms + NORM_EPS) * ng_ref[...]).astype(_BF16)
        return carry

    lax.fori_loop(0, accf_ref.shape[0] // TOKEN_TILE, norm_body, 0)


def _gla_mix(qf, kf, kef, qb, kb, keb, v, dec, norm_g):
    B, L, _ = qf.shape
    hs = GLA_HEADS_PER_STEP
    n_t = dec.shape[1]
    key_spec = pl.BlockSpec((1, L, hs * GLA_DK), lambda b, p: (b, 0, p))
    val_spec = pl.BlockSpec((1, L, hs * GLA_DV), lambda b, p: (b, 0, p))
    return pl.pallas_call(
        _gla_mix_kernel,
        grid=(B, GLA_HEADS // hs),
        in_specs=[key_spec] * 6 + [
            val_spec,
            pl.BlockSpec((1, n_t, 2 * CHUNKS_PER_TILE, hs * GLA_DK), lambda b, p: (b, 0, 0, p)),
            pl.BlockSpec((1, GLA_DV), lambda b, p: (0, 0)),
        ],
        out_specs=val_spec,
        out_shape=jax.ShapeDtypeStruct((B, L, BRANCH), _BF16),
        scratch_shapes=[
            pltpu.VMEM((L, hs * GLA_DV), _F32),
            pltpu.VMEM((L, hs * GLA_DV), _F32),
            pltpu.VMEM((hs, GLA_DK, GLA_DV), _F32),
            pltpu.VMEM((hs, GLA_DK, GLA_DV), _F32),
        ],
        compiler_params=pltpu.CompilerParams(
            dimension_semantics=("arbitrary", "arbitrary"), vmem_limit_bytes=VMEM_LIMIT),
        name="gla_mix",
    )(qf, kf, kef, qb, kb, keb, v, dec, norm_g.reshape(1, GLA_DV))


def _na_proj_kernel(x_ref, modb_ref, modc_ref, w_ref, q_ref, k_ref, v_ref, g_ref):
    t = pl.program_id(1)
    for sub in range(SUBS_PER_STEP):
        rows = slice(sub * TOKEN_TILE, (sub + 1) * TOKEN_TILE)
        m = _sub_tile_mod(modb_ref, modc_ref, (t == 0) if sub == 0 else False)
        hb = _modulate(x_ref[0, rows, :], m)
        q_ref[0, rows, :] = (_dot(hb, w_ref[:, 0:BRANCH]) * (NA_DH ** -0.5 * LOG2E)).astype(_BF16)
        k_ref[0, rows, :] = _dot(hb, w_ref[:, BRANCH:2 * BRANCH]).astype(_BF16)
        v_ref[0, rows, :] = _dot(hb, w_ref[:, 2 * BRANCH:3 * BRANCH]).astype(_BF16)
        g_ref[0, rows, :] = _dot(hb, w_ref[:, 3 * BRANCH:4 * BRANCH]).astype(_BF16)


def _na_project(xs, mod_i, w_in):
    B, L, _ = xs.shape
    spec = pl.BlockSpec((1, STEP_ROWS, BRANCH), lambda b, t: (b, t, 0))
    shape = jax.ShapeDtypeStruct((B, L, BRANCH), _BF16)
    return pl.pallas_call(
        _na_proj_kernel,
        grid=(B, L // STEP_ROWS),
        in_specs=[
            pl.BlockSpec((1, STEP_ROWS, D_MODEL), lambda b, t: (b, t, 0)),
            *_mod_specs(),
            pl.BlockSpec((D_MODEL, 4 * BRANCH), lambda b, t: (0, 0)),
        ],
        out_specs=[spec] * 4,
        out_shape=[shape] * 4,
        compiler_params=pltpu.CompilerParams(
            dimension_semantics=("arbitrary", "arbitrary"), vmem_limit_bytes=VMEM_LIMIT),
        name="na_project",
    )(xs, mod_i, mod_i, w_in)


NA_ROWS = 2048 // GRID_W
NA_PAIR_W = 2 * NA_DH
NA_BAND = NA_KH * GRID_W
NA_ROWS_PER_STEP = 4


def _stack_heads(q2):
    lane = lax.broadcasted_iota(jnp.int32, q2.shape, 1)
    zero = jnp.zeros_like(q2)
    return jnp.concatenate([jnp.where(lane < NA_DH, q2, zero), jnp.where(lane >= NA_DH, q2, zero)], axis=0)


def _unstack_heads(o, n):
    lane = lax.broadcasted_iota(jnp.int32, (n, NA_PAIR_W), 1)
    return jnp.where(lane < NA_DH, o[0:n], o[n:2 * n])


def _na_mix_kernel(q_ref, k_ref, v_ref, bias_ref, out_ref, sl_a, sc_a, sl_b, sc_b, *, ctx_queries):
    kc = k_ref[0, 0:CTX_LEN, :]
    vc = v_ref[0, 0:CTX_LEN, :]

    if ctx_queries:
        qs = _stack_heads(q_ref[0, 0:CTX_LEN, :])
        s = _dot_nt(qs, kc)
        m = jnp.max(s, axis=-1, keepdims=True)
        p = jnp.exp2(s - m)
        l = jnp.sum(p, axis=-1, keepdims=True)
        o = _dot(p.astype(_BF16), vc) / l
        out_ref[0, 0:CTX_LEN, :] = _unstack_heads(o, CTX_LEN).astype(_BF16)
    else:
        out_ref[0, 0:CTX_LEN, :] = jnp.zeros((CTX_LEN, NA_PAIR_W), _BF16)

    U = NA_ROWS_PER_STEP
    M = 2 * GRID_W
    n_groups = NA_ROWS // U

    def row_slices(g, u):
        r = g * U + u
        r_start = jnp.clip(r - NA_KH // 2, 0, NA_ROWS - NA_KH)
        q_rows = pl.ds(pl.multiple_of(CTX_LEN + r * GRID_W, GRID_W), GRID_W)
        band = pl.ds(pl.multiple_of(CTX_LEN + r_start * GRID_W, GRID_W), NA_BAND)
        return q_rows, band, r - r_start

    def scores(g, sl_ref, sc_ref):
        qss = []
        for u in range(U):
            q_rows, band, row_class = row_slices(g, u)
            qss.append(_stack_heads(q_ref[0, q_rows, :]))
            sl_ref[u] = _dot_nt(qss[u], k_ref[0, band, :]) + bias_ref[0, row_class]
        sc_ref[...] = _dot_nt(jnp.concatenate(qss, axis=0), kc)

    def finish(g, sl_ref, sc_ref):
        p_locs, p_ctxs, ls = [], [], []
        for u in range(U):
            s_loc = sl_ref[u]
            s_ctx = sc_ref[u * M:(u + 1) * M, :]
            m = jnp.maximum(jnp.max(s_loc, axis=-1, keepdims=True), jnp.max(s_ctx, axis=-1, keepdims=True))
            p_loc = jnp.exp2(s_loc - m)
            p_ctx = jnp.exp2(s_ctx - m)
            ls.append(jnp.sum(p_loc, axis=-1, keepdims=True) + jnp.sum(p_ctx, axis=-1, keepdims=True))
            p_locs.append(p_loc.astype(_BF16))
            p_ctxs.append(p_ctx.astype(_BF16))
        o_ctx_all = _dot(jnp.concatenate(p_ctxs, axis=0), vc)
        for u in range(U):
            q_rows, band, _ = row_slices(g, u)
            o = (_dot(p_locs[u], v_ref[0, band, :]) + o_ctx_all[u * M:(u + 1) * M]) / ls[u]
            out_ref[0, q_rows, :] = _unstack_heads(o, GRID_W).astype(_BF16)

    scores(0, sl_a, sc_a)

    def pair_body(i, carry):
        g = 2 * i
        scores(g + 1, sl_b, sc_b)
        finish(g, sl_a, sc_a)
        scores(g + 2, sl_a, sc_a)
        finish(g + 1, sl_b, sc_b)
        return carry

    lax.fori_loop(0, n_groups // 2 - 1, pair_body, 0)
    scores(n_groups - 1, sl_b, sc_b)
    finish(n_groups - 2, sl_a, sc_a)
    finish(n_groups - 1, sl_b, sc_b)


RPB_ROWS_PAD = 16
LANES = 128


def _na_bias_kernel(rpb_ref, out_ref):
    c = lax.broadcasted_iota(jnp.int32, (GRID_W, LANES), 0)
    lane = lax.broadcasted_iota(jnp.int32, (GRID_W, LANES), 1)
    kc = lane & (GRID_W - 1)
    col_start = jnp.clip(c - NA_KW // 2, 0, GRID_W - NA_KW)
    valid = (kc >= col_start) & (kc < col_start + NA_KW)
    low_half = lane < GRID_W
    for a in range(2):
        toe = []
        for ro in range(2 * NA_KH - 1):
            w = jnp.broadcast_to(rpb_ref[0, a, ro:ro + 1, :], (GRID_W, LANES)) * LOG2E
            t_lo = pltpu.roll(w, LANES - (NA_KW - 1), 1, stride=1, stride_axis=0)
            t_hi = pltpu.roll(w, GRID_W - (NA_KW - 1), 1, stride=1, stride_axis=0)
            toe.append((t_lo, t_hi))
        for d in range(NA_KH):
            for jj in range(NA_KH // 2):
                ro = 2 * jj - d + NA_KH - 1
                tile = jnp.where(low_half, toe[ro][0], toe[ro + 1][1])
                out_ref[0, d, a * GRID_W:(a + 1) * GRID_W, jj * LANES:(jj + 1) * LANES] = (
                    jnp.where(valid, tile, NEG_BIG))


def _na_bias_table(rpb):
    n_pairs = NA_HEADS // 2
    n_ro, n_co = rpb.shape[1], rpb.shape[2]
    rpb = jnp.pad(rpb, ((0, 0), (0, RPB_ROWS_PAD - n_ro), (0, LANES - n_co)))
    return pl.pallas_call(
        _na_bias_kernel,
        grid=(n_pairs,),
        in_specs=[pl.BlockSpec((1, 2, RPB_ROWS_PAD, LANES), lambda p: (p, 0, 0, 0))],
        out_specs=pl.BlockSpec((1, NA_KH, 2 * GRID_W, NA_BAND), lambda p: (p, 0, 0, 0)),
        out_shape=jax.ShapeDtypeStruct((n_pairs, NA_KH, 2 * GRID_W, NA_BAND), _F32),
        compiler_params=pltpu.CompilerParams(dimension_semantics=("arbitrary",)),
        name="na_bias_table",
    )(rpb.reshape(n_pairs, 2, RPB_ROWS_PAD, LANES))


def _na_mix(q, k, v, bias, ctx_queries):
    B, L, _ = q.shape
    n_pairs = NA_HEADS // 2
    U = NA_ROWS_PER_STEP
    spec = pl.BlockSpec((1, L, NA_PAIR_W), lambda p, b: (b, 0, p))
    score_bufs = [pltpu.VMEM((U, 2 * GRID_W, NA_BAND), _F32), pltpu.VMEM((U * 2 * GRID_W, CTX_LEN), _F32)]
    return pl.pallas_call(
        functools.partial(_na_mix_kernel, ctx_queries=ctx_queries),
        grid=(n_pairs, B),
        in_specs=[spec, spec, spec,
                  pl.BlockSpec((1, NA_KH, 2 * GRID_W, NA_BAND), lambda p, b: (p, 0, 0, 0))],
        out_specs=spec,
        out_shape=jax.ShapeDtypeStruct((B, L, BRANCH), _BF16),
        scratch_shapes=score_bufs + score_bufs,
        compiler_params=pltpu.CompilerParams(
            dimension_semantics=("arbitrary", "arbitrary"), vmem_limit_bytes=VMEM_LIMIT),
        name="na_mix",
    )(q, k, v, bias)


def _merge_kernel(*refs, n_sub, ctx_in_first):
    o_refs, g_refs, x_refs = refs[0:n_sub], refs[n_sub:2 * n_sub], refs[2 * n_sub:3 * n_sub]
    modb_ref, modc_ref, w_ref, lng_ref, lnb_ref, out_ref = refs[3 * n_sub:]
    t = pl.program_id(1)
    for sub in range(n_sub):
        m = _sub_tile_mod(modb_ref, modc_ref, (t == 0) if (ctx_in_first and sub == 0) else False)
        gate = m[:, 2 * D_MODEL:3 * D_MODEL] * (1.0 / ALPHA)
        g = g_refs[sub][0]
        a = o_refs[sub][0] * (g * (1.0 / (1.0 + jnp.exp(-g))))
        y = _dot(a, w_ref[...])
        z = x_refs[sub][0] + gate * y
        mu = jnp.mean(z, axis=-1, keepdims=True)
        zc = z - mu
        var = jnp.mean(zc * zc, axis=-1, keepdims=True)
        out_ref[0, sub * TOKEN_TILE:(sub + 1) * TOKEN_TILE, :] = (
            zc * lax.rsqrt(var + LN_EPS / ALPHA ** 2) * lng_ref[...] + lnb_ref[...])


def _merge(o, g, xs, mod_i, w_out, ln_g, ln_b, latent_only):
    B, L, _ = xs.shape
    off = 1 if latent_only else 0
    n_sub = 2 if latent_only else SUBS_PER_STEP
    rows = n_sub * TOKEN_TILE
    n_t = (L - off * TOKEN_TILE) // rows

    def sub_specs(width):
        return [pl.BlockSpec((1, TOKEN_TILE, width), lambda b, t, s=s: (b, n_sub * t + s + off, 0))
                for s in range(n_sub)]

    return pl.pallas_call(
        functools.partial(_merge_kernel, n_sub=n_sub, ctx_in_first=not latent_only),
        grid=(B, n_t),
        in_specs=sub_specs(BRANCH) + sub_specs(BRANCH) + sub_specs(D_MODEL) + _mod_specs() + [
            pl.BlockSpec((BRANCH, D_MODEL), lambda b, t: (0, 0)),
            pl.BlockSpec((1, D_MODEL), lambda b, t: (0, 0)),
            pl.BlockSpec((1, D_MODEL), lambda b, t: (0, 0)),
        ],
        out_specs=pl.BlockSpec((1, rows, D_MODEL), lambda b, t: (b, t, 0)),
        out_shape=jax.ShapeDtypeStruct((B, n_t * rows, D_MODEL), _F32),
        compiler_params=pltpu.CompilerParams(
            dimension_semantics=("arbitrary", "arbitrary"), vmem_limit_bytes=VMEM_LIMIT),
        name="merge",
    )(*([o] * n_sub + [g] * n_sub + [xs] * n_sub), mod_i, mod_i, w_out,
      ln_g.reshape(1, D_MODEL), ln_b.reshape(1, D_MODEL))


def _rope_tables(n_tokens):
    pos = jnp.arange(n_tokens, dtype=jnp.int32)
    row = (pos // GRID_W).astype(_F32)
    col = (pos % GRID_W).astype(_F32)
    quarter = GLA_DK // 4
    inv = ROPE_BASE ** (-jnp.arange(quarter, dtype=_F32) / quarter)
    ang = jnp.concatenate([row[:, None] * inv, col[:, None] * inv], -1)
    cos, sin = jnp.cos(ang), jnp.sin(ang)
    cos = jnp.concatenate([cos, cos], -1)
    sin = jnp.concatenate([-sin, sin], -1)
    return (jnp.concatenate([jnp.ones((CTX_LEN, GLA_DK), _F32), cos], 0),
            jnp.concatenate([jnp.zeros((CTX_LEN, GLA_DK), _F32), sin], 0))


def kernel(x, c, ctx, c_ctx, ada_w, ada_b, ln_g, ln_b, w_out, gla_w_in, gla_dec_w1, gla_dec_w2,
           gla_dec_b, gla_norm_g, na_w_in, na_rpb):
    B, S, D = x.shape
    assert (B, S, D) == (8, 2048, D_MODEL) and ctx.shape == (B, CTX_LEN, D)
    cond = jnp.zeros((COND_ROWS, D), _F32).at[0:B].set(c).at[CTX_COND_ROW].set(c_ctx)
    mod = _modulation(cond, ada_w, ada_b).reshape(DEPTH, COND_ROWS, 1, 3 * D_MODEL)
    cos_t, sin_t = _rope_tables(S)
    xs = jnp.concatenate([ctx, x], axis=1)

    for i in range(DEPTH):
        j = i // N_MIXERS
        last = i == DEPTH - 1
        w_o = w_out[i].astype(_BF16)
        if i % N_MIXERS == 0:
            w1 = jnp.concatenate([gla_dec_w1[j, 0], gla_dec_w1[j, 1]], axis=1)
            w1 = jnp.pad(w1, ((0, 0), (0, LOW_PAD - 2 * GLA_GATE_RANK)))
            w_cat = jnp.concatenate([gla_w_in[j], w1], axis=1).astype(_BF16)
            w2_cat = jnp.zeros((LOW_PAD, 2 * GLA_KEY_W), _F32)
            w2_cat = w2_cat.at[0:GLA_GATE_RANK, 0:GLA_KEY_W].set(gla_dec_w2[j, 0])
            w2_cat = w2_cat.at[GLA_GATE_RANK:2 * GLA_GATE_RANK, GLA_KEY_W:].set(gla_dec_w2[j, 1])
            dec_b = gla_dec_b[j].reshape(1, 2 * GLA_KEY_W)
            qf, kf, kef, qb, kb, keb, v, g, dec = _gla_project(
                xs, mod[i], w_cat, w2_cat.astype(_BF16), dec_b, cos_t, sin_t)
            o = _gla_mix(qf, kf, kef, qb, kb, keb, v, dec, gla_norm_g[j])
        else:
            q, k, v, g = _na_project(xs, mod[i], na_w_in[j].astype(_BF16))
            o = _na_mix(q, k, v, _na_bias_table(na_rpb[j]), ctx_queries=not last)
        xs = _merge(o, g, xs, mod[i], w_o, ln_g[i], ln_b[i], latent_only=last)
    return xs
```

```python
import functools

import numpy as np
import jax
import jax.numpy as jnp
from jax import lax
from jax.experimental import pallas as pl
from jax.experimental.pallas import tpu as pltpu

D_MODEL = 1024
DEPTH = 4
GRID_W = 64
CTX_LEN = 256
N_MIXERS = 2
BRANCH = D_MODEL
GLA_HEADS = 4
GLA_DK = 128
GLA_DV = 256
GLA_KEY_W = GLA_HEADS * GLA_DK
GLA_GATE_RANK = 16
GLA_TAU = 16.0
GLA_CHUNK = 64
NA_HEADS = 16
NA_DH = 64
NA_KH = 8
NA_KW = 16
ROPE_BASE = 10000.0
LN_EPS = 1e-5
NORM_EPS = 1e-6
ALPHA = (2 * DEPTH) ** 0.25

TOKEN_TILE = 256
SUBS_PER_STEP = 3
STEP_ROWS = SUBS_PER_STEP * TOKEN_TILE
CHUNKS_PER_TILE = TOKEN_TILE // GLA_CHUNK
COND_ROWS = 16
CTX_COND_ROW = 8
LOW_PAD = 128
NEG_BIG = -1e30
LOG2E = 1.4426950408889634
VMEM_LIMIT = 56 * 1024 * 1024

_F32 = jnp.float32
_BF16 = jnp.bfloat16


def _dot(a, b):
    return jnp.dot(a, b, preferred_element_type=_F32)


def _dot_nt(a, b):
    return lax.dot_general(a, b, (((1,), (1,)), ((), ())), preferred_element_type=_F32)


def _dot_tn(a, b):
    return lax.dot_general(a, b, (((0,), (0,)), ((), ())), preferred_element_type=_F32)


def _silu(x):
    return x * (1.0 / (1.0 + jnp.exp(-x)))


def _split_bf16(x):
    hi = x.astype(_BF16)
    lo = (x - hi.astype(_F32)).astype(_BF16)
    return hi, lo


def _mod_kernel(cond_ref, w_ref, b_ref, out_ref):
    s = _silu(cond_ref[...]).astype(_BF16)
    out_ref[0] = _dot(s, w_ref[0].astype(_BF16)) + b_ref[0]


def _modulation(cond, ada_w, ada_b):
    n_col = 3 * D_MODEL // D_MODEL
    return pl.pallas_call(
        _mod_kernel,
        grid=(DEPTH, n_col),
        in_specs=[
            pl.BlockSpec((COND_ROWS, D_MODEL), lambda i, j: (0, 0)),
            pl.BlockSpec((1, D_MODEL, D_MODEL), lambda i, j: (i, 0, j)),
            pl.BlockSpec((1, 1, D_MODEL), lambda i, j: (i, 0, j)),
        ],
        out_specs=pl.BlockSpec((1, COND_ROWS, D_MODEL), lambda i, j: (i, 0, j)),
        out_shape=jax.ShapeDtypeStruct((DEPTH, COND_ROWS, 3 * D_MODEL), _F32),
        compiler_params=pltpu.CompilerParams(
            dimension_semantics=("arbitrary", "arbitrary"), vmem_limit_bytes=VMEM_LIMIT),
        name="modulation",
    )(cond, ada_w, ada_b.reshape(DEPTH, 1, 3 * D_MODEL))


def _mod_specs():
    return [pl.BlockSpec((1, 1, 3 * D_MODEL), lambda b, t: (b, 0, 0)),
            pl.BlockSpec((1, 1, 3 * D_MODEL), lambda b, t: (CTX_COND_ROW, 0, 0))]


def _sub_tile_mod(modb_ref, modc_ref, is_ctx):
    if is_ctx is False:
        return modb_ref[0]
    return jnp.where(is_ctx, modc_ref[0], modb_ref[0])


def _modulate(x, m):
    return (x * (1.0 + m[:, D_MODEL:2 * D_MODEL]) + m[:, 0:D_MODEL]).astype(_BF16)


def _gla_proj_kernel(x_ref, modb_ref, modc_ref, w_ref, w2_ref, decb_ref, cos_ref, sin_ref, tri_ref,
                     qf_ref, kf_ref, kef_ref, qb_ref, kb_ref, keb_ref, v_ref, g_ref, dec_ref):
    t = pl.program_id(1)
    kw = GLA_KEY_W
    tiles = []
    for sub in range(SUBS_PER_STEP):
        rows = slice(sub * TOKEN_TILE, (sub + 1) * TOKEN_TILE)
        m = _sub_tile_mod(modb_ref, modc_ref, (t == 0) if sub == 0 else False)
        hb = _modulate(x_ref[0, rows, :], m)
        q = _dot(hb, w_ref[:, 0:kw]) * (GLA_DK ** -0.5)
        k = _dot(hb, w_ref[:, kw:2 * kw])
        low = _dot(hb, w_ref[:, 2 * kw + 2 * BRANCH:2 * kw + 2 * BRANCH + LOW_PAD])
        logits = _dot(low.astype(_BF16), w2_ref[...]) + decb_ref[...]
        la = (jnp.minimum(logits, 0.0) - jnp.log(1.0 + jnp.exp(-jnp.abs(logits)))) * (1.0 / GLA_TAU)
        tiles.append((rows, hb, q, k, la))
    for sub, (rows, hb, q, k, la) in enumerate(tiles):
        pre = _dot(tri_ref[...], la.astype(_BF16))
        v_ref[0, rows, :] = _dot(hb, w_ref[:, 2 * kw:2 * kw + BRANCH]).astype(_BF16)
        g_ref[0, rows, :] = _dot(hb, w_ref[:, 2 * kw + BRANCH:2 * kw + 2 * BRANCH]).astype(_BF16)
        outs = [r.at[0, rows, :] for r in (qf_ref, kf_ref, kef_ref, qb_ref, kb_ref, keb_ref)]
        _gla_decay_factors(q, k, la, pre, cos_ref[rows, :], sin_ref[rows, :], *outs, dec_ref.at[0, sub])


def _gla_decay_factors(q, k, la, pre, cos, sin, qf_ref, kf_ref, kef_ref, qb_ref, kb_ref, keb_ref, dec_ref):
    kw = GLA_KEY_W
    tot4 = jnp.concatenate([pre[(c + 1) * GLA_CHUNK - 1:(c + 1) * GLA_CHUNK, :] for c in range(CHUNKS_PER_TILE)],
                           axis=0)

    def chunk_rows(t4):
        return jnp.broadcast_to(t4[:, None, :], (CHUNKS_PER_TILE, GLA_CHUNK, GLA_DK)).reshape(TOKEN_TILE, GLA_DK)

    for h in range(GLA_HEADS):
        sl = slice(h * GLA_DK, (h + 1) * GLA_DK)
        sl_b = slice(kw + h * GLA_DK, kw + (h + 1) * GLA_DK)
        pre_f, tot_f = pre[:, sl], chunk_rows(tot4[:, sl])
        pre_b, tot_b = pre[:, sl_b], chunk_rows(tot4[:, sl_b])
        ante_b = pre_b - la[:, sl_b]
        qh = q[:, sl]
        kh = k[:, sl]
        qh = qh * cos + pltpu.roll(qh, GLA_DK // 2, 1) * sin
        kh = kh * cos + pltpu.roll(kh, GLA_DK // 2, 1) * sin
        qf_ref[:, sl] = (qh * jnp.exp(pre_f)).astype(_BF16)
        kf_ref[:, sl] = (kh * jnp.exp(-pre_f)).astype(_BF16)
        kef_ref[:, sl] = (kh * jnp.exp(tot_f - pre_f)).astype(_BF16)
        qb_ref[:, sl] = (qh * jnp.exp(tot_b - pre_b)).astype(_BF16)
        kb_ref[:, sl] = (kh * jnp.exp(ante_b - tot_b)).astype(_BF16)
        keb_ref[:, sl] = (kh * jnp.exp(ante_b)).astype(_BF16)

    dec_ref[...] = jnp.exp(jnp.concatenate([tot4[:, 0:kw], tot4[:, kw:2 * kw]], axis=0))


def _chunk_prefix_matrix():
    i = np.arange(TOKEN_TILE)
    same = (i[:, None] // GLA_CHUNK) == (i[None, :] // GLA_CHUNK)
    return jnp.asarray(same & (i[None, :] <= i[:, None]), _BF16)


def _gla_project(xs, mod_i, w_cat, w2_cat, dec_b, cos_t, sin_t):
    B, L, _ = xs.shape
    n_w = w_cat.shape[1]
    tri = _chunk_prefix_matrix()
    key_spec = pl.BlockSpec((1, STEP_ROWS, GLA_KEY_W), lambda b, t: (b, t, 0))
    val_spec = pl.BlockSpec((1, STEP_ROWS, BRANCH), lambda b, t: (b, t, 0))
    key_shape = jax.ShapeDtypeStruct((B, L, GLA_KEY_W), _BF16)
    val_shape = jax.ShapeDtypeStruct((B, L, BRANCH), _BF16)
    const = lambda shape: pl.BlockSpec(shape, lambda b, t: (0,) * len(shape))
    rope_spec = pl.BlockSpec((STEP_ROWS, GLA_DK), lambda b, t: (t, 0))
    return pl.pallas_call(
        _gla_proj_kernel,
        grid=(B, L // STEP_ROWS),
        in_specs=[
            pl.BlockSpec((1, STEP_ROWS, D_MODEL), lambda b, t: (b, t, 0)),
            *_mod_specs(),
            const((D_MODEL, n_w)),
            const((LOW_PAD, 2 * GLA_KEY_W)),
            const((1, 2 * GLA_KEY_W)),
            rope_spec, rope_spec,
            const((TOKEN_TILE, TOKEN_TILE)),
        ],
        out_specs=[key_spec] * 6 + [val_spec] * 2 + [
            pl.BlockSpec((1, SUBS_PER_STEP, 2 * CHUNKS_PER_TILE, GLA_KEY_W), lambda b, t: (b, t, 0, 0))],
        out_shape=[key_shape] * 6 + [val_shape] * 2 + [
            jax.ShapeDtypeStruct((B, L // TOKEN_TILE, 2 * CHUNKS_PER_TILE, GLA_KEY_W), _F32)],
        compiler_params=pltpu.CompilerParams(
            dimension_semantics=("arbitrary", "arbitrary"), vmem_limit_bytes=VMEM_LIMIT),
        name="gla_project",
    )(xs, mod_i, mod_i, w_cat, w2_cat, dec_b, cos_t, sin_t, tri)


GLA_HEADS_PER_STEP = 2
GLA_CHUNKS_PER_STEP = 4


def _gla_mix_kernel(qf_ref, kf_ref, kef_ref, qb_ref, kb_ref, keb_ref, v_ref, dec_ref, ng_ref,
                    out_ref, accf_ref, accb_ref, sf_ref, sb_ref):
    n_chunks = accf_ref.shape[0] // GLA_CHUNK
    n_ctx = CTX_LEN // GLA_CHUNK
    C = GLA_CHUNK
    sf_ref[...] = jnp.zeros_like(sf_ref)
    sb_ref[...] = jnp.zeros_like(sb_ref)
    ti = lax.broadcasted_iota(jnp.int32, (C, C), 0)
    si = lax.broadcasted_iota(jnp.int32, (C, C), 1)
    mask_f = si <= ti
    mask_b = si > ti
    sub8 = lax.broadcasted_iota(jnp.int32, (2 * CHUNKS_PER_TILE, GLA_DK), 0)

    fwd = (qf_ref, kf_ref, kef_ref, sf_ref, accf_ref, mask_f, 0)
    bwd = (qb_ref, kb_ref, keb_ref, sb_ref, accb_ref, mask_b, CHUNKS_PER_TILE)

    def body(i, carry):
        chains = []
        for u in range(GLA_CHUNKS_PER_STEP):
            cf = i * GLA_CHUNKS_PER_STEP + u
            cb = jnp.where(cf < n_ctx, n_ctx - 1 - cf, n_chunks + n_ctx - 1 - cf)
            for c, (q_ref, k_ref, ke_ref, s_ref, acc_ref, mask, dec_row) in ((cf, fwd), (cb, bwd)):
                for h in range(GLA_HEADS_PER_STEP):
                    chains.append(dict(u=u, c=c, h=h, q_ref=q_ref, k_ref=k_ref, ke_ref=ke_ref, s_ref=s_ref,
                                       acc_ref=acc_ref, mask=mask, dec_row=dec_row))
        for ch in chains:
            c, h = ch["c"], ch["h"]
            ch["rows"] = pl.ds(pl.multiple_of(c * C, C), C)
            ch["ksl"] = slice(h * GLA_DK, (h + 1) * GLA_DK)
            ch["vsl"] = slice(h * GLA_DV, (h + 1) * GLA_DV)
            ch["q"] = ch["q_ref"][0, ch["rows"], ch["ksl"]]
            ch["v"] = v_ref[0, ch["rows"], ch["vsl"]]
            ch["att"] = _dot_nt(ch["q"], ch["k_ref"][0, ch["rows"], ch["ksl"]])
        for ch in chains:
            ch["kv"] = _dot_tn(ch["ke_ref"][0, ch["rows"], ch["ksl"]], ch["v"])
            tile = lax.shift_right_logical(ch["c"], CHUNKS_PER_TILE.bit_length() - 1)
            dec8 = dec_ref[0, tile, :, ch["ksl"]]
            pick = sub8 == ch["dec_row"] + (ch["c"] & (CHUNKS_PER_TILE - 1))
            dec = jnp.sum(jnp.where(pick, dec8, 0.0), axis=0, keepdims=True)
            dcol = jnp.broadcast_to(dec, (GLA_DK, GLA_DK)).T
            ch["decay"] = jnp.concatenate([dcol, dcol], axis=1)
        state = {}
        for u in range(GLA_CHUNKS_PER_STEP):
            for ch in chains:
                if ch["u"] != u:
                    continue
                key = (id(ch["s_ref"]), ch["h"])
                s = state[key] if u else ch["s_ref"][ch["h"]]
                att = jnp.where(ch["mask"], ch["att"], 0.0).astype(_BF16)
                lhs = jnp.concatenate([ch["q"], att], axis=1)
                rhs = jnp.concatenate([s.astype(_BF16), ch["v"]], axis=0)
                ch["acc_ref"][ch["rows"], ch["vsl"]] = _dot(lhs, rhs)
                state[key] = s * ch["decay"] + ch["kv"]
        for ch in chains:
            if ch["u"] == GLA_CHUNKS_PER_STEP - 1:
                ch["s_ref"][ch["h"]] = state[(id(ch["s_ref"]), ch["h"])]
        return carry

    lax.fori_loop(0, n_chunks // GLA_CHUNKS_PER_STEP, body, 0)

    def norm_body(j, carry):
        rows = pl.ds(pl.multiple_of(j * TOKEN_TILE, TOKEN_TILE), TOKEN_TILE)
        for h in range(GLA_HEADS_PER_STEP):
            vsl = slice(h * GLA_DV, (h + 1) * GLA_DV)
            o = accf_ref[rows, vsl] + accb_ref[rows, vsl]
            ms = jnp.mean(o * o, axis=-1, keepdims=True)
            out_ref[0, rows, vsl] = (o * lax.rsqrt(ms + NORM_EPS) * ng_ref[...]).astype(_BF16)
        return carry

    lax.fori_loop(0, accf_ref.shape[0] // TOKEN_TILE, norm_body, 0)


def _gla_mix(qf, kf, kef, qb, kb, keb, v, dec, norm_g):
    B, L, _ = qf.shape
    hs = GLA_HEADS_PER_STEP
    n_t = dec.shape[1]
    key_spec = pl.BlockSpec((1, L, hs * GLA_DK), lambda b, p: (b, 0, p))
    val_spec = pl.BlockSpec((1, L, hs * GLA_DV), lambda b, p: (b, 0, p))
    return pl.pallas_call(
        _gla_mix_kernel,
        grid=(B, GLA_HEADS // hs),
        in_specs=[key_spec] * 6 + [
            val_spec,
            pl.BlockSpec((1, n_t, 2 * CHUNKS_PER_TILE, hs * GLA_DK), lambda b, p: (b, 0, 0, p)),
            pl.BlockSpec((1, GLA_DV), lambda b, p: (0, 0)),
        ],
        out_specs=val_spec,
        out_shape=jax.ShapeDtypeStruct((B, L, BRANCH), _BF16),
        scratch_shapes=[
            pltpu.VMEM((L, hs * GLA_DV), _F32),
            pltpu.VMEM((L, hs * GLA_DV), _F32),
            pltpu.VMEM((hs, GLA_DK, GLA_DV), _F32),
            pltpu.VMEM((hs, GLA_DK, GLA_DV), _F32),
        ],
        compiler_params=pltpu.CompilerParams(
            dimension_semantics=("arbitrary", "arbitrary"), vmem_limit_bytes=VMEM_LIMIT),
        name="gla_mix",
    )(qf, kf, kef, qb, kb, keb, v, dec, norm_g.reshape(1, GLA_DV))


def _na_proj_kernel(x_ref, modb_ref, modc_ref, w_ref, wkt_ref, q_ref, kt_ref, v_ref, g_ref):
    t = pl.program_id(1)
    for sub in range(SUBS_PER_STEP):
        rows = slice(sub * TOKEN_TILE, (sub + 1) * TOKEN_TILE)
        m = _sub_tile_mod(modb_ref, modc_ref, (t == 0) if sub == 0 else False)
        hb = _modulate(x_ref[0, rows, :], m)
        q_ref[0, rows, :] = (_dot(hb, w_ref[:, 0:BRANCH]) * (NA_DH ** -0.5 * LOG2E)).astype(_BF16)
        kt_ref[0, :, rows] = _dot_nt(wkt_ref[...], hb).astype(_BF16)
        v_ref[0, rows, :] = _dot(hb, w_ref[:, BRANCH:2 * BRANCH]).astype(_BF16)
        g_ref[0, rows, :] = _dot(hb, w_ref[:, 2 * BRANCH:3 * BRANCH]).astype(_BF16)


def _na_project(xs, mod_i, w_qvg, w_kt):
    B, L, _ = xs.shape
    spec = pl.BlockSpec((1, STEP_ROWS, BRANCH), lambda b, t: (b, t, 0))
    shape = jax.ShapeDtypeStruct((B, L, BRANCH), _BF16)
    kt_spec = pl.BlockSpec((1, BRANCH, STEP_ROWS), lambda b, t: (b, 0, t))
    kt_shape = jax.ShapeDtypeStruct((B, BRANCH, L), _BF16)
    return pl.pallas_call(
        _na_proj_kernel,
        grid=(B, L // STEP_ROWS),
        in_specs=[
            pl.BlockSpec((1, STEP_ROWS, D_MODEL), lambda b, t: (b, t, 0)),
            *_mod_specs(),
            pl.BlockSpec((D_MODEL, 3 * BRANCH), lambda b, t: (0, 0)),
            pl.BlockSpec((BRANCH, D_MODEL), lambda b, t: (0, 0)),
        ],
        out_specs=[spec, kt_spec, spec, spec],
        out_shape=[shape, kt_shape, shape, shape],
        compiler_params=pltpu.CompilerParams(
            dimension_semantics=("arbitrary", "arbitrary"), vmem_limit_bytes=VMEM_LIMIT),
        name="na_project",
    )(xs, mod_i, mod_i, w_qvg, w_kt)


NA_ROWS = 2048 // GRID_W
NA_PAIR_W = 2 * NA_DH
NA_BAND = NA_KH * GRID_W
NA_ROWS_PER_STEP = 4
LANES = 128
NA_KT_BLOCKS = (CTX_LEN + 2048) // LANES


def _stack_heads(q2):
    lane = lax.broadcasted_iota(jnp.int32, q2.shape, 1)
    zero = jnp.zeros_like(q2)
    return jnp.concatenate([jnp.where(lane < NA_DH, q2, zero), jnp.where(lane >= NA_DH, q2, zero)], axis=0)


def _unstack_heads(o, n):
    lane = lax.broadcasted_iota(jnp.int32, (n, NA_PAIR_W), 1)
    return jnp.where(lane < NA_DH, o[0:n], o[n:2 * n])


def _na_mix_kernel(q_ref, kt_ref, v_ref, bias_ref, out_ref, sl_a, sc_a, sl_b, sc_b, ktb_ref, *, ctx_queries):
    kct = kt_ref[0, :, 0:CTX_LEN]
    vc = v_ref[0, 0:CTX_LEN, :]
    for i in range(NA_KT_BLOCKS):
        ktb_ref[0, i] = kt_ref[0, :, i * LANES:(i + 1) * LANES]
    for i in range(NA_KT_BLOCKS - 1):
        ktb_ref[1, i] = kt_ref[0, :, i * LANES + GRID_W:(i + 1) * LANES + GRID_W]
    ktb_ref[1, NA_KT_BLOCKS - 1] = jnp.zeros((NA_PAIR_W, LANES), _BF16)

    if ctx_queries:
        qs = _stack_heads(q_ref[0, 0:CTX_LEN, :])
        s = _dot(qs, kct)
        m = jnp.max(s, axis=-1, keepdims=True)
        p = jnp.exp2(s - m)
        l = jnp.sum(p, axis=-1, keepdims=True)
        o = _dot(p.astype(_BF16), vc) / l
        out_ref[0, 0:CTX_LEN, :] = _unstack_heads(o, CTX_LEN).astype(_BF16)
    else:
        out_ref[0, 0:CTX_LEN, :] = jnp.zeros((CTX_LEN, NA_PAIR_W), _BF16)

    U = NA_ROWS_PER_STEP
    M = 2 * GRID_W
    n_groups = NA_ROWS // U

    def row_slices(g, u):
        r = g * U + u
        r_start = jnp.clip(r - NA_KH // 2, 0, NA_ROWS - NA_KH)
        q_rows = pl.ds(pl.multiple_of(CTX_LEN + r * GRID_W, GRID_W), GRID_W)
        band = pl.ds(pl.multiple_of(CTX_LEN + r_start * GRID_W, GRID_W), NA_BAND)
        return q_rows, band, r - r_start

    def key_band_t(g, u):
        r = g * U + u
        band_tok = CTX_LEN // GRID_W + jnp.clip(r - NA_KH // 2, 0, NA_ROWS - NA_KH)
        parity = band_tok & 1
        blocks = ktb_ref[parity, pl.ds(lax.shift_right_logical(band_tok, 1), NA_BAND // LANES)]
        return jnp.concatenate([blocks[i] for i in range(NA_BAND // LANES)], axis=1)

    def scores(g, sl_ref, sc_ref):
        qss = []
        for u in range(U):
            q_rows, _, row_class = row_slices(g, u)
            qss.append(_stack_heads(q_ref[0, q_rows, :]))
            sl_ref[u] = _dot(qss[u], key_band_t(g, u)) + bias_ref[0, row_class]
        sc_ref[...] = _dot(jnp.concatenate(qss, axis=0), kct)

    def finish(g, sl_ref, sc_ref):
        p_locs, p_ctxs, ls = [], [], []
        for u in range(U):
            s_loc = sl_ref[u]
            s_ctx = sc_ref[u * M:(u + 1) * M, :]
            m = jnp.maximum(jnp.max(s_loc, axis=-1, keepdims=True), jnp.max(s_ctx, axis=-1, keepdims=True))
            p_loc = jnp.exp2(s_loc - m)
            p_ctx = jnp.exp2(s_ctx - m)
            ls.append(jnp.sum(p_loc, axis=-1, keepdims=True) + jnp.sum(p_ctx, axis=-1, keepdims=True))
            p_locs.append(p_loc.astype(_BF16))
            p_ctxs.append(p_ctx.astype(_BF16))
        o_ctx_all = _dot(jnp.concatenate(p_ctxs, axis=0), vc)
        for u in range(U):
            q_rows, band, _ = row_slices(g, u)
            o = (_dot(p_locs[u], v_ref[0, band, :]) + o_ctx_all[u * M:(u + 1) * M]) / ls[u]
            out_ref[0, q_rows, :] = _unstack_heads(o, GRID_W).astype(_BF16)

    scores(0, sl_a, sc_a)

    def pair_body(i, carry):
        g = 2 * i
        scores(g + 1, sl_b, sc_b)
        finish(g, sl_a, sc_a)
        scores(g + 2, sl_a, sc_a)
        finish(g + 1, sl_b, sc_b)
        return carry

    lax.fori_loop(0, n_groups // 2 - 1, pair_body, 0)
    scores(n_groups - 1, sl_b, sc_b)
    finish(n_groups - 2, sl_a, sc_a)
    finish(n_groups - 1, sl_b, sc_b)


RPB_ROWS_PAD = 16


def _na_bias_kernel(rpb_ref, out_ref):
    c = lax.broadcasted_iota(jnp.int32, (GRID_W, LANES), 0)
    lane = lax.broadcasted_iota(jnp.int32, (GRID_W, LANES), 1)
    kc = lane & (GRID_W - 1)
    col_start = jnp.clip(c - NA_KW // 2, 0, GRID_W - NA_KW)
    valid = (kc >= col_start) & (kc < col_start + NA_KW)
    low_half = lane < GRID_W
    for a in range(2):
        toe = []
        for ro in range(2 * NA_KH - 1):
            w = jnp.broadcast_to(rpb_ref[0, a, ro:ro + 1, :], (GRID_W, LANES)) * LOG2E
            t_lo = pltpu.roll(w, LANES - (NA_KW - 1), 1, stride=1, stride_axis=0)
            t_hi = pltpu.roll(w, GRID_W - (NA_KW - 1), 1, stride=1, stride_axis=0)
            toe.append((t_lo, t_hi))
        for d in range(NA_KH):
            for jj in range(NA_KH // 2):
                ro = 2 * jj - d + NA_KH - 1
                tile = jnp.where(low_half, toe[ro][0], toe[ro + 1][1])
                out_ref[0, d, a * GRID_W:(a + 1) * GRID_W, jj * LANES:(jj + 1) * LANES] = (
                    jnp.where(valid, tile, NEG_BIG))


def _na_bias_table(rpb):
    n_pairs = NA_HEADS // 2
    n_ro, n_co = rpb.shape[1], rpb.shape[2]
    rpb = jnp.pad(rpb, ((0, 0), (0, RPB_ROWS_PAD - n_ro), (0, LANES - n_co)))
    return pl.pallas_call(
        _na_bias_kernel,
        grid=(n_pairs,),
        in_specs=[pl.BlockSpec((1, 2, RPB_ROWS_PAD, LANES), lambda p: (p, 0, 0, 0))],
        out_specs=pl.BlockSpec((1, NA_KH, 2 * GRID_W, NA_BAND), lambda p: (p, 0, 0, 0)),
        out_shape=jax.ShapeDtypeStruct((n_pairs, NA_KH, 2 * GRID_W, NA_BAND), _F32),
        compiler_params=pltpu.CompilerParams(dimension_semantics=("arbitrary",)),
        name="na_bias_table",
    )(rpb.reshape(n_pairs, 2, RPB_ROWS_PAD, LANES))


def _na_mix(q, k, v, bias, ctx_queries):
    B, L, _ = q.shape
    n_pairs = NA_HEADS // 2
    U = NA_ROWS_PER_STEP
    spec = pl.BlockSpec((1, L, NA_PAIR_W), lambda p, b: (b, 0, p))
    kt_spec = pl.BlockSpec((1, NA_PAIR_W, L), lambda p, b: (b, p, 0))
    score_bufs = [pltpu.VMEM((U, 2 * GRID_W, NA_BAND), _F32), pltpu.VMEM((U * 2 * GRID_W, CTX_LEN), _F32)]
    return pl.pallas_call(
        functools.partial(_na_mix_kernel, ctx_queries=ctx_queries),
        grid=(n_pairs, B),
        in_specs=[spec, kt_spec, spec,
                  pl.BlockSpec((1, NA_KH, 2 * GRID_W, NA_BAND), lambda p, b: (p, 0, 0, 0))],
        out_specs=spec,
        out_shape=jax.ShapeDtypeStruct((B, L, BRANCH), _BF16),
        scratch_shapes=score_bufs + score_bufs + [pltpu.VMEM((2, NA_KT_BLOCKS, NA_PAIR_W, LANES), _BF16)],
        compiler_params=pltpu.CompilerParams(
            dimension_semantics=("arbitrary", "arbitrary"), vmem_limit_bytes=VMEM_LIMIT),
        name="na_mix",
    )(q, k, v, bias)


def _merge_kernel(*refs, n_sub, ctx_in_first):
    o_refs, g_refs, x_refs = refs[0:n_sub], refs[n_sub:2 * n_sub], refs[2 * n_sub:3 * n_sub]
    modb_ref, modc_ref, w_ref, lng_ref, lnb_ref, out_ref = refs[3 * n_sub:]
    t = pl.program_id(1)
    for sub in range(n_sub):
        m = _sub_tile_mod(modb_ref, modc_ref, (t == 0) if (ctx_in_first and sub == 0) else False)
        gate = m[:, 2 * D_MODEL:3 * D_MODEL] * (1.0 / ALPHA)
        g = g_refs[sub][0]
        a = o_refs[sub][0] * (g * (1.0 / (1.0 + jnp.exp(-g))))
        y = _dot(a, w_ref[...])
        z = x_refs[sub][0] + gate * y
        mu = jnp.mean(z, axis=-1, keepdims=True)
        zc = z - mu
        var = jnp.mean(zc * zc, axis=-1, keepdims=True)
        out_ref[0, sub * TOKEN_TILE:(sub + 1) * TOKEN_TILE, :] = (
            zc * lax.rsqrt(var + LN_EPS / ALPHA ** 2) * lng_ref[...] + lnb_ref[...])


def _merge(o, g, xs, mod_i, w_out, ln_g, ln_b, latent_only):
    B, L, _ = xs.shape
    off = 1 if latent_only else 0
    n_sub = 2 if latent_only else SUBS_PER_STEP
    rows = n_sub * TOKEN_TILE
    n_t = (L - off * TOKEN_TILE) // rows

    def sub_specs(width):
        return [pl.BlockSpec((1, TOKEN_TILE, width), lambda b, t, s=s: (b, n_sub * t + s + off, 0))
                for s in range(n_sub)]

    return pl.pallas_call(
        functools.partial(_merge_kernel, n_sub=n_sub, ctx_in_first=not latent_only),
        grid=(B, n_t),
        in_specs=sub_specs(BRANCH) + sub_specs(BRANCH) + sub_specs(D_MODEL) + _mod_specs() + [
            pl.BlockSpec((BRANCH, D_MODEL), lambda b, t: (0, 0)),
            pl.BlockSpec((1, D_MODEL), lambda b, t: (0, 0)),
            pl.BlockSpec((1, D_MODEL), lambda b, t: (0, 0)),
        ],
        out_specs=pl.BlockSpec((1, rows, D_MODEL), lambda b, t: (b, t, 0)),
        out_shape=jax.ShapeDtypeStruct((B, n_t * rows, D_MODEL), _F32),
        compiler_params=pltpu.CompilerParams(
            dimension_semantics=("arbitrary", "arbitrary"), vmem_limit_bytes=VMEM_LIMIT),
        name="merge",
    )(*([o] * n_sub + [g] * n_sub + [xs] * n_sub), mod_i, mod_i, w_out,
      ln_g.reshape(1, D_MODEL), ln_b.reshape(1, D_MODEL))


def _rope_tables(n_tokens):
    pos = jnp.arange(n_tokens, dtype=jnp.int32)
    row = (pos // GRID_W).astype(_F32)
    col = (pos % GRID_W).astype(_F32)
    quarter = GLA_DK // 4
    inv = ROPE_BASE ** (-jnp.arange(quarter, dtype=_F32) / quarter)
    ang = jnp.concatenate([row[:, None] * inv, col[:, None] * inv], -1)
    cos, sin = jnp.cos(ang), jnp.sin(ang)
    cos = jnp.concatenate([cos, cos], -1)
    sin = jnp.concatenate([-sin, sin], -1)
    return (jnp.concatenate([jnp.ones((CTX_LEN, GLA_DK), _F32), cos], 0),
            jnp.concatenate([jnp.zeros((CTX_LEN, GLA_DK), _F32), sin], 0))


def kernel(x, c, ctx, c_ctx, ada_w, ada_b, ln_g, ln_b, w_out, gla_w_in, gla_dec_w1, gla_dec_w2,
           gla_dec_b, gla_norm_g, na_w_in, na_rpb):
    B, S, D = x.shape
    assert (B, S, D) == (8, 2048, D_MODEL) and ctx.shape == (B, CTX_LEN, D)
    cond = jnp.zeros((COND_ROWS, D), _F32).at[0:B].set(c).at[CTX_COND_ROW].set(c_ctx)
    mod = _modulation(cond, ada_w, ada_b).reshape(DEPTH, COND_ROWS, 1, 3 * D_MODEL)
    cos_t, sin_t = _rope_tables(S)
    xs = jnp.concatenate([ctx, x], axis=1)

    for i in range(DEPTH):
        j = i // N_MIXERS
        last = i == DEPTH - 1
        w_o = w_out[i].astype(_BF16)
        if i % N_MIXERS == 0:
            w1 = jnp.concatenate([gla_dec_w1[j, 0], gla_dec_w1[j, 1]], axis=1)
            w1 = jnp.pad(w1, ((0, 0), (0, LOW_PAD - 2 * GLA_GATE_RANK)))
            w_cat = jnp.concatenate([gla_w_in[j], w1], axis=1).astype(_BF16)
            w2_cat = jnp.zeros((LOW_PAD, 2 * GLA_KEY_W), _F32)
            w2_cat = w2_cat.at[0:GLA_GATE_RANK, 0:GLA_KEY_W].set(gla_dec_w2[j, 0])
            w2_cat = w2_cat.at[GLA_GATE_RANK:2 * GLA_GATE_RANK, GLA_KEY_W:].set(gla_dec_w2[j, 1])
            dec_b = gla_dec_b[j].reshape(1, 2 * GLA_KEY_W)
            qf, kf, kef, qb, kb, keb, v, g, dec = _gla_project(
                xs, mod[i], w_cat, w2_cat.astype(_BF16), dec_b, cos_t, sin_t)
            o = _gla_mix(qf, kf, kef, qb, kb, keb, v, dec, gla_norm_g[j])
        else:
            w = na_w_in[j].astype(_BF16)
            w_qvg = jnp.concatenate([w[:, 0:BRANCH], w[:, 2 * BRANCH:4 * BRANCH]], axis=1)
            q, kt, v, g = _na_project(xs, mod[i], w_qvg, w[:, BRANCH:2 * BRANCH].T)
            o = _na_mix(q, kt, v, _na_bias_table(na_rpb[j]), ctx_queries=not last)
        xs = _merge(o, g, xs, mod[i], w_o, ln_g[i], ln_b[i], latent_only=last)
    return xs
```

```python
import functools

import numpy as np
import jax
import jax.numpy as jnp
from jax import lax
from jax.experimental import pallas as pl
from jax.experimental.pallas import tpu as pltpu

D_MODEL = 1024
DEPTH = 4
GRID_W = 64
CTX_LEN = 256
N_MIXERS = 2
BRANCH = D_MODEL
GLA_HEADS = 4
GLA_DK = 128
GLA_DV = 256
GLA_KEY_W = GLA_HEADS * GLA_DK
GLA_GATE_RANK = 16
GLA_TAU = 16.0
GLA_CHUNK = 64
NA_HEADS = 16
NA_DH = 64
NA_KH = 8
NA_KW = 16
ROPE_BASE = 10000.0
LN_EPS = 1e-5
NORM_EPS = 1e-6
ALPHA = (2 * DEPTH) ** 0.25

LANES = 128
TOKEN_TILE = 256
SUBS_PER_STEP = 3
STEP_ROWS = SUBS_PER_STEP * TOKEN_TILE
CHUNKS_PER_TILE = TOKEN_TILE // GLA_CHUNK
COND_ROWS = 16
CTX_COND_ROW = 8
LOW_PAD = 128
NEG_BIG = -1e30
LOG2E = 1.4426950408889634
VMEM_LIMIT = 58 * 1024 * 1024

_F32 = jnp.float32
_BF16 = jnp.bfloat16


def _dot(a, b):
    return jnp.dot(a, b, preferred_element_type=_F32)


def _dot_nt(a, b):
    return lax.dot_general(a, b, (((1,), (1,)), ((), ())), preferred_element_type=_F32)


def _dot_tn(a, b):
    return lax.dot_general(a, b, (((0,), (0,)), ((), ())), preferred_element_type=_F32)


def _silu(x):
    return x * (1.0 / (1.0 + jnp.exp(-x)))


def _const_spec(shape):
    return pl.BlockSpec(shape, lambda b, t: (0,) * len(shape), pipeline_mode=pl.Buffered(1))


def _mod_kernel(cond_ref, w_ref, b_ref, out_ref):
    s = _silu(cond_ref[...]).astype(_BF16)
    out_ref[0] = _dot(s, w_ref[0].astype(_BF16)) + b_ref[0]


def _modulation(cond, ada_w, ada_b):
    n_col = 3 * D_MODEL // D_MODEL
    return pl.pallas_call(
        _mod_kernel,
        grid=(DEPTH, n_col),
        in_specs=[
            pl.BlockSpec((COND_ROWS, D_MODEL), lambda i, j: (0, 0)),
            pl.BlockSpec((1, D_MODEL, D_MODEL), lambda i, j: (i, 0, j)),
            pl.BlockSpec((1, 1, D_MODEL), lambda i, j: (i, 0, j)),
        ],
        out_specs=pl.BlockSpec((1, COND_ROWS, D_MODEL), lambda i, j: (i, 0, j)),
        out_shape=jax.ShapeDtypeStruct((DEPTH, COND_ROWS, 3 * D_MODEL), _F32),
        compiler_params=pltpu.CompilerParams(
            dimension_semantics=("arbitrary", "arbitrary"), vmem_limit_bytes=VMEM_LIMIT),
        name="modulation",
    )(cond, ada_w, ada_b.reshape(DEPTH, 1, 3 * D_MODEL))


def _mod_specs():
    return [pl.BlockSpec((1, 1, 3 * D_MODEL), lambda b, t: (b, 0, 0)),
            pl.BlockSpec((1, 1, 3 * D_MODEL), lambda b, t: (CTX_COND_ROW, 0, 0))]


def _sub_tile_mod(modb_ref, modc_ref, is_ctx):
    if is_ctx is False:
        return modb_ref[0]
    return jnp.where(is_ctx, modc_ref[0], modb_ref[0])


def _modulate(x, m):
    return (x * (1.0 + m[:, D_MODEL:2 * D_MODEL]) + m[:, 0:D_MODEL]).astype(_BF16)


N_STREAM_REFS = 1 + SUBS_PER_STEP
N_MERGE_REFS = 2 + N_STREAM_REFS + 5


def _stream_operands(ctx, lat):
    if ctx is None:
        ctx, shift = lat, 0
    else:
        shift = 1
    specs = [pl.BlockSpec((1, TOKEN_TILE, D_MODEL), lambda b, t: (b, 0, 0))]
    for s in range(SUBS_PER_STEP):
        specs.append(pl.BlockSpec(
            (1, TOKEN_TILE, D_MODEL),
            lambda b, t, s=s: (b, jnp.maximum(SUBS_PER_STEP * t + s - shift, 0), 0)))
    return specs, [ctx] + [lat] * SUBS_PER_STEP


def _stream_sub_tile(stream_refs, sub, t):
    x = stream_refs[1 + sub][0]
    if sub == 0:
        x = jnp.where(t == 0, stream_refs[0][0], x)
    return x


def _merged_sub_tile(o, g, x, gate, w_ref, lng_ref, lnb_ref):
    a = o * (g * (1.0 / (1.0 + jnp.exp(-g))))
    y = _dot(a, w_ref[...])
    z = x + (gate * (1.0 / ALPHA)) * y
    mu = jnp.mean(z, axis=-1, keepdims=True)
    zc = z - mu
    var = jnp.mean(zc * zc, axis=-1, keepdims=True)
    return zc * lax.rsqrt(var + LN_EPS / ALPHA ** 2) * lng_ref[...] + lnb_ref[...]


def _layer_input_tiles(refs, fused):
    t = pl.program_id(1)
    if not fused:
        stream, rest = refs[:N_STREAM_REFS], refs[N_STREAM_REFS:]
        return (lambda sub, xnew_ref: _stream_sub_tile(stream, sub, t)), rest
    o_ref, g_ref = refs[0], refs[1]
    stream = refs[2:2 + N_STREAM_REFS]
    pmodb_ref, pmodc_ref, wout_ref, lng_ref, lnb_ref = refs[2 + N_STREAM_REFS:N_MERGE_REFS]

    def tile(sub, xnew_ref):
        rows = slice(sub * TOKEN_TILE, (sub + 1) * TOKEN_TILE)
        m = _sub_tile_mod(pmodb_ref, pmodc_ref, (t == 0) if sub == 0 else False)
        x_new = _merged_sub_tile(o_ref[0, rows, :], g_ref[0, rows, :], _stream_sub_tile(stream, sub, t),
                                 m[:, 2 * D_MODEL:3 * D_MODEL], wout_ref, lng_ref, lnb_ref)
        xnew_ref[0, rows, :] = x_new
        return x_new

    return tile, refs[N_MERGE_REFS:]


def _merge_operands(merge):
    if merge is None:
        return [], []
    o, g, ctx, lat, mod_prev, w_out, ln_g, ln_b = merge
    step = pl.BlockSpec((1, STEP_ROWS, BRANCH), lambda b, t: (b, t, 0))
    s_specs, s_args = _stream_operands(ctx, lat)
    specs = [step, step] + s_specs + _mod_specs() + [
        _const_spec((BRANCH, D_MODEL)), _const_spec((1, D_MODEL)), _const_spec((1, D_MODEL))]
    args = [o, g] + s_args + [mod_prev, mod_prev, w_out, ln_g.reshape(1, D_MODEL), ln_b.reshape(1, D_MODEL)]
    return specs, args


def _gla_proj_kernel(*refs, fused):
    tile_fn, refs = _layer_input_tiles(refs, fused)
    modb_ref, modc_ref, w_ref, w2_ref, decb_ref, cos_ref, sin_ref, tri_ref = refs[:8]
    outs = refs[8:]
    xnew_ref = outs[0] if fused else None
    qf_ref, kf_ref, kef_ref, qb_ref, kb_ref, keb_ref, v_ref, g_ref, dec_ref = outs[1 if fused else 0:]
    t = pl.program_id(1)
    kw = GLA_KEY_W
    tiles = []
    for sub in range(SUBS_PER_STEP):
        rows = slice(sub * TOKEN_TILE, (sub + 1) * TOKEN_TILE)
        m = _sub_tile_mod(modb_ref, modc_ref, (t == 0) if sub == 0 else False)
        hb = _modulate(tile_fn(sub, xnew_ref), m)
        q = _dot(hb, w_ref[:, 0:kw]) * (GLA_DK ** -0.5)
        k = _dot(hb, w_ref[:, kw:2 * kw])
        low = _dot(hb, w_ref[:, 2 * kw + 2 * BRANCH:2 * kw + 2 * BRANCH + LOW_PAD])
        logits = _dot(low.astype(_BF16), w2_ref[...]) + decb_ref[...]
        la = (jnp.minimum(logits, 0.0) - jnp.log(1.0 + jnp.exp(-jnp.abs(logits)))) * (1.0 / GLA_TAU)
        tiles.append((rows, hb, q, k, la))
    for sub, (rows, hb, q, k, la) in enumerate(tiles):
        pre = _dot(tri_ref[...], la.astype(_BF16))
        v_ref[0, rows, :] = _dot(hb, w_ref[:, 2 * kw:2 * kw + BRANCH]).astype(_BF16)
        g_ref[0, rows, :] = _dot(hb, w_ref[:, 2 * kw + BRANCH:2 * kw + 2 * BRANCH]).astype(_BF16)
        key_outs = [r.at[0, rows, :] for r in (qf_ref, kf_ref, kef_ref, qb_ref, kb_ref, keb_ref)]
        _gla_decay_factors(q, k, la, pre, cos_ref[rows, :], sin_ref[rows, :], *key_outs, dec_ref.at[0, sub])


def _gla_decay_factors(q, k, la, pre, cos, sin, qf_ref, kf_ref, kef_ref, qb_ref, kb_ref, keb_ref, dec_ref):
    kw = GLA_KEY_W
    tot4 = jnp.concatenate([pre[(c + 1) * GLA_CHUNK - 1:(c + 1) * GLA_CHUNK, :] for c in range(CHUNKS_PER_TILE)],
                           axis=0)

    def chunk_rows(t4):
        return jnp.broadcast_to(t4[:, None, :], (CHUNKS_PER_TILE, GLA_CHUNK, GLA_DK)).reshape(TOKEN_TILE, GLA_DK)

    for h in range(GLA_HEADS):
        sl = slice(h * GLA_DK, (h + 1) * GLA_DK)
        sl_b = slice(kw + h * GLA_DK, kw + (h + 1) * GLA_DK)
        pre_f, tot_f = pre[:, sl], chunk_rows(tot4[:, sl])
        pre_b, tot_b = pre[:, sl_b], chunk_rows(tot4[:, sl_b])
        ante_b = pre_b - la[:, sl_b]
        qh = q[:, sl]
        kh = k[:, sl]
        qh = qh * cos + pltpu.roll(qh, GLA_DK // 2, 1) * sin
        kh = kh * cos + pltpu.roll(kh, GLA_DK // 2, 1) * sin
        qf_ref[:, sl] = (qh * jnp.exp(pre_f)).astype(_BF16)
        kf_ref[:, sl] = (kh * jnp.exp(-pre_f)).astype(_BF16)
        kef_ref[:, sl] = (kh * jnp.exp(tot_f - pre_f)).astype(_BF16)
        qb_ref[:, sl] = (qh * jnp.exp(tot_b - pre_b)).astype(_BF16)
        kb_ref[:, sl] = (kh * jnp.exp(ante_b - tot_b)).astype(_BF16)
        keb_ref[:, sl] = (kh * jnp.exp(ante_b)).astype(_BF16)

    dec_ref[...] = jnp.exp(jnp.concatenate([tot4[:, 0:kw], tot4[:, kw:2 * kw]], axis=0))


def _chunk_prefix_matrix():
    i = np.arange(TOKEN_TILE)
    same = (i[:, None] // GLA_CHUNK) == (i[None, :] // GLA_CHUNK)
    return jnp.asarray(same & (i[None, :] <= i[:, None]), _BF16)


def _gla_project(layer_in, merge, mod_i, w_cat, w2_cat, dec_b, cos_t, sin_t):
    fused = merge is not None
    B = (merge[0] if fused else layer_in[1]).shape[0]
    L = CTX_LEN + 2048
    n_w = w_cat.shape[1]
    tri = _chunk_prefix_matrix()
    in_specs, in_args = _merge_operands(merge) if fused else _stream_operands(*layer_in)
    key_spec = pl.BlockSpec((1, STEP_ROWS, GLA_KEY_W), lambda b, t: (b, t, 0))
    val_spec = pl.BlockSpec((1, STEP_ROWS, BRANCH), lambda b, t: (b, t, 0))
    key_shape = jax.ShapeDtypeStruct((B, L, GLA_KEY_W), _BF16)
    val_shape = jax.ShapeDtypeStruct((B, L, BRANCH), _BF16)
    rope_spec = pl.BlockSpec((STEP_ROWS, GLA_DK), lambda b, t: (t, 0))
    x_specs = [pl.BlockSpec((1, STEP_ROWS, D_MODEL), lambda b, t: (b, t, 0))] if fused else []
    x_shapes = [jax.ShapeDtypeStruct((B, L, D_MODEL), _F32)] if fused else []
    outs = pl.pallas_call(
        functools.partial(_gla_proj_kernel, fused=fused),
        grid=(B, L // STEP_ROWS),
        in_specs=in_specs + _mod_specs() + [
            _const_spec((D_MODEL, n_w)),
            _const_spec((LOW_PAD, 2 * GLA_KEY_W)),
            _const_spec((1, 2 * GLA_KEY_W)),
            rope_spec, rope_spec,
            _const_spec((TOKEN_TILE, TOKEN_TILE)),
        ],
        out_specs=x_specs + [key_spec] * 6 + [val_spec] * 2 + [
            pl.BlockSpec((1, SUBS_PER_STEP, 2 * CHUNKS_PER_TILE, GLA_KEY_W), lambda b, t: (b, t, 0, 0))],
        out_shape=x_shapes + [key_shape] * 6 + [val_shape] * 2 + [
            jax.ShapeDtypeStruct((B, L // TOKEN_TILE, 2 * CHUNKS_PER_TILE, GLA_KEY_W), _F32)],
        compiler_params=pltpu.CompilerParams(
            dimension_semantics=("arbitrary", "arbitrary"), vmem_limit_bytes=VMEM_LIMIT),
        name="gla_project",
    )(*in_args, mod_i, mod_i, w_cat, w2_cat, dec_b, cos_t, sin_t, tri)
    return (outs[0], outs[1:]) if fused else (None, outs)


GLA_HEADS_PER_STEP = 2
GLA_CHUNKS_PER_STEP = 4


def _gla_mix_kernel(qf_ref, kf_ref, kef_ref, qb_ref, kb_ref, keb_ref, v_ref, dec_ref, ng_ref,
                    out_ref, accf_ref, accb_ref, sf_ref, sb_ref):
    n_chunks = accf_ref.shape[0] // GLA_CHUNK
    n_ctx = CTX_LEN // GLA_CHUNK
    C = GLA_CHUNK
    sf_ref[...] = jnp.zeros_like(sf_ref)
    sb_ref[...] = jnp.zeros_like(sb_ref)
    ti = lax.broadcasted_iota(jnp.int32, (C, C), 0)
    si = lax.broadcasted_iota(jnp.int32, (C, C), 1)
    mask_f = si <= ti
    mask_b = si > ti
    sub8 = lax.broadcasted_iota(jnp.int32, (2 * CHUNKS_PER_TILE, GLA_DK), 0)

    fwd = (qf_ref, kf_ref, kef_ref, sf_ref, accf_ref, mask_f, 0)
    bwd = (qb_ref, kb_ref, keb_ref, sb_ref, accb_ref, mask_b, CHUNKS_PER_TILE)

    def body(i, carry):
        chains = []
        for u in range(GLA_CHUNKS_PER_STEP):
            cf = i * GLA_CHUNKS_PER_STEP + u
            cb = jnp.where(cf < n_ctx, n_ctx - 1 - cf, n_chunks + n_ctx - 1 - cf)
            for c, (q_ref, k_ref, ke_ref, s_ref, acc_ref, mask, dec_row) in ((cf, fwd), (cb, bwd)):
                for h in range(GLA_HEADS_PER_STEP):
                    chains.append(dict(u=u, c=c, h=h, q_ref=q_ref, k_ref=k_ref, ke_ref=ke_ref, s_ref=s_ref,
                                       acc_ref=acc_ref, mask=mask, dec_row=dec_row))
        for ch in chains:
            c, h = ch["c"], ch["h"]
            ch["rows"] = pl.ds(pl.multiple_of(c * C, C), C)
            ch["ksl"] = slice(h * GLA_DK, (h + 1) * GLA_DK)
            ch["vsl"] = slice(h * GLA_DV, (h + 1) * GLA_DV)
            ch["q"] = ch["q_ref"][0, ch["rows"], ch["ksl"]]
            ch["v"] = v_ref[0, ch["rows"], ch["vsl"]]
            ch["att"] = _dot_nt(ch["q"], ch["k_ref"][0, ch["rows"], ch["ksl"]])
        for ch in chains:
            ch["kv"] = _dot_tn(ch["ke_ref"][0, ch["rows"], ch["ksl"]], ch["v"])
            tile = lax.shift_right_logical(ch["c"], CHUNKS_PER_TILE.bit_length() - 1)
            dec8 = dec_ref[0, tile, :, ch["ksl"]]
            pick = sub8 == ch["dec_row"] + (ch["c"] & (CHUNKS_PER_TILE - 1))
            dec = jnp.sum(jnp.where(pick, dec8, 0.0), axis=0, keepdims=True)
            dcol = jnp.broadcast_to(dec, (GLA_DK, GLA_DK)).T
            ch["decay"] = jnp.concatenate([dcol, dcol], axis=1)
        state = {}
        for u in range(GLA_CHUNKS_PER_STEP):
            for ch in chains:
                if ch["u"] != u:
                    continue
                key = (id(ch["s_ref"]), ch["h"])
                s = state[key] if u else ch["s_ref"][ch["h"]]
                att = jnp.where(ch["mask"], ch["att"], 0.0).astype(_BF16)
                lhs = jnp.concatenate([ch["q"], att], axis=1)
                rhs = jnp.concatenate([s.astype(_BF16), ch["v"]], axis=0)
                ch["acc_ref"][ch["rows"], ch["vsl"]] = _dot(lhs, rhs)
                state[key] = s * ch["decay"] + ch["kv"]
        for ch in chains:
            if ch["u"] == GLA_CHUNKS_PER_STEP - 1:
                ch["s_ref"][ch["h"]] = state[(id(ch["s_ref"]), ch["h"])]
        return carry

    lax.fori_loop(0, n_chunks // GLA_CHUNKS_PER_STEP, body, 0)

    def norm_body(j, carry):
        rows = pl.ds(pl.multiple_of(j * TOKEN_TILE, TOKEN_TILE), TOKEN_TILE)
        for h in range(GLA_HEADS_PER_STEP):
            vsl = slice(h * GLA_DV, (h + 1) * GLA_DV)
            o = accf_ref[rows, vsl] + accb_ref[rows, vsl]
            ms = jnp.mean(o * o, axis=-1, keepdims=True)
            out_ref[0, rows, vsl] = (o * lax.rsqrt(ms + NORM_EPS) * ng_ref[...]).astype(_BF16)
        return carry

    lax.fori_loop(0, accf_ref.shape[0] // TOKEN_TILE, norm_body, 0)


def _gla_mix(qf, kf, kef, qb, kb, keb, v, dec, norm_g):
    B, L, _ = qf.shape
    hs = GLA_HEADS_PER_STEP
    n_t = dec.shape[1]
    key_spec = pl.BlockSpec((1, L, hs * GLA_DK), lambda b, p: (b, 0, p))
    val_spec = pl.BlockSpec((1, L, hs * GLA_DV), lambda b, p: (b, 0, p))
    return pl.pallas_call(
        _gla_mix_kernel,
        grid=(B, GLA_HEADS // hs),
        in_specs=[key_spec] * 6 + [
            val_spec,
            pl.BlockSpec((1, n_t, 2 * CHUNKS_PER_TILE, hs * GLA_DK), lambda b, p: (b, 0, 0, p)),
            pl.BlockSpec((1, GLA_DV), lambda b, p: (0, 0)),
        ],
        out_specs=val_spec,
        out_shape=jax.ShapeDtypeStruct((B, L, BRANCH), _BF16),
        scratch_shapes=[
            pltpu.VMEM((L, hs * GLA_DV), _F32),
            pltpu.VMEM((L, hs * GLA_DV), _F32),
            pltpu.VMEM((hs, GLA_DK, GLA_DV), _F32),
            pltpu.VMEM((hs, GLA_DK, GLA_DV), _F32),
        ],
        compiler_params=pltpu.CompilerParams(
            dimension_semantics=("arbitrary", "arbitrary"), vmem_limit_bytes=VMEM_LIMIT),
        name="gla_mix",
    )(qf, kf, kef, qb, kb, keb, v, dec, norm_g.reshape(1, GLA_DV))


def _na_proj_kernel(*refs, fused):
    tile_fn, refs = _layer_input_tiles(refs, fused)
    modb_ref, modc_ref, w_ref = refs[:3]
    outs = refs[3:]
    xnew_ref = outs[0] if fused else None
    q_ref, k_ref, v_ref, g_ref = outs[1 if fused else 0:]
    t = pl.program_id(1)
    for sub in range(SUBS_PER_STEP):
        rows = slice(sub * TOKEN_TILE, (sub + 1) * TOKEN_TILE)
        m = _sub_tile_mod(modb_ref, modc_ref, (t == 0) if sub == 0 else False)
        hb = _modulate(tile_fn(sub, xnew_ref), m)
        q_ref[0, rows, :] = (_dot(hb, w_ref[:, 0:BRANCH]) * (NA_DH ** -0.5 * LOG2E)).astype(_BF16)
        k_ref[0, rows, :] = _dot(hb, w_ref[:, BRANCH:2 * BRANCH]).astype(_BF16)
        v_ref[0, rows, :] = _dot(hb, w_ref[:, 2 * BRANCH:3 * BRANCH]).astype(_BF16)
        g_ref[0, rows, :] = _dot(hb, w_ref[:, 3 * BRANCH:4 * BRANCH]).astype(_BF16)


def _na_project(layer_in, merge, mod_i, w_in):
    fused = merge is not None
    B = (merge[0] if fused else layer_in[1]).shape[0]
    L = CTX_LEN + 2048
    in_specs, in_args = _merge_operands(merge) if fused else _stream_operands(*layer_in)
    spec = pl.BlockSpec((1, STEP_ROWS, BRANCH), lambda b, t: (b, t, 0))
    shape = jax.ShapeDtypeStruct((B, L, BRANCH), _BF16)
    x_specs = [pl.BlockSpec((1, STEP_ROWS, D_MODEL), lambda b, t: (b, t, 0))] if fused else []
    x_shapes = [jax.ShapeDtypeStruct((B, L, D_MODEL), _F32)] if fused else []
    outs = pl.pallas_call(
        functools.partial(_na_proj_kernel, fused=fused),
        grid=(B, L // STEP_ROWS),
        in_specs=in_specs + _mod_specs() + [_const_spec((D_MODEL, 4 * BRANCH))],
        out_specs=x_specs + [spec] * 4,
        out_shape=x_shapes + [shape] * 4,
        compiler_params=pltpu.CompilerParams(
            dimension_semantics=("arbitrary", "arbitrary"), vmem_limit_bytes=VMEM_LIMIT),
        name="na_project",
    )(*in_args, mod_i, mod_i, w_in)
    return (outs[0], outs[1:]) if fused else (None, outs)


NA_ROWS = 2048 // GRID_W
NA_PAIR_W = 2 * NA_DH
NA_BAND = NA_KH * GRID_W
NA_ROWS_PER_STEP = 4


def _stack_heads(q2):
    lane = lax.broadcasted_iota(jnp.int32, q2.shape, 1)
    zero = jnp.zeros_like(q2)
    return jnp.concatenate([jnp.where(lane < NA_DH, q2, zero), jnp.where(lane >= NA_DH, q2, zero)], axis=0)


def _unstack_heads(o, n):
    lane = lax.broadcasted_iota(jnp.int32, (n, NA_PAIR_W), 1)
    return jnp.where(lane < NA_DH, o[0:n], o[n:2 * n])


def _na_mix_kernel(q_ref, k_ref, v_ref, bias_ref, out_ref, sl_a, sc_a, sl_b, sc_b, *, ctx_queries):
    kc = k_ref[0, 0:CTX_LEN, :]
    vc = v_ref[0, 0:CTX_LEN, :]

    if ctx_queries:
        qs = _stack_heads(q_ref[0, 0:CTX_LEN, :])
        s = _dot_nt(qs, kc)
        m = jnp.max(s, axis=-1, keepdims=True)
        p = jnp.exp2(s - m)
        l = jnp.sum(p, axis=-1, keepdims=True)
        o = _dot(p.astype(_BF16), vc) / l
        out_ref[0, 0:CTX_LEN, :] = _unstack_heads(o, CTX_LEN).astype(_BF16)
    else:
        out_ref[0, 0:CTX_LEN, :] = jnp.zeros((CTX_LEN, NA_PAIR_W), _BF16)

    U = NA_ROWS_PER_STEP
    M = 2 * GRID_W
    n_groups = NA_ROWS // U

    def row_slices(g, u):
        r = g * U + u
        r_start = jnp.clip(r - NA_KH // 2, 0, NA_ROWS - NA_KH)
        q_rows = pl.ds(pl.multiple_of(CTX_LEN + r * GRID_W, GRID_W), GRID_W)
        band = pl.ds(pl.multiple_of(CTX_LEN + r_start * GRID_W, GRID_W), NA_BAND)
        return q_rows, band, r - r_start

    def scores(g, sl_ref, sc_ref):
        qss = []
        for u in range(U):
            q_rows, band, row_class = row_slices(g, u)
            qss.append(_stack_heads(q_ref[0, q_rows, :]))
            sl_ref[u] = _dot_nt(qss[u], k_ref[0, band, :]) + bias_ref[0, row_class]
        sc_ref[...] = _dot_nt(jnp.concatenate(qss, axis=0), kc)

    def finish(g, sl_ref, sc_ref):
        p_locs, p_ctxs, ls = [], [], []
        for u in range(U):
            s_loc = sl_ref[u]
            s_ctx = sc_ref[u * M:(u + 1) * M, :]
            m = jnp.maximum(jnp.max(s_loc, axis=-1, keepdims=True), jnp.max(s_ctx, axis=-1, keepdims=True))
            p_loc = jnp.exp2(s_loc - m)
            p_ctx = jnp.exp2(s_ctx - m)
            ls.append(jnp.sum(p_loc, axis=-1, keepdims=True) + jnp.sum(p_ctx, axis=-1, keepdims=True))
            p_locs.append(p_loc.astype(_BF16))
            p_ctxs.append(p_ctx.astype(_BF16))
        o_ctx_all = _dot(jnp.concatenate(p_ctxs, axis=0), vc)
        for u in range(U):
            q_rows, band, _ = row_slices(g, u)
            o = (_dot(p_locs[u], v_ref[0, band, :]) + o_ctx_all[u * M:(u + 1) * M]) / ls[u]
            out_ref[0, q_rows, :] = _unstack_heads(o, GRID_W).astype(_BF16)

    scores(0, sl_a, sc_a)

    def pair_body(i, carry):
        g = 2 * i
        scores(g + 1, sl_b, sc_b)
        finish(g, sl_a, sc_a)
        scores(g + 2, sl_a, sc_a)
        finish(g + 1, sl_b, sc_b)
        return carry

    lax.fori_loop(0, n_groups // 2 - 1, pair_body, 0)
    scores(n_groups - 1, sl_b, sc_b)
    finish(n_groups - 2, sl_a, sc_a)
    finish(n_groups - 1, sl_b, sc_b)


RPB_ROWS_PAD = 16


def _na_bias_kernel(rpb_ref, out_ref):
    c = lax.broadcasted_iota(jnp.int32, (GRID_W, LANES), 0)
    lane = lax.broadcasted_iota(jnp.int32, (GRID_W, LANES), 1)
    kc = lane & (GRID_W - 1)
    col_start = jnp.clip(c - NA_KW // 2, 0, GRID_W - NA_KW)
    valid = (kc >= col_start) & (kc < col_start + NA_KW)
    low_half = lane < GRID_W
    for a in range(2):
        toe = []
        for ro in range(2 * NA_KH - 1):
            w = jnp.broadcast_to(rpb_ref[0, a, ro:ro + 1, :], (GRID_W, LANES)) * LOG2E
            t_lo = pltpu.roll(w, LANES - (NA_KW - 1), 1, stride=1, stride_axis=0)
            t_hi = pltpu.roll(w, GRID_W - (NA_KW - 1), 1, stride=1, stride_axis=0)
            toe.append((t_lo, t_hi))
        for d in range(NA_KH):
            for jj in range(NA_KH // 2):
                ro = 2 * jj - d + NA_KH - 1
                tile = jnp.where(low_half, toe[ro][0], toe[ro + 1][1])
                out_ref[0, d, a * GRID_W:(a + 1) * GRID_W, jj * LANES:(jj + 1) * LANES] = (
                    jnp.where(valid, tile, NEG_BIG))


def _na_bias_table(rpb):
    n_pairs = NA_HEADS // 2
    n_ro, n_co = rpb.shape[1], rpb.shape[2]
    rpb = jnp.pad(rpb, ((0, 0), (0, RPB_ROWS_PAD - n_ro), (0, LANES - n_co)))
    return pl.pallas_call(
        _na_bias_kernel,
        grid=(n_pairs,),
        in_specs=[pl.BlockSpec((1, 2, RPB_ROWS_PAD, LANES), lambda p: (p, 0, 0, 0))],
        out_specs=pl.BlockSpec((1, NA_KH, 2 * GRID_W, NA_BAND), lambda p: (p, 0, 0, 0)),
        out_shape=jax.ShapeDtypeStruct((n_pairs, NA_KH, 2 * GRID_W, NA_BAND), _F32),
        compiler_params=pltpu.CompilerParams(dimension_semantics=("arbitrary",)),
        name="na_bias_table",
    )(rpb.reshape(n_pairs, 2, RPB_ROWS_PAD, LANES))


def _na_mix(q, k, v, bias, ctx_queries):
    B, L, _ = q.shape
    n_pairs = NA_HEADS // 2
    U = NA_ROWS_PER_STEP
    spec = pl.BlockSpec((1, L, NA_PAIR_W), lambda p, b: (b, 0, p))
    score_bufs = [pltpu.VMEM((U, 2 * GRID_W, NA_BAND), _F32), pltpu.VMEM((U * 2 * GRID_W, CTX_LEN), _F32)]
    return pl.pallas_call(
        functools.partial(_na_mix_kernel, ctx_queries=ctx_queries),
        grid=(n_pairs, B),
        in_specs=[spec, spec, spec,
                  pl.BlockSpec((1, NA_KH, 2 * GRID_W, NA_BAND), lambda p, b: (p, 0, 0, 0))],
        out_specs=spec,
        out_shape=jax.ShapeDtypeStruct((B, L, BRANCH), _BF16),
        scratch_shapes=score_bufs + score_bufs,
        compiler_params=pltpu.CompilerParams(
            dimension_semantics=("arbitrary", "arbitrary"), vmem_limit_bytes=VMEM_LIMIT),
        name="na_mix",
    )(q, k, v, bias)


FINAL_SUBS = 2


def _final_merge_kernel(*refs):
    n = FINAL_SUBS
    o_refs, g_refs, x_refs = refs[0:n], refs[n:2 * n], refs[2 * n:3 * n]
    mod_ref, w_ref, lng_ref, lnb_ref, out_ref = refs[3 * n:]
    gate = mod_ref[0][:, 2 * D_MODEL:3 * D_MODEL]
    for sub in range(n):
        out_ref[0, sub * TOKEN_TILE:(sub + 1) * TOKEN_TILE, :] = _merged_sub_tile(
            o_refs[sub][0], g_refs[sub][0], x_refs[sub][0], gate, w_ref, lng_ref, lnb_ref)


def _final_merge(o, g, xs, mod_i, w_out, ln_g, ln_b):
    B, L, _ = xs.shape
    rows = FINAL_SUBS * TOKEN_TILE
    n_t = (L - CTX_LEN) // rows

    def sub_specs(width):
        return [pl.BlockSpec((1, TOKEN_TILE, width), lambda b, t, s=s: (b, FINAL_SUBS * t + s + 1, 0))
                for s in range(FINAL_SUBS)]

    return pl.pallas_call(
        _final_merge_kernel,
        grid=(B, n_t),
        in_specs=sub_specs(BRANCH) + sub_specs(BRANCH) + sub_specs(D_MODEL) + [
            _mod_specs()[0],
            _const_spec((BRANCH, D_MODEL)), _const_spec((1, D_MODEL)), _const_spec((1, D_MODEL)),
        ],
        out_specs=pl.BlockSpec((1, rows, D_MODEL), lambda b, t: (b, t, 0)),
        out_shape=jax.ShapeDtypeStruct((B, n_t * rows, D_MODEL), _F32),
        compiler_params=pltpu.CompilerParams(
            dimension_semantics=("arbitrary", "arbitrary"), vmem_limit_bytes=VMEM_LIMIT),
        name="final_merge",
    )(*([o] * FINAL_SUBS + [g] * FINAL_SUBS + [xs] * FINAL_SUBS), mod_i, w_out,
      ln_g.reshape(1, D_MODEL), ln_b.reshape(1, D_MODEL))


def _rope_tables(n_tokens):
    pos = jnp.arange(n_tokens, dtype=jnp.int32)
    row = (pos // GRID_W).astype(_F32)
    col = (pos % GRID_W).astype(_F32)
    quarter = GLA_DK // 4
    inv = ROPE_BASE ** (-jnp.arange(quarter, dtype=_F32) / quarter)
    ang = jnp.concatenate([row[:, None] * inv, col[:, None] * inv], -1)
    cos, sin = jnp.cos(ang), jnp.sin(ang)
    cos = jnp.concatenate([cos, cos], -1)
    sin = jnp.concatenate([-sin, sin], -1)
    return (jnp.concatenate([jnp.ones((CTX_LEN, GLA_DK), _F32), cos], 0),
            jnp.concatenate([jnp.zeros((CTX_LEN, GLA_DK), _F32), sin], 0))


def kernel(x, c, ctx, c_ctx, ada_w, ada_b, ln_g, ln_b, w_out, gla_w_in, gla_dec_w1, gla_dec_w2,
           gla_dec_b, gla_norm_g, na_w_in, na_rpb):
    B, S, D = x.shape
    assert (B, S, D) == (8, 2048, D_MODEL) and ctx.shape == (B, CTX_LEN, D)
    cond = jnp.zeros((COND_ROWS, D), _F32).at[0:B].set(c).at[CTX_COND_ROW].set(c_ctx)
    mod = _modulation(cond, ada_w, ada_b).reshape(DEPTH, COND_ROWS, 1, 3 * D_MODEL)
    cos_t, sin_t = _rope_tables(S)

    layer_in = (ctx, x)
    merge = None
    for i in range(DEPTH):
        j = i // N_MIXERS
        if i % N_MIXERS == 0:
            w1 = jnp.concatenate([gla_dec_w1[j, 0], gla_dec_w1[j, 1]], axis=1)
            w1 = jnp.pad(w1, ((0, 0), (0, LOW_PAD - 2 * GLA_GATE_RANK)))
            w_cat = jnp.concatenate([gla_w_in[j], w1], axis=1).astype(_BF16)
            w2_cat = jnp.zeros((LOW_PAD, 2 * GLA_KEY_W), _F32)
            w2_cat = w2_cat.at[0:GLA_GATE_RANK, 0:GLA_KEY_W].set(gla_dec_w2[j, 0])
            w2_cat = w2_cat.at[GLA_GATE_RANK:2 * GLA_GATE_RANK, GLA_KEY_W:].set(gla_dec_w2[j, 1])
            dec_b = gla_dec_b[j].reshape(1, 2 * GLA_KEY_W)
            x_new, (qf, kf, kef, qb, kb, keb, v, g, dec) = _gla_project(
                layer_in, merge, mod[i], w_cat, w2_cat.astype(_BF16), dec_b, cos_t, sin_t)
            o = _gla_mix(qf, kf, kef, qb, kb, keb, v, dec, gla_norm_g[j])
        else:
            x_new, (q, k, v, g) = _na_project(layer_in, merge, mod[i], na_w_in[j].astype(_BF16))
            o = _na_mix(q, k, v, _na_bias_table(na_rpb[j]), ctx_queries=i < DEPTH - 1)
        if x_new is not None:
            layer_in = (None, x_new)
        merge = (o, g) + layer_in + (mod[i], w_out[i].astype(_BF16), ln_g[i], ln_b[i])
    o, g, _, xs, mod_last, w_o, lg, lb = merge
    return _final_merge(o, g, xs, mod_last, w_o, lg, lb)
```

```python
import functools

import numpy as np
import jax
import jax.numpy as jnp
from jax import lax
from jax.experimental import pallas as pl
from jax.experimental.pallas import tpu as pltpu

D_MODEL = 1024
DEPTH = 4
GRID_W = 64
CTX_LEN = 256
N_MIXERS = 2
BRANCH = D_MODEL
GLA_HEADS = 4
GLA_DK = 128
GLA_DV = 256
GLA_KEY_W = GLA_HEADS * GLA_DK
GLA_GATE_RANK = 16
GLA_TAU = 16.0
GLA_CHUNK = 64
NA_HEADS = 16
NA_DH = 64
NA_KH = 8
NA_KW = 16
ROPE_BASE = 10000.0
LN_EPS = 1e-5
NORM_EPS = 1e-6
ALPHA = (2 * DEPTH) ** 0.25

LANES = 128
TOKEN_TILE = 256
SUBS_PER_STEP = 3
STEP_ROWS = SUBS_PER_STEP * TOKEN_TILE
CHUNKS_PER_TILE = TOKEN_TILE // GLA_CHUNK
COND_ROWS = 16
CTX_COND_ROW = 8
LOW_PAD = 128
NEG_BIG = -1e30
LOG2E = 1.4426950408889634
VMEM_LIMIT = 58 * 1024 * 1024

_F32 = jnp.float32
_BF16 = jnp.bfloat16


def _dot(a, b):
    return jnp.dot(a, b, preferred_element_type=_F32)


def _dot_nt(a, b):
    return lax.dot_general(a, b, (((1,), (1,)), ((), ())), preferred_element_type=_F32)


def _dot_tn(a, b):
    return lax.dot_general(a, b, (((0,), (0,)), ((), ())), preferred_element_type=_F32)


def _silu(x):
    return x * (1.0 / (1.0 + jnp.exp(-x)))


def _const_spec(shape):
    return pl.BlockSpec(shape, lambda b, t: (0,) * len(shape), pipeline_mode=pl.Buffered(1))


def _mod_kernel(cond_ref, w_ref, b_ref, out_ref):
    s = _silu(cond_ref[...]).astype(_BF16)
    out_ref[0] = _dot(s, w_ref[0].astype(_BF16)) + b_ref[0]


def _modulation(cond, ada_w, ada_b):
    n_col = 3 * D_MODEL // D_MODEL
    return pl.pallas_call(
        _mod_kernel,
        grid=(DEPTH, n_col),
        in_specs=[
            pl.BlockSpec((COND_ROWS, D_MODEL), lambda i, j: (0, 0)),
            pl.BlockSpec((1, D_MODEL, D_MODEL), lambda i, j: (i, 0, j)),
            pl.BlockSpec((1, 1, D_MODEL), lambda i, j: (i, 0, j)),
        ],
        out_specs=pl.BlockSpec((1, COND_ROWS, D_MODEL), lambda i, j: (i, 0, j)),
        out_shape=jax.ShapeDtypeStruct((DEPTH, COND_ROWS, 3 * D_MODEL), _F32),
        compiler_params=pltpu.CompilerParams(
            dimension_semantics=("arbitrary", "arbitrary"), vmem_limit_bytes=VMEM_LIMIT),
        name="modulation",
    )(cond, ada_w, ada_b.reshape(DEPTH, 1, 3 * D_MODEL))


def _mod_specs():
    return [pl.BlockSpec((1, 1, 3 * D_MODEL), lambda b, t: (b, 0, 0)),
            pl.BlockSpec((1, 1, 3 * D_MODEL), lambda b, t: (CTX_COND_ROW, 0, 0))]


def _sub_tile_mod(modb_ref, modc_ref, is_ctx):
    if is_ctx is False:
        return modb_ref[0]
    return jnp.where(is_ctx, modc_ref[0], modb_ref[0])


def _modulate(x, m):
    return (x * (1.0 + m[:, D_MODEL:2 * D_MODEL]) + m[:, 0:D_MODEL]).astype(_BF16)


N_STREAM_REFS = 1 + SUBS_PER_STEP
N_MERGE_REFS = 2 + N_STREAM_REFS + 5


def _stream_operands(ctx, lat):
    if ctx is None:
        ctx, shift = lat, 0
    else:
        shift = 1
    specs = [pl.BlockSpec((1, TOKEN_TILE, D_MODEL), lambda b, t: (b, 0, 0))]
    for s in range(SUBS_PER_STEP):
        specs.append(pl.BlockSpec(
            (1, TOKEN_TILE, D_MODEL),
            lambda b, t, s=s: (b, jnp.maximum(SUBS_PER_STEP * t + s - shift, 0), 0)))
    return specs, [ctx] + [lat] * SUBS_PER_STEP


def _stream_sub_tile(stream_refs, sub, t):
    x = stream_refs[1 + sub][0]
    if sub == 0:
        x = jnp.where(t == 0, stream_refs[0][0], x)
    return x


def _merged_sub_tile(o, g, x, gate, w_ref, lng_ref, lnb_ref):
    a = o * (g * (1.0 / (1.0 + jnp.exp(-g))))
    y = _dot(a, w_ref[...])
    z = x + (gate * (1.0 / ALPHA)) * y
    mu = jnp.mean(z, axis=-1, keepdims=True)
    zc = z - mu
    var = jnp.mean(zc * zc, axis=-1, keepdims=True)
    return zc * lax.rsqrt(var + LN_EPS / ALPHA ** 2) * lng_ref[...] + lnb_ref[...]


def _layer_input_tiles(refs, fused):
    t = pl.program_id(1)
    if not fused:
        stream, rest = refs[:N_STREAM_REFS], refs[N_STREAM_REFS:]
        return (lambda sub, xnew_ref: _stream_sub_tile(stream, sub, t)), rest
    o_ref, g_ref = refs[0], refs[1]
    stream = refs[2:2 + N_STREAM_REFS]
    pmodb_ref, pmodc_ref, wout_ref, lng_ref, lnb_ref = refs[2 + N_STREAM_REFS:N_MERGE_REFS]

    def tile(sub, xnew_ref):
        rows = slice(sub * TOKEN_TILE, (sub + 1) * TOKEN_TILE)
        m = _sub_tile_mod(pmodb_ref, pmodc_ref, (t == 0) if sub == 0 else False)
        x_new = _merged_sub_tile(o_ref[0, rows, :], g_ref[0, rows, :], _stream_sub_tile(stream, sub, t),
                                 m[:, 2 * D_MODEL:3 * D_MODEL], wout_ref, lng_ref, lnb_ref)
        xnew_ref[0, rows, :] = x_new
        return x_new

    return tile, refs[N_MERGE_REFS:]


def _merge_operands(merge):
    if merge is None:
        return [], []
    o, g, ctx, lat, mod_prev, w_out, ln_g, ln_b = merge
    step = pl.BlockSpec((1, STEP_ROWS, BRANCH), lambda b, t: (b, t, 0))
    s_specs, s_args = _stream_operands(ctx, lat)
    specs = [step, step] + s_specs + _mod_specs() + [
        _const_spec((BRANCH, D_MODEL)), _const_spec((1, D_MODEL)), _const_spec((1, D_MODEL))]
    args = [o, g] + s_args + [mod_prev, mod_prev, w_out, ln_g.reshape(1, D_MODEL), ln_b.reshape(1, D_MODEL)]
    return specs, args


def _gla_proj_kernel(*refs, fused):
    tile_fn, refs = _layer_input_tiles(refs, fused)
    modb_ref, modc_ref, w_ref, w2_ref, decb_ref, cos_ref, sin_ref, tri_ref = refs[:8]
    outs = refs[8:]
    xnew_ref = outs[0] if fused else None
    qf_ref, kf_ref, kef_ref, qb_ref, kb_ref, keb_ref, v_ref, g_ref, dec_ref = outs[1 if fused else 0:]
    t = pl.program_id(1)
    kw = GLA_KEY_W
    tiles = []
    for sub in range(SUBS_PER_STEP):
        rows = slice(sub * TOKEN_TILE, (sub + 1) * TOKEN_TILE)
        m = _sub_tile_mod(modb_ref, modc_ref, (t == 0) if sub == 0 else False)
        hb = _modulate(tile_fn(sub, xnew_ref), m)
        q = _dot(hb, w_ref[:, 0:kw]) * (GLA_DK ** -0.5)
        k = _dot(hb, w_ref[:, kw:2 * kw])
        low = _dot(hb, w_ref[:, 2 * kw + 2 * BRANCH:2 * kw + 2 * BRANCH + LOW_PAD])
        logits = _dot(low.astype(_BF16), w2_ref[...]) + decb_ref[...]
        la = (jnp.minimum(logits, 0.0) - jnp.log(1.0 + jnp.exp(-jnp.abs(logits)))) * (LOG2E / GLA_TAU)
        tiles.append((rows, hb, q, k, la))
    for sub, (rows, hb, q, k, la) in enumerate(tiles):
        pre = _dot(tri_ref[...], la.astype(_BF16))
        v_ref[0, rows, :] = _dot(hb, w_ref[:, 2 * kw:2 * kw + BRANCH]).astype(_BF16)
        g_ref[0, rows, :] = _dot(hb, w_ref[:, 2 * kw + BRANCH:2 * kw + 2 * BRANCH]).astype(_BF16)
        key_outs = [r.at[0, rows, :] for r in (qf_ref, kf_ref, kef_ref, qb_ref, kb_ref, keb_ref)]
        _gla_decay_factors(q, k, la, pre, cos_ref[rows, :], sin_ref[rows, :], *key_outs, dec_ref.at[0, sub])


def _gla_decay_factors(q, k, la, pre, cos, sin, qf_ref, kf_ref, kef_ref, qb_ref, kb_ref, keb_ref, dec_ref):
    kw = GLA_KEY_W
    tot4 = jnp.concatenate([pre[(c + 1) * GLA_CHUNK - 1:(c + 1) * GLA_CHUNK, :] for c in range(CHUNKS_PER_TILE)],
                           axis=0)

    def chunk_rows(t4):
        return jnp.broadcast_to(t4[:, None, :], (CHUNKS_PER_TILE, GLA_CHUNK, GLA_DK)).reshape(TOKEN_TILE, GLA_DK)

    for h in range(GLA_HEADS):
        sl = slice(h * GLA_DK, (h + 1) * GLA_DK)
        sl_b = slice(kw + h * GLA_DK, kw + (h + 1) * GLA_DK)
        pre_f, tot_f = pre[:, sl], chunk_rows(tot4[:, sl])
        pre_b, tot_b = pre[:, sl_b], chunk_rows(tot4[:, sl_b])
        ante_b = pre_b - la[:, sl_b]
        qh = q[:, sl]
        kh = k[:, sl]
        qh = qh * cos + pltpu.roll(qh, GLA_DK // 2, 1) * sin
        kh = kh * cos + pltpu.roll(kh, GLA_DK // 2, 1) * sin
        qf_ref[:, sl] = (qh * jnp.exp2(pre_f)).astype(_BF16)
        kf_ref[:, sl] = (kh * jnp.exp2(-pre_f)).astype(_BF16)
        kef_ref[:, sl] = (kh * jnp.exp2(tot_f - pre_f)).astype(_BF16)
        qb_ref[:, sl] = (qh * jnp.exp2(tot_b - pre_b)).astype(_BF16)
        kb_ref[:, sl] = (kh * jnp.exp2(ante_b - tot_b)).astype(_BF16)
        keb_ref[:, sl] = (kh * jnp.exp2(ante_b)).astype(_BF16)

    dec_ref[...] = jnp.exp2(jnp.concatenate([tot4[:, 0:kw], tot4[:, kw:2 * kw]], axis=0))


def _chunk_prefix_matrix():
    i = np.arange(TOKEN_TILE)
    same = (i[:, None] // GLA_CHUNK) == (i[None, :] // GLA_CHUNK)
    return jnp.asarray(same & (i[None, :] <= i[:, None]), _BF16)


def _gla_project(layer_in, merge, mod_i, w_cat, w2_cat, dec_b, cos_t, sin_t):
    fused = merge is not None
    B = (merge[0] if fused else layer_in[1]).shape[0]
    L = CTX_LEN + 2048
    n_w = w_cat.shape[1]
    tri = _chunk_prefix_matrix()
    in_specs, in_args = _merge_operands(merge) if fused else _stream_operands(*layer_in)
    key_spec = pl.BlockSpec((1, STEP_ROWS, GLA_KEY_W), lambda b, t: (b, t, 0))
    val_spec = pl.BlockSpec((1, STEP_ROWS, BRANCH), lambda b, t: (b, t, 0))
    key_shape = jax.ShapeDtypeStruct((B, L, GLA_KEY_W), _BF16)
    val_shape = jax.ShapeDtypeStruct((B, L, BRANCH), _BF16)
    rope_spec = pl.BlockSpec((STEP_ROWS, GLA_DK), lambda b, t: (t, 0))
    x_specs = [pl.BlockSpec((1, STEP_ROWS, D_MODEL), lambda b, t: (b, t, 0))] if fused else []
    x_shapes = [jax.ShapeDtypeStruct((B, L, D_MODEL), _F32)] if fused else []
    outs = pl.pallas_call(
        functools.partial(_gla_proj_kernel, fused=fused),
        grid=(B, L // STEP_ROWS),
        in_specs=in_specs + _mod_specs() + [
            _const_spec((D_MODEL, n_w)),
            _const_spec((LOW_PAD, 2 * GLA_KEY_W)),
            _const_spec((1, 2 * GLA_KEY_W)),
            rope_spec, rope_spec,
            _const_spec((TOKEN_TILE, TOKEN_TILE)),
        ],
        out_specs=x_specs + [key_spec] * 6 + [val_spec] * 2 + [
            pl.BlockSpec((1, SUBS_PER_STEP, 2 * CHUNKS_PER_TILE, GLA_KEY_W), lambda b, t: (b, t, 0, 0))],
        out_shape=x_shapes + [key_shape] * 6 + [val_shape] * 2 + [
            jax.ShapeDtypeStruct((B, L // TOKEN_TILE, 2 * CHUNKS_PER_TILE, GLA_KEY_W), _F32)],
        compiler_params=pltpu.CompilerParams(
            dimension_semantics=("arbitrary", "arbitrary"), vmem_limit_bytes=VMEM_LIMIT),
        name="gla_project",
    )(*in_args, mod_i, mod_i, w_cat, w2_cat, dec_b, cos_t, sin_t, tri)
    return (outs[0], outs[1:]) if fused else (None, outs)


GLA_HEADS_PER_STEP = 2
GLA_CHUNKS_PER_STEP = 4


def _gla_mix_kernel(qf_ref, kf_ref, kef_ref, qb_ref, kb_ref, keb_ref, v_ref, dec_ref, ng_ref,
                    out_ref, accf_ref, accb_ref, sf_ref, sb_ref):
    n_chunks = accf_ref.shape[0] // GLA_CHUNK
    n_ctx = CTX_LEN // GLA_CHUNK
    C = GLA_CHUNK
    sf_ref[...] = jnp.zeros_like(sf_ref)
    sb_ref[...] = jnp.zeros_like(sb_ref)
    ti = lax.broadcasted_iota(jnp.int32, (C, C), 0)
    si = lax.broadcasted_iota(jnp.int32, (C, C), 1)
    mask_f = si <= ti
    mask_b = si > ti
    sub8 = lax.broadcasted_iota(jnp.int32, (2 * CHUNKS_PER_TILE, GLA_DK), 0)

    fwd = (qf_ref, kf_ref, kef_ref, sf_ref, accf_ref, mask_f, 0)
    bwd = (qb_ref, kb_ref, keb_ref, sb_ref, accb_ref, mask_b, CHUNKS_PER_TILE)

    def body(i, carry):
        chains = []
        for u in range(GLA_CHUNKS_PER_STEP):
            cf = i * GLA_CHUNKS_PER_STEP + u
            cb = jnp.where(cf < n_ctx, n_ctx - 1 - cf, n_chunks + n_ctx - 1 - cf)
            for c, (q_ref, k_ref, ke_ref, s_ref, acc_ref, mask, dec_row) in ((cf, fwd), (cb, bwd)):
                for h in range(GLA_HEADS_PER_STEP):
                    chains.append(dict(u=u, c=c, h=h, q_ref=q_ref, k_ref=k_ref, ke_ref=ke_ref, s_ref=s_ref,
                                       acc_ref=acc_ref, mask=mask, dec_row=dec_row))
        for ch in chains:
            c, h = ch["c"], ch["h"]
            ch["rows"] = pl.ds(pl.multiple_of(c * C, C), C)
            ch["ksl"] = slice(h * GLA_DK, (h + 1) * GLA_DK)
            ch["vsl"] = slice(h * GLA_DV, (h + 1) * GLA_DV)
            ch["q"] = ch["q_ref"][0, ch["rows"], ch["ksl"]]
            ch["v"] = v_ref[0, ch["rows"], ch["vsl"]]
            ch["att"] = _dot_nt(ch["q"], ch["k_ref"][0, ch["rows"], ch["ksl"]])
        for ch in chains:
            ch["kv"] = _dot_tn(ch["ke_ref"][0, ch["rows"], ch["ksl"]], ch["v"])
            tile = lax.shift_right_logical(ch["c"], CHUNKS_PER_TILE.bit_length() - 1)
            dec8 = dec_ref[0, tile, :, ch["ksl"]]
            pick = sub8 == ch["dec_row"] + (ch["c"] & (CHUNKS_PER_TILE - 1))
            dec = jnp.sum(jnp.where(pick, dec8, 0.0), axis=0, keepdims=True)
            dcol = jnp.broadcast_to(dec, (GLA_DK, GLA_DK)).T
            ch["decay"] = jnp.concatenate([dcol, dcol], axis=1)
        state = {}
        for u in range(GLA_CHUNKS_PER_STEP):
            for ch in chains:
                if ch["u"] != u:
                    continue
                key = (id(ch["s_ref"]), ch["h"])
                s = state[key] if u else ch["s_ref"][ch["h"]]
                att = jnp.where(ch["mask"], ch["att"], 0.0).astype(_BF16)
                lhs = jnp.concatenate([ch["q"], att], axis=1)
                rhs = jnp.concatenate([s.astype(_BF16), ch["v"]], axis=0)
                ch["acc_ref"][ch["rows"], ch["vsl"]] = _dot(lhs, rhs)
                state[key] = s * ch["decay"] + ch["kv"]
        for ch in chains:
            if ch["u"] == GLA_CHUNKS_PER_STEP - 1:
                ch["s_ref"][ch["h"]] = state[(id(ch["s_ref"]), ch["h"])]
        return carry

    lax.fori_loop(0, n_chunks // GLA_CHUNKS_PER_STEP, body, 0)

    def norm_body(j, carry):
        rows = pl.ds(pl.multiple_of(j * TOKEN_TILE, TOKEN_TILE), TOKEN_TILE)
        for h in range(GLA_HEADS_PER_STEP):
            vsl = slice(h * GLA_DV, (h + 1) * GLA_DV)
            o = accf_ref[rows, vsl] + accb_ref[rows, vsl]
            ms = jnp.mean(o * o, axis=-1, keepdims=True)
            out_ref[0, rows, vsl] = (o * lax.rsqrt(ms + NORM_EPS) * ng_ref[...]).astype(_BF16)
        return carry

    lax.fori_loop(0, accf_ref.shape[0] // TOKEN_TILE, norm_body, 0)


def _gla_mix(qf, kf, kef, qb, kb, keb, v, dec, norm_g):
    B, L, _ = qf.shape
    hs = GLA_HEADS_PER_STEP
    n_t = dec.shape[1]
    key_spec = pl.BlockSpec((1, L, hs * GLA_DK), lambda b, p: (b, 0, p))
    val_spec = pl.BlockSpec((1, L, hs * GLA_DV), lambda b, p: (b, 0, p))
    return pl.pallas_call(
        _gla_mix_kernel,
        grid=(B, GLA_HEADS // hs),
        in_specs=[key_spec] * 6 + [
            val_spec,
            pl.BlockSpec((1, n_t, 2 * CHUNKS_PER_TILE, hs * GLA_DK), lambda b, p: (b, 0, 0, p)),
            pl.BlockSpec((1, GLA_DV), lambda b, p: (0, 0)),
        ],
        out_specs=val_spec,
        out_shape=jax.ShapeDtypeStruct((B, L, BRANCH), _BF16),
        scratch_shapes=[
            pltpu.VMEM((L, hs * GLA_DV), _F32),
            pltpu.VMEM((L, hs * GLA_DV), _F32),
            pltpu.VMEM((hs, GLA_DK, GLA_DV), _F32),
            pltpu.VMEM((hs, GLA_DK, GLA_DV), _F32),
        ],
        compiler_params=pltpu.CompilerParams(
            dimension_semantics=("arbitrary", "arbitrary"), vmem_limit_bytes=VMEM_LIMIT),
        name="gla_mix",
    )(qf, kf, kef, qb, kb, keb, v, dec, norm_g.reshape(1, GLA_DV))


def _na_proj_kernel(*refs, fused):
    tile_fn, refs = _layer_input_tiles(refs, fused)
    modb_ref, modc_ref, w_ref = refs[:3]
    outs = refs[3:]
    xnew_ref = outs[0] if fused else None
    q_ref, k_ref, v_ref, g_ref = outs[1 if fused else 0:]
    t = pl.program_id(1)
    for sub in range(SUBS_PER_STEP):
        rows = slice(sub * TOKEN_TILE, (sub + 1) * TOKEN_TILE)
        m = _sub_tile_mod(modb_ref, modc_ref, (t == 0) if sub == 0 else False)
        hb = _modulate(tile_fn(sub, xnew_ref), m)
        q_ref[0, rows, :] = (_dot(hb, w_ref[:, 0:BRANCH]) * (NA_DH ** -0.5 * LOG2E)).astype(_BF16)
        k_ref[0, rows, :] = _dot(hb, w_ref[:, BRANCH:2 * BRANCH]).astype(_BF16)
        v_ref[0, rows, :] = _dot(hb, w_ref[:, 2 * BRANCH:3 * BRANCH]).astype(_BF16)
        g_ref[0, rows, :] = _dot(hb, w_ref[:, 3 * BRANCH:4 * BRANCH]).astype(_BF16)


def _na_project(layer_in, merge, mod_i, w_in):
    fused = merge is not None
    B = (merge[0] if fused else layer_in[1]).shape[0]
    L = CTX_LEN + 2048
    in_specs, in_args = _merge_operands(merge) if fused else _stream_operands(*layer_in)
    spec = pl.BlockSpec((1, STEP_ROWS, BRANCH), lambda b, t: (b, t, 0))
    shape = jax.ShapeDtypeStruct((B, L, BRANCH), _BF16)
    x_specs = [pl.BlockSpec((1, STEP_ROWS, D_MODEL), lambda b, t: (b, t, 0))] if fused else []
    x_shapes = [jax.ShapeDtypeStruct((B, L, D_MODEL), _F32)] if fused else []
    outs = pl.pallas_call(
        functools.partial(_na_proj_kernel, fused=fused),
        grid=(B, L // STEP_ROWS),
        in_specs=in_specs + _mod_specs() + [_const_spec((D_MODEL, 4 * BRANCH))],
        out_specs=x_specs + [spec] * 4,
        out_shape=x_shapes + [shape] * 4,
        compiler_params=pltpu.CompilerParams(
            dimension_semantics=("arbitrary", "arbitrary"), vmem_limit_bytes=VMEM_LIMIT),
        name="na_project",
    )(*in_args, mod_i, mod_i, w_in)
    return (outs[0], outs[1:]) if fused else (None, outs)


NA_ROWS = 2048 // GRID_W
NA_PAIR_W = 2 * NA_DH
NA_BAND = NA_KH * GRID_W
NA_ROWS_PER_STEP = 4
NA_BATCH_PER_STEP = 4


def _stack_heads(q2):
    lane = lax.broadcasted_iota(jnp.int32, q2.shape, 1)
    zero = jnp.zeros_like(q2)
    return jnp.concatenate([jnp.where(lane < NA_DH, q2, zero), jnp.where(lane >= NA_DH, q2, zero)], axis=0)


def _unstack_heads(o, n):
    lane = lax.broadcasted_iota(jnp.int32, (n, NA_PAIR_W), 1)
    return jnp.where(lane < NA_DH, o[0:n], o[n:2 * n])


def _na_mix_kernel(q_ref, k_ref, v_ref, bias_ref, out_ref, sl_a, sc_a, sl_b, sc_b, *, ctx_queries):
    for e in range(NA_BATCH_PER_STEP):
        if ctx_queries:
            kc = k_ref[e, 0:CTX_LEN, :]
            vc = v_ref[e, 0:CTX_LEN, :]
            qs = _stack_heads(q_ref[e, 0:CTX_LEN, :])
            s = _dot_nt(qs, kc)
            m = jnp.max(s, axis=-1, keepdims=True)
            p = jnp.exp2(s - m)
            l = jnp.sum(p, axis=-1, keepdims=True)
            o = _dot(p.astype(_BF16), vc) / l
            out_ref[e, 0:CTX_LEN, :] = _unstack_heads(o, CTX_LEN).astype(_BF16)
        else:
            out_ref[e, 0:CTX_LEN, :] = jnp.zeros((CTX_LEN, NA_PAIR_W), _BF16)

    U = NA_ROWS_PER_STEP
    M = 2 * GRID_W
    groups_per_elem = NA_ROWS // U
    n_groups = NA_BATCH_PER_STEP * groups_per_elem

    def row_slices(g, u):
        g = jnp.asarray(g, jnp.int32)
        e = lax.shift_right_logical(g, groups_per_elem.bit_length() - 1)
        r = (g & (groups_per_elem - 1)) * U + u
        r_start = jnp.clip(r - NA_KH // 2, 0, NA_ROWS - NA_KH)
        q_rows = pl.ds(pl.multiple_of(CTX_LEN + r * GRID_W, GRID_W), GRID_W)
        band = pl.ds(pl.multiple_of(CTX_LEN + r_start * GRID_W, GRID_W), NA_BAND)
        return e, q_rows, band, r - r_start

    def scores(g, sl_ref, sc_ref):
        qss = []
        for u in range(U):
            e, q_rows, band, row_class = row_slices(g, u)
            qss.append(_stack_heads(q_ref[e, q_rows, :]))
            sl_ref[u] = _dot_nt(qss[u], k_ref[e, band, :]) + bias_ref[0, row_class]
        sc_ref[...] = _dot_nt(jnp.concatenate(qss, axis=0), k_ref[e, 0:CTX_LEN, :])

    def finish(g, sl_ref, sc_ref):
        p_locs, p_ctxs, ls = [], [], []
        for u in range(U):
            s_loc = sl_ref[u]
            s_ctx = sc_ref[u * M:(u + 1) * M, :]
            m = jnp.maximum(jnp.max(s_loc, axis=-1, keepdims=True), jnp.max(s_ctx, axis=-1, keepdims=True))
            p_loc = jnp.exp2(s_loc - m)
            p_ctx = jnp.exp2(s_ctx - m)
            ls.append(jnp.sum(p_loc, axis=-1, keepdims=True) + jnp.sum(p_ctx, axis=-1, keepdims=True))
            p_locs.append(p_loc.astype(_BF16))
            p_ctxs.append(p_ctx.astype(_BF16))
        e = row_slices(g, 0)[0]
        o_ctx_all = _dot(jnp.concatenate(p_ctxs, axis=0), v_ref[e, 0:CTX_LEN, :])
        for u in range(U):
            _, q_rows, band, _ = row_slices(g, u)
            o = (_dot(p_locs[u], v_ref[e, band, :]) + o_ctx_all[u * M:(u + 1) * M]) / ls[u]
            out_ref[e, q_rows, :] = _unstack_heads(o, GRID_W).astype(_BF16)

    buf_a = (sl_a, sc_a)
    buf_b = (sl_b, sc_b)
    scores(0, *buf_a)

    def pair_body(i, carry):
        g = 2 * i
        scores(g + 1, *buf_b)
        finish(g, *buf_a)
        scores(g + 2, *buf_a)
        finish(g + 1, *buf_b)
        return carry

    lax.fori_loop(0, n_groups // 2 - 1, pair_body, 0)
    scores(n_groups - 1, *buf_b)
    finish(n_groups - 2, *buf_a)
    finish(n_groups - 1, *buf_b)


RPB_ROWS_PAD = 16


def _na_bias_kernel(rpb_ref, out_ref):
    c = lax.broadcasted_iota(jnp.int32, (GRID_W, LANES), 0)
    lane = lax.broadcasted_iota(jnp.int32, (GRID_W, LANES), 1)
    kc = lane & (GRID_W - 1)
    col_start = jnp.clip(c - NA_KW // 2, 0, GRID_W - NA_KW)
    valid = (kc >= col_start) & (kc < col_start + NA_KW)
    low_half = lane < GRID_W
    for a in range(2):
        toe = []
        for ro in range(2 * NA_KH - 1):
            w = jnp.broadcast_to(rpb_ref[0, a, ro:ro + 1, :], (GRID_W, LANES)) * LOG2E
            t_lo = pltpu.roll(w, LANES - (NA_KW - 1), 1, stride=1, stride_axis=0)
            t_hi = pltpu.roll(w, GRID_W - (NA_KW - 1), 1, stride=1, stride_axis=0)
            toe.append((t_lo, t_hi))
        for d in range(NA_KH):
            for jj in range(NA_KH // 2):
                ro = 2 * jj - d + NA_KH - 1
                tile = jnp.where(low_half, toe[ro][0], toe[ro + 1][1])
                out_ref[0, d, a * GRID_W:(a + 1) * GRID_W, jj * LANES:(jj + 1) * LANES] = (
                    jnp.where(valid, tile, NEG_BIG))


def _na_bias_table(rpb):
    n_pairs = NA_HEADS // 2
    n_ro, n_co = rpb.shape[1], rpb.shape[2]
    rpb = jnp.pad(rpb, ((0, 0), (0, RPB_ROWS_PAD - n_ro), (0, LANES - n_co)))
    return pl.pallas_call(
        _na_bias_kernel,
        grid=(n_pairs,),
        in_specs=[pl.BlockSpec((1, 2, RPB_ROWS_PAD, LANES), lambda p: (p, 0, 0, 0))],
        out_specs=pl.BlockSpec((1, NA_KH, 2 * GRID_W, NA_BAND), lambda p: (p, 0, 0, 0)),
        out_shape=jax.ShapeDtypeStruct((n_pairs, NA_KH, 2 * GRID_W, NA_BAND), _F32),
        compiler_params=pltpu.CompilerParams(dimension_semantics=("arbitrary",)),
        name="na_bias_table",
    )(rpb.reshape(n_pairs, 2, RPB_ROWS_PAD, LANES))


def _na_mix(q, k, v, bias, ctx_queries):
    B, L, _ = q.shape
    n_pairs = NA_HEADS // 2
    U = NA_ROWS_PER_STEP
    spec = pl.BlockSpec((NA_BATCH_PER_STEP, L, NA_PAIR_W), lambda p, b: (b, 0, p))
    score_bufs = [pltpu.VMEM((U, 2 * GRID_W, NA_BAND), _F32),
                  pltpu.VMEM((U * 2 * GRID_W, CTX_LEN), _F32)]
    return pl.pallas_call(
        functools.partial(_na_mix_kernel, ctx_queries=ctx_queries),
        grid=(n_pairs, B // NA_BATCH_PER_STEP),
        in_specs=[spec, spec, spec,
                  pl.BlockSpec((1, NA_KH, 2 * GRID_W, NA_BAND), lambda p, b: (p, 0, 0, 0))],
        out_specs=spec,
        out_shape=jax.ShapeDtypeStruct((B, L, BRANCH), _BF16),
        scratch_shapes=score_bufs + score_bufs,
        compiler_params=pltpu.CompilerParams(
            dimension_semantics=("arbitrary", "arbitrary"), vmem_limit_bytes=VMEM_LIMIT),
        name="na_mix",
    )(q, k, v, bias)


FINAL_SUBS = 2


def _final_merge_kernel(*refs):
    n = FINAL_SUBS
    o_refs, g_refs, x_refs = refs[0:n], refs[n:2 * n], refs[2 * n:3 * n]
    mod_ref, w_ref, lng_ref, lnb_ref, out_ref = refs[3 * n:]
    gate = mod_ref[0][:, 2 * D_MODEL:3 * D_MODEL]
    for sub in range(n):
        out_ref[0, sub * TOKEN_TILE:(sub + 1) * TOKEN_TILE, :] = _merged_sub_tile(
            o_refs[sub][0], g_refs[sub][0], x_refs[sub][0], gate, w_ref, lng_ref, lnb_ref)


def _final_merge(o, g, xs, mod_i, w_out, ln_g, ln_b):
    B, L, _ = xs.shape
    rows = FINAL_SUBS * TOKEN_TILE
    n_t = (L - CTX_LEN) // rows

    def sub_specs(width):
        return [pl.BlockSpec((1, TOKEN_TILE, width), lambda b, t, s=s: (b, FINAL_SUBS * t + s + 1, 0))
                for s in range(FINAL_SUBS)]

    return pl.pallas_call(
        _final_merge_kernel,
        grid=(B, n_t),
        in_specs=sub_specs(BRANCH) + sub_specs(BRANCH) + sub_specs(D_MODEL) + [
            _mod_specs()[0],
            _const_spec((BRANCH, D_MODEL)), _const_spec((1, D_MODEL)), _const_spec((1, D_MODEL)),
        ],
        out_specs=pl.BlockSpec((1, rows, D_MODEL), lambda b, t: (b, t, 0)),
        out_shape=jax.ShapeDtypeStruct((B, n_t * rows, D_MODEL), _F32),
        compiler_params=pltpu.CompilerParams(
            dimension_semantics=("arbitrary", "arbitrary"), vmem_limit_bytes=VMEM_LIMIT),
        name="final_merge",
    )(*([o] * FINAL_SUBS + [g] * FINAL_SUBS + [xs] * FINAL_SUBS), mod_i, w_out,
      ln_g.reshape(1, D_MODEL), ln_b.reshape(1, D_MODEL))


def _rope_tables(n_tokens):
    pos = jnp.arange(n_tokens, dtype=jnp.int32)
    row = (pos // GRID_W).astype(_F32)
    col = (pos % GRID_W).astype(_F32)
    quarter = GLA_DK // 4
    inv = ROPE_BASE ** (-jnp.arange(quarter, dtype=_F32) / quarter)
    ang = jnp.concatenate([row[:, None] * inv, col[:, None] * inv], -1)
    cos, sin = jnp.cos(ang), jnp.sin(ang)
    cos = jnp.concatenate([cos, cos], -1)
    sin = jnp.concatenate([-sin, sin], -1)
    return (jnp.concatenate([jnp.ones((CTX_LEN, GLA_DK), _F32), cos], 0),
            jnp.concatenate([jnp.zeros((CTX_LEN, GLA_DK), _F32), sin], 0))


def kernel(x, c, ctx, c_ctx, ada_w, ada_b, ln_g, ln_b, w_out, gla_w_in, gla_dec_w1, gla_dec_w2,
           gla_dec_b, gla_norm_g, na_w_in, na_rpb):
    B, S, D = x.shape
    assert (B, S, D) == (8, 2048, D_MODEL) and ctx.shape == (B, CTX_LEN, D)
    cond = jnp.zeros((COND_ROWS, D), _F32).at[0:B].set(c).at[CTX_COND_ROW].set(c_ctx)
    mod = _modulation(cond, ada_w, ada_b).reshape(DEPTH, COND_ROWS, 1, 3 * D_MODEL)
    cos_t, sin_t = _rope_tables(S)

    layer_in = (ctx, x)
    merge = None
    for i in range(DEPTH):
        j = i // N_MIXERS
        if i % N_MIXERS == 0:
            w1 = jnp.concatenate([gla_dec_w1[j, 0], gla_dec_w1[j, 1]], axis=1)
            w1 = jnp.pad(w1, ((0, 0), (0, LOW_PAD - 2 * GLA_GATE_RANK)))
            w_cat = jnp.concatenate([gla_w_in[j], w1], axis=1).astype(_BF16)
            w2_cat = jnp.zeros((LOW_PAD, 2 * GLA_KEY_W), _F32)
            w2_cat = w2_cat.at[0:GLA_GATE_RANK, 0:GLA_KEY_W].set(gla_dec_w2[j, 0])
            w2_cat = w2_cat.at[GLA_GATE_RANK:2 * GLA_GATE_RANK, GLA_KEY_W:].set(gla_dec_w2[j, 1])
            dec_b = gla_dec_b[j].reshape(1, 2 * GLA_KEY_W)
            x_new, (qf, kf, kef, qb, kb, keb, v, g, dec) = _gla_project(
                layer_in, merge, mod[i], w_cat, w2_cat.astype(_BF16), dec_b, cos_t, sin_t)
            o = _gla_mix(qf, kf, kef, qb, kb, keb, v, dec, gla_norm_g[j])
        else:
            x_new, (q, k, v, g) = _na_project(layer_in, merge, mod[i], na_w_in[j].astype(_BF16))
            o = _na_mix(q, k, v, _na_bias_table(na_rpb[j]), ctx_queries=i < DEPTH - 1)
        if x_new is not None:
            layer_in = (None, x_new)
        merge = (o, g) + layer_in + (mod[i], w_out[i].astype(_BF16), ln_g[i], ln_b[i])
    o, g, _, xs, mod_last, w_o, lg, lb = merge
    return _final_merge(o, g, xs, mod_last, w_o, lg, lb)
```

```python
import functools

import numpy as np
import jax
import jax.numpy as jnp
from jax import lax
from jax.experimental import pallas as pl
from jax.experimental.pallas import tpu as pltpu

D_MODEL = 1024
DEPTH = 4
GRID_W = 64
CTX_LEN = 256
N_MIXERS = 2
BRANCH = D_MODEL
GLA_HEADS = 4
GLA_DK = 128
GLA_DV = 256
GLA_KEY_W = GLA_HEADS * GLA_DK
GLA_GATE_RANK = 16
GLA_TAU = 16.0
GLA_CHUNK = 64
NA_HEADS = 16
NA_DH = 64
NA_KH = 8
NA_KW = 16
ROPE_BASE = 10000.0
LN_EPS = 1e-5
NORM_EPS = 1e-6
ALPHA = (2 * DEPTH) ** 0.25

LANES = 128
TOKEN_TILE = 256
SUBS_PER_STEP = 3
STEP_ROWS = SUBS_PER_STEP * TOKEN_TILE
CHUNKS_PER_TILE = TOKEN_TILE // GLA_CHUNK
COND_ROWS = 16
CTX_COND_ROW = 8
LOW_PAD = 128
NEG_BIG = -1e30
LOG2E = 1.4426950408889634
VMEM_LIMIT = 58 * 1024 * 1024

_F32 = jnp.float32
_BF16 = jnp.bfloat16


def _dot(a, b):
    return jnp.dot(a, b, preferred_element_type=_F32)


def _dot_nt(a, b):
    return lax.dot_general(a, b, (((1,), (1,)), ((), ())), preferred_element_type=_F32)


def _dot_tn(a, b):
    return lax.dot_general(a, b, (((0,), (0,)), ((), ())), preferred_element_type=_F32)


def _silu(x):
    return x * (1.0 / (1.0 + jnp.exp(-x)))


def _const_spec(shape):
    return pl.BlockSpec(shape, lambda b, t: (0,) * len(shape), pipeline_mode=pl.Buffered(1))


def _mod_kernel(cond_ref, w_ref, b_ref, out_ref):
    s = _silu(cond_ref[...]).astype(_BF16)
    out_ref[0] = _dot(s, w_ref[0].astype(_BF16)) + b_ref[0]


def _modulation(cond, ada_w, ada_b):
    n_col = 3 * D_MODEL // D_MODEL
    return pl.pallas_call(
        _mod_kernel,
        grid=(DEPTH, n_col),
        in_specs=[
            pl.BlockSpec((COND_ROWS, D_MODEL), lambda i, j: (0, 0)),
            pl.BlockSpec((1, D_MODEL, D_MODEL), lambda i, j: (i, 0, j)),
            pl.BlockSpec((1, 1, D_MODEL), lambda i, j: (i, 0, j)),
        ],
        out_specs=pl.BlockSpec((1, COND_ROWS, D_MODEL), lambda i, j: (i, 0, j)),
        out_shape=jax.ShapeDtypeStruct((DEPTH, COND_ROWS, 3 * D_MODEL), _F32),
        compiler_params=pltpu.CompilerParams(
            dimension_semantics=("arbitrary", "arbitrary"), vmem_limit_bytes=VMEM_LIMIT),
        name="modulation",
    )(cond, ada_w, ada_b.reshape(DEPTH, 1, 3 * D_MODEL))


def _mod_specs():
    return [pl.BlockSpec((1, 1, 3 * D_MODEL), lambda b, t: (b, 0, 0)),
            pl.BlockSpec((1, 1, 3 * D_MODEL), lambda b, t: (CTX_COND_ROW, 0, 0))]


def _sub_tile_mod(modb_ref, modc_ref, is_ctx):
    if is_ctx is False:
        return modb_ref[0]
    return jnp.where(is_ctx, modc_ref[0], modb_ref[0])


def _modulate(x, m):
    return (x * (1.0 + m[:, D_MODEL:2 * D_MODEL]) + m[:, 0:D_MODEL]).astype(_BF16)


N_STREAM_REFS = 1 + SUBS_PER_STEP
N_MERGE_REFS = 2 + N_STREAM_REFS + 5


def _stream_operands(ctx, lat):
    if ctx is None:
        ctx, shift = lat, 0
    else:
        shift = 1
    specs = [pl.BlockSpec((1, TOKEN_TILE, D_MODEL), lambda b, t: (b, 0, 0))]
    for s in range(SUBS_PER_STEP):
        specs.append(pl.BlockSpec(
            (1, TOKEN_TILE, D_MODEL),
            lambda b, t, s=s: (b, jnp.maximum(SUBS_PER_STEP * t + s - shift, 0), 0)))
    return specs, [ctx] + [lat] * SUBS_PER_STEP


def _stream_sub_tile(stream_refs, sub, t):
    x = stream_refs[1 + sub][0]
    if sub == 0:
        x = jnp.where(t == 0, stream_refs[0][0], x)
    return x


def _merged_sub_tile(o, g, x, gate, w_ref, lng_ref, lnb_ref):
    a = o * (g * (1.0 / (1.0 + jnp.exp(-g))))
    y = _dot(a, w_ref[...])
    z = x + (gate * (1.0 / ALPHA)) * y
    mu = jnp.mean(z, axis=-1, keepdims=True)
    zc = z - mu
    var = jnp.mean(zc * zc, axis=-1, keepdims=True)
    return zc * lax.rsqrt(var + LN_EPS / ALPHA ** 2) * lng_ref[...] + lnb_ref[...]


def _layer_input_tiles(refs, fused):
    t = pl.program_id(1)
    if not fused:
        stream, rest = refs[:N_STREAM_REFS], refs[N_STREAM_REFS:]
        return (lambda sub, xnew_ref: _stream_sub_tile(stream, sub, t)), rest
    o_ref, g_ref = refs[0], refs[1]
    stream = refs[2:2 + N_STREAM_REFS]
    pmodb_ref, pmodc_ref, wout_ref, lng_ref, lnb_ref = refs[2 + N_STREAM_REFS:N_MERGE_REFS]

    def tile(sub, xnew_ref):
        rows = slice(sub * TOKEN_TILE, (sub + 1) * TOKEN_TILE)
        m = _sub_tile_mod(pmodb_ref, pmodc_ref, (t == 0) if sub == 0 else False)
        x_new = _merged_sub_tile(o_ref[0, rows, :], g_ref[0, rows, :], _stream_sub_tile(stream, sub, t),
                                 m[:, 2 * D_MODEL:3 * D_MODEL], wout_ref, lng_ref, lnb_ref)
        xnew_ref[0, rows, :] = x_new
        return x_new

    return tile, refs[N_MERGE_REFS:]


def _merge_operands(merge):
    if merge is None:
        return [], []
    o, g, ctx, lat, mod_prev, w_out, ln_g, ln_b = merge
    step = pl.BlockSpec((1, STEP_ROWS, BRANCH), lambda b, t: (b, t, 0))
    s_specs, s_args = _stream_operands(ctx, lat)
    specs = [step, step] + s_specs + _mod_specs() + [
        _const_spec((BRANCH, D_MODEL)), _const_spec((1, D_MODEL)), _const_spec((1, D_MODEL))]
    args = [o, g] + s_args + [mod_prev, mod_prev, w_out, ln_g.reshape(1, D_MODEL), ln_b.reshape(1, D_MODEL)]
    return specs, args


def _gla_proj_kernel(*refs, fused):
    tile_fn, refs = _layer_input_tiles(refs, fused)
    modb_ref, modc_ref, w_ref, w2_ref, decb_ref, cos_ref, sin_ref, tri_ref = refs[:8]
    outs = refs[8:]
    xnew_ref = outs[0] if fused else None
    qf_ref, kf_ref, kef_ref, qb_ref, kb_ref, keb_ref, v_ref, g_ref, dec_ref = outs[1 if fused else 0:]
    t = pl.program_id(1)
    kw = GLA_KEY_W
    tiles = []
    for sub in range(SUBS_PER_STEP):
        rows = slice(sub * TOKEN_TILE, (sub + 1) * TOKEN_TILE)
        m = _sub_tile_mod(modb_ref, modc_ref, (t == 0) if sub == 0 else False)
        hb = _modulate(tile_fn(sub, xnew_ref), m)
        q = _dot(hb, w_ref[:, 0:kw]) * (GLA_DK ** -0.5)
        k = _dot(hb, w_ref[:, kw:2 * kw])
        low = _dot(hb, w_ref[:, 2 * kw + 2 * BRANCH:2 * kw + 2 * BRANCH + LOW_PAD])
        y = _dot(low.astype(_BF16), w2_ref[...]) + decb_ref[...]
        la = (jnp.minimum(y, 0.0) - jnp.log2(1.0 + jnp.exp2(-jnp.abs(y)))) * (1.0 / GLA_TAU)
        tiles.append((rows, hb, q, k, la))
    for sub, (rows, hb, q, k, la) in enumerate(tiles):
        pre = _dot(tri_ref[...], la.astype(_BF16))
        v_ref[0, rows, :] = _dot(hb, w_ref[:, 2 * kw:2 * kw + BRANCH]).astype(_BF16)
        g_ref[0, rows, :] = _dot(hb, w_ref[:, 2 * kw + BRANCH:2 * kw + 2 * BRANCH]).astype(_BF16)
        key_outs = [r.at[0, rows, :] for r in (qf_ref, kf_ref, kef_ref, qb_ref, kb_ref, keb_ref)]
        _gla_decay_factors(q, k, la, pre, cos_ref[rows, :], sin_ref[rows, :], *key_outs, dec_ref.at[0, sub])


def _gla_decay_factors(q, k, la, pre, cos, sin, qf_ref, kf_ref, kef_ref, qb_ref, kb_ref, keb_ref, dec_ref):
    kw = GLA_KEY_W
    tot4 = jnp.concatenate([pre[(c + 1) * GLA_CHUNK - 1:(c + 1) * GLA_CHUNK, :] for c in range(CHUNKS_PER_TILE)],
                           axis=0)

    def chunk_rows(t4):
        return jnp.broadcast_to(t4[:, None, :], (CHUNKS_PER_TILE, GLA_CHUNK, GLA_DK)).reshape(TOKEN_TILE, GLA_DK)

    for h in range(GLA_HEADS):
        sl = slice(h * GLA_DK, (h + 1) * GLA_DK)
        sl_b = slice(kw + h * GLA_DK, kw + (h + 1) * GLA_DK)
        pre_f, tot_f = pre[:, sl], chunk_rows(tot4[:, sl])
        pre_b, tot_b = pre[:, sl_b], chunk_rows(tot4[:, sl_b])
        ante_b = pre_b - la[:, sl_b]
        qh = q[:, sl]
        kh = k[:, sl]
        qh = qh * cos + pltpu.roll(qh, GLA_DK // 2, 1) * sin
        kh = kh * cos + pltpu.roll(kh, GLA_DK // 2, 1) * sin
        qf_ref[:, sl] = (qh * jnp.exp2(pre_f)).astype(_BF16)
        kf_ref[:, sl] = (kh * jnp.exp2(-pre_f)).astype(_BF16)
        kef_ref[:, sl] = (kh * jnp.exp2(tot_f - pre_f)).astype(_BF16)
        qb_ref[:, sl] = (qh * jnp.exp2(tot_b - pre_b)).astype(_BF16)
        kb_ref[:, sl] = (kh * jnp.exp2(ante_b - tot_b)).astype(_BF16)
        keb_ref[:, sl] = (kh * jnp.exp2(ante_b)).astype(_BF16)

    dec_ref[...] = jnp.exp2(jnp.concatenate([tot4[:, 0:kw], tot4[:, kw:2 * kw]], axis=0))


def _chunk_prefix_matrix():
    i = np.arange(TOKEN_TILE)
    same = (i[:, None] // GLA_CHUNK) == (i[None, :] // GLA_CHUNK)
    return jnp.asarray(same & (i[None, :] <= i[:, None]), _BF16)


def _gla_project(layer_in, merge, mod_i, w_cat, w2_cat, dec_b, cos_t, sin_t):
    fused = merge is not None
    B = (merge[0] if fused else layer_in[1]).shape[0]
    L = CTX_LEN + 2048
    n_w = w_cat.shape[1]
    tri = _chunk_prefix_matrix()
    in_specs, in_args = _merge_operands(merge) if fused else _stream_operands(*layer_in)
    key_spec = pl.BlockSpec((1, STEP_ROWS, GLA_KEY_W), lambda b, t: (b, t, 0))
    val_spec = pl.BlockSpec((1, STEP_ROWS, BRANCH), lambda b, t: (b, t, 0))
    key_shape = jax.ShapeDtypeStruct((B, L, GLA_KEY_W), _BF16)
    val_shape = jax.ShapeDtypeStruct((B, L, BRANCH), _BF16)
    rope_spec = pl.BlockSpec((STEP_ROWS, GLA_DK), lambda b, t: (t, 0))
    x_specs = [pl.BlockSpec((1, STEP_ROWS, D_MODEL), lambda b, t: (b, t, 0))] if fused else []
    x_shapes = [jax.ShapeDtypeStruct((B, L, D_MODEL), _F32)] if fused else []
    outs = pl.pallas_call(
        functools.partial(_gla_proj_kernel, fused=fused),
        grid=(B, L // STEP_ROWS),
        in_specs=in_specs + _mod_specs() + [
            _const_spec((D_MODEL, n_w)),
            _const_spec((LOW_PAD, 2 * GLA_KEY_W)),
            _const_spec((1, 2 * GLA_KEY_W)),
            rope_spec, rope_spec,
            _const_spec((TOKEN_TILE, TOKEN_TILE)),
        ],
        out_specs=x_specs + [key_spec] * 6 + [val_spec] * 2 + [
            pl.BlockSpec((1, SUBS_PER_STEP, 2 * CHUNKS_PER_TILE, GLA_KEY_W), lambda b, t: (b, t, 0, 0))],
        out_shape=x_shapes + [key_shape] * 6 + [val_shape] * 2 + [
            jax.ShapeDtypeStruct((B, L // TOKEN_TILE, 2 * CHUNKS_PER_TILE, GLA_KEY_W), _F32)],
        compiler_params=pltpu.CompilerParams(
            dimension_semantics=("arbitrary", "arbitrary"), vmem_limit_bytes=VMEM_LIMIT),
        name="gla_project",
    )(*in_args, mod_i, mod_i, w_cat, w2_cat, dec_b, cos_t, sin_t, tri)
    return (outs[0], outs[1:]) if fused else (None, outs)


GLA_HEADS_PER_STEP = 2
GLA_CHUNKS_PER_STEP = 4


def _gla_mix_kernel(qf_ref, kf_ref, kef_ref, qb_ref, kb_ref, keb_ref, v_ref, dec_ref, ng_ref,
                    out_ref, accf_ref, accb_ref, sf_ref, sb_ref):
    n_chunks = accf_ref.shape[0] // GLA_CHUNK
    n_ctx = CTX_LEN // GLA_CHUNK
    C = GLA_CHUNK
    sf_ref[...] = jnp.zeros_like(sf_ref)
    sb_ref[...] = jnp.zeros_like(sb_ref)
    ti = lax.broadcasted_iota(jnp.int32, (C, C), 0)
    si = lax.broadcasted_iota(jnp.int32, (C, C), 1)
    mask_f = si <= ti
    mask_b = si > ti
    sub8 = lax.broadcasted_iota(jnp.int32, (2 * CHUNKS_PER_TILE, GLA_DK), 0)

    fwd = (qf_ref, kf_ref, kef_ref, sf_ref, accf_ref, mask_f, 0)
    bwd = (qb_ref, kb_ref, keb_ref, sb_ref, accb_ref, mask_b, CHUNKS_PER_TILE)

    def body(i, carry):
        chains = []
        for u in range(GLA_CHUNKS_PER_STEP):
            cf = i * GLA_CHUNKS_PER_STEP + u
            cb = jnp.where(cf < n_ctx, n_ctx - 1 - cf, n_chunks + n_ctx - 1 - cf)
            for c, (q_ref, k_ref, ke_ref, s_ref, acc_ref, mask, dec_row) in ((cf, fwd), (cb, bwd)):
                for h in range(GLA_HEADS_PER_STEP):
                    chains.append(dict(u=u, c=c, h=h, q_ref=q_ref, k_ref=k_ref, ke_ref=ke_ref, s_ref=s_ref,
                                       acc_ref=acc_ref, mask=mask, dec_row=dec_row))
        for ch in chains:
            c, h = ch["c"], ch["h"]
            ch["rows"] = pl.ds(pl.multiple_of(c * C, C), C)
            ch["ksl"] = slice(h * GLA_DK, (h + 1) * GLA_DK)
            ch["vsl"] = slice(h * GLA_DV, (h + 1) * GLA_DV)
            ch["q"] = ch["q_ref"][0, ch["rows"], ch["ksl"]]
            ch["v"] = v_ref[0, ch["rows"], ch["vsl"]]
            ch["att"] = _dot_nt(ch["q"], ch["k_ref"][0, ch["rows"], ch["ksl"]])
        for ch in chains:
            ch["kv"] = _dot_tn(ch["ke_ref"][0, ch["rows"], ch["ksl"]], ch["v"])
            tile = lax.shift_right_logical(ch["c"], CHUNKS_PER_TILE.bit_length() - 1)
            dec8 = dec_ref[0, tile, :, ch["ksl"]]
            pick = sub8 == ch["dec_row"] + (ch["c"] & (CHUNKS_PER_TILE - 1))
            dec = jnp.sum(jnp.where(pick, dec8, 0.0), axis=0, keepdims=True)
            dcol = jnp.broadcast_to(dec, (GLA_DK, GLA_DK)).T
            ch["decay"] = jnp.concatenate([dcol, dcol], axis=1)
        state = {}
        for u in range(GLA_CHUNKS_PER_STEP):
            for ch in chains:
                if ch["u"] != u:
                    continue
                key = (id(ch["s_ref"]), ch["h"])
                s = state[key] if u else ch["s_ref"][ch["h"]]
                att = jnp.where(ch["mask"], ch["att"], 0.0).astype(_BF16)
                lhs = jnp.concatenate([ch["q"], att], axis=1)
                rhs = jnp.concatenate([s.astype(_BF16), ch["v"]], axis=0)
                ch["acc_ref"][ch["rows"], ch["vsl"]] = _dot(lhs, rhs)
                state[key] = s * ch["decay"] + ch["kv"]
        for ch in chains:
            if ch["u"] == GLA_CHUNKS_PER_STEP - 1:
                ch["s_ref"][ch["h"]] = state[(id(ch["s_ref"]), ch["h"])]
        return carry

    lax.fori_loop(0, n_chunks // GLA_CHUNKS_PER_STEP, body, 0)

    def norm_body(j, carry):
        rows = pl.ds(pl.multiple_of(j * TOKEN_TILE, TOKEN_TILE), TOKEN_TILE)
        for h in range(GLA_HEADS_PER_STEP):
            vsl = slice(h * GLA_DV, (h + 1) * GLA_DV)
            o = accf_ref[rows, vsl] + accb_ref[rows, vsl]
            ms = jnp.mean(o * o, axis=-1, keepdims=True)
            out_ref[0, rows, vsl] = (o * lax.rsqrt(ms + NORM_EPS) * ng_ref[...]).astype(_BF16)
        return carry

    lax.fori_loop(0, accf_ref.shape[0] // TOKEN_TILE, norm_body, 0)


def _gla_mix(qf, kf, kef, qb, kb, keb, v, dec, norm_g):
    B, L, _ = qf.shape
    hs = GLA_HEADS_PER_STEP
    n_t = dec.shape[1]
    key_spec = pl.BlockSpec((1, L, hs * GLA_DK), lambda b, p: (b, 0, p))
    val_spec = pl.BlockSpec((1, L, hs * GLA_DV), lambda b, p: (b, 0, p))
    return pl.pallas_call(
        _gla_mix_kernel,
        grid=(B, GLA_HEADS // hs),
        in_specs=[key_spec] * 6 + [
            val_spec,
            pl.BlockSpec((1, n_t, 2 * CHUNKS_PER_TILE, hs * GLA_DK), lambda b, p: (b, 0, 0, p)),
            pl.BlockSpec((1, GLA_DV), lambda b, p: (0, 0)),
        ],
        out_specs=val_spec,
        out_shape=jax.ShapeDtypeStruct((B, L, BRANCH), _BF16),
        scratch_shapes=[
            pltpu.VMEM((L, hs * GLA_DV), _F32),
            pltpu.VMEM((L, hs * GLA_DV), _F32),
            pltpu.VMEM((hs, GLA_DK, GLA_DV), _F32),
            pltpu.VMEM((hs, GLA_DK, GLA_DV), _F32),
        ],
        compiler_params=pltpu.CompilerParams(
            dimension_semantics=("arbitrary", "arbitrary"), vmem_limit_bytes=VMEM_LIMIT),
        name="gla_mix",
    )(qf, kf, kef, qb, kb, keb, v, dec, norm_g.reshape(1, GLA_DV))


def _na_proj_kernel(*refs, fused):
    tile_fn, refs = _layer_input_tiles(refs, fused)
    modb_ref, modc_ref, w_ref = refs[:3]
    outs = refs[3:]
    xnew_ref = outs[0] if fused else None
    q_ref, k_ref, v_ref, g_ref = outs[1 if fused else 0:]
    t = pl.program_id(1)
    for sub in range(SUBS_PER_STEP):
        rows = slice(sub * TOKEN_TILE, (sub + 1) * TOKEN_TILE)
        m = _sub_tile_mod(modb_ref, modc_ref, (t == 0) if sub == 0 else False)
        hb = _modulate(tile_fn(sub, xnew_ref), m)
        q_ref[0, rows, :] = (_dot(hb, w_ref[:, 0:BRANCH]) * (NA_DH ** -0.5 * LOG2E)).astype(_BF16)
        k_ref[0, rows, :] = _dot(hb, w_ref[:, BRANCH:2 * BRANCH]).astype(_BF16)
        v_ref[0, rows, :] = _dot(hb, w_ref[:, 2 * BRANCH:3 * BRANCH]).astype(_BF16)
        g_ref[0, rows, :] = _dot(hb, w_ref[:, 3 * BRANCH:4 * BRANCH]).astype(_BF16)


def _na_project(layer_in, merge, mod_i, w_in):
    fused = merge is not None
    B = (merge[0] if fused else layer_in[1]).shape[0]
    L = CTX_LEN + 2048
    in_specs, in_args = _merge_operands(merge) if fused else _stream_operands(*layer_in)
    spec = pl.BlockSpec((1, STEP_ROWS, BRANCH), lambda b, t: (b, t, 0))
    shape = jax.ShapeDtypeStruct((B, L, BRANCH), _BF16)
    x_specs = [pl.BlockSpec((1, STEP_ROWS, D_MODEL), lambda b, t: (b, t, 0))] if fused else []
    x_shapes = [jax.ShapeDtypeStruct((B, L, D_MODEL), _F32)] if fused else []
    outs = pl.pallas_call(
        functools.partial(_na_proj_kernel, fused=fused),
        grid=(B, L // STEP_ROWS),
        in_specs=in_specs + _mod_specs() + [_const_spec((D_MODEL, 4 * BRANCH))],
        out_specs=x_specs + [spec] * 4,
        out_shape=x_shapes + [shape] * 4,
        compiler_params=pltpu.CompilerParams(
            dimension_semantics=("arbitrary", "arbitrary"), vmem_limit_bytes=VMEM_LIMIT),
        name="na_project",
    )(*in_args, mod_i, mod_i, w_in)
    return (outs[0], outs[1:]) if fused else (None, outs)


NA_ROWS = 2048 // GRID_W
NA_PAIR_W = 2 * NA_DH
NA_BAND = NA_KH * GRID_W
NA_ROWS_PER_STEP = 4
NA_BATCH_PER_STEP = 8


def _stack_heads(q2):
    lane = lax.broadcasted_iota(jnp.int32, q2.shape, 1)
    zero = jnp.zeros_like(q2)
    return jnp.concatenate([jnp.where(lane < NA_DH, q2, zero), jnp.where(lane >= NA_DH, q2, zero)], axis=0)


def _unstack_heads(o, n):
    lane = lax.broadcasted_iota(jnp.int32, (n, NA_PAIR_W), 1)
    return jnp.where(lane < NA_DH, o[0:n], o[n:2 * n])


def _na_mix_kernel(q_ref, k_ref, v_ref, bias_ref, out_ref, sl_a, sc_a, sl_b, sc_b, *, ctx_queries):
    for e in range(NA_BATCH_PER_STEP):
        if ctx_queries:
            kc = k_ref[e, 0:CTX_LEN, :]
            vc = v_ref[e, 0:CTX_LEN, :]
            qs = _stack_heads(q_ref[e, 0:CTX_LEN, :])
            s = _dot_nt(qs, kc)
            m = jnp.max(s, axis=-1, keepdims=True)
            p = jnp.exp2(s - m)
            l = jnp.sum(p, axis=-1, keepdims=True)
            o = _dot(p.astype(_BF16), vc) / l
            out_ref[e, 0:CTX_LEN, :] = _unstack_heads(o, CTX_LEN).astype(_BF16)
        else:
            out_ref[e, 0:CTX_LEN, :] = jnp.zeros((CTX_LEN, NA_PAIR_W), _BF16)

    U = NA_ROWS_PER_STEP
    M = 2 * GRID_W
    groups_per_elem = NA_ROWS // U
    n_groups = NA_BATCH_PER_STEP * groups_per_elem

    def row_slices(g, u):
        g = jnp.asarray(g, jnp.int32)
        e = lax.shift_right_logical(g, groups_per_elem.bit_length() - 1)
        r = (g & (groups_per_elem - 1)) * U + u
        r_start = jnp.clip(r - NA_KH // 2, 0, NA_ROWS - NA_KH)
        q_rows = pl.ds(pl.multiple_of(CTX_LEN + r * GRID_W, GRID_W), GRID_W)
        band = pl.ds(pl.multiple_of(CTX_LEN + r_start * GRID_W, GRID_W), NA_BAND)
        return e, q_rows, band, r - r_start

    def scores(g, sl_ref, sc_ref):
        qss = []
        for u in range(U):
            e, q_rows, band, row_class = row_slices(g, u)
            qss.append(_stack_heads(q_ref[e, q_rows, :]))
            sl_ref[u] = _dot_nt(qss[u], k_ref[e, band, :]) + bias_ref[0, row_class]
        sc_ref[...] = _dot_nt(jnp.concatenate(qss, axis=0), k_ref[e, 0:CTX_LEN, :])

    def finish(g, sl_ref, sc_ref):
        p_locs, p_ctxs, ls = [], [], []
        for u in range(U):
            s_loc = sl_ref[u]
            s_ctx = sc_ref[u * M:(u + 1) * M, :]
            m = jnp.maximum(jnp.max(s_loc, axis=-1, keepdims=True), jnp.max(s_ctx, axis=-1, keepdims=True))
            p_loc = jnp.exp2(s_loc - m)
            p_ctx = jnp.exp2(s_ctx - m)
            ls.append(jnp.sum(p_loc, axis=-1, keepdims=True) + jnp.sum(p_ctx, axis=-1, keepdims=True))
            p_locs.append(p_loc.astype(_BF16))
            p_ctxs.append(p_ctx.astype(_BF16))
        e = row_slices(g, 0)[0]
        o_ctx_all = _dot(jnp.concatenate(p_ctxs, axis=0), v_ref[e, 0:CTX_LEN, :])
        for u in range(U):
            _, q_rows, band, _ = row_slices(g, u)
            o = (_dot(p_locs[u], v_ref[e, band, :]) + o_ctx_all[u * M:(u + 1) * M]) / ls[u]
            out_ref[e, q_rows, :] = _unstack_heads(o, GRID_W).astype(_BF16)

    buf_a = (sl_a, sc_a)
    buf_b = (sl_b, sc_b)
    scores(0, *buf_a)

    def pair_body(i, carry):
        g = 2 * i
        scores(g + 1, *buf_b)
        finish(g, *buf_a)
        scores(g + 2, *buf_a)
        finish(g + 1, *buf_b)
        return carry

    lax.fori_loop(0, n_groups // 2 - 1, pair_body, 0)
    scores(n_groups - 1, *buf_b)
    finish(n_groups - 2, *buf_a)
    finish(n_groups - 1, *buf_b)


RPB_ROWS_PAD = 16


def _na_bias_kernel(rpb_ref, out_ref):
    c = lax.broadcasted_iota(jnp.int32, (GRID_W, LANES), 0)
    lane = lax.broadcasted_iota(jnp.int32, (GRID_W, LANES), 1)
    kc = lane & (GRID_W - 1)
    col_start = jnp.clip(c - NA_KW // 2, 0, GRID_W - NA_KW)
    valid = (kc >= col_start) & (kc < col_start + NA_KW)
    low_half = lane < GRID_W
    for a in range(2):
        toe = []
        for ro in range(2 * NA_KH - 1):
            w = jnp.broadcast_to(rpb_ref[0, a, ro:ro + 1, :], (GRID_W, LANES)) * LOG2E
            t_lo = pltpu.roll(w, LANES - (NA_KW - 1), 1, stride=1, stride_axis=0)
            t_hi = pltpu.roll(w, GRID_W - (NA_KW - 1), 1, stride=1, stride_axis=0)
            toe.append((t_lo, t_hi))
        for d in range(NA_KH):
            for jj in range(NA_KH // 2):
                ro = 2 * jj - d + NA_KH - 1
                tile = jnp.where(low_half, toe[ro][0], toe[ro + 1][1])
                out_ref[0, d, a * GRID_W:(a + 1) * GRID_W, jj * LANES:(jj + 1) * LANES] = (
                    jnp.where(valid, tile, NEG_BIG))


NA_PAIRS = NA_HEADS // 2


def _na_bias_tables(rpb):
    n_layers, _, n_ro, n_co = rpb.shape
    rpb = jnp.pad(rpb, ((0, 0), (0, 0), (0, RPB_ROWS_PAD - n_ro), (0, LANES - n_co)))
    n = n_layers * NA_PAIRS
    return pl.pallas_call(
        _na_bias_kernel,
        grid=(n,),
        in_specs=[pl.BlockSpec((1, 2, RPB_ROWS_PAD, LANES), lambda p: (p, 0, 0, 0))],
        out_specs=pl.BlockSpec((1, NA_KH, 2 * GRID_W, NA_BAND), lambda p: (p, 0, 0, 0)),
        out_shape=jax.ShapeDtypeStruct((n, NA_KH, 2 * GRID_W, NA_BAND), _F32),
        compiler_params=pltpu.CompilerParams(dimension_semantics=("arbitrary",)),
        name="na_bias_table",
    )(rpb.reshape(n, 2, RPB_ROWS_PAD, LANES))


def _na_mix(q, k, v, bias_tables, na_layer, ctx_queries):
    B, L, _ = q.shape
    n_pairs = NA_PAIRS
    U = NA_ROWS_PER_STEP
    spec = pl.BlockSpec((NA_BATCH_PER_STEP, L, NA_PAIR_W), lambda p, b: (b, 0, p))
    score_bufs = [pltpu.VMEM((U, 2 * GRID_W, NA_BAND), _F32),
                  pltpu.VMEM((U * 2 * GRID_W, CTX_LEN), _F32)]
    return pl.pallas_call(
        functools.partial(_na_mix_kernel, ctx_queries=ctx_queries),
        grid=(n_pairs, B // NA_BATCH_PER_STEP),
        in_specs=[spec, spec, spec,
                  pl.BlockSpec((1, NA_KH, 2 * GRID_W, NA_BAND),
                               lambda p, b: (na_layer * NA_PAIRS + p, 0, 0, 0))],
        out_specs=spec,
        out_shape=jax.ShapeDtypeStruct((B, L, BRANCH), _BF16),
        scratch_shapes=score_bufs + score_bufs,
        compiler_params=pltpu.CompilerParams(
            dimension_semantics=("arbitrary", "arbitrary"), vmem_limit_bytes=VMEM_LIMIT),
        name="na_mix",
    )(q, k, v, bias_tables)


FINAL_SUBS = 4


def _final_merge_kernel(*refs):
    n = FINAL_SUBS
    o_refs, g_refs, x_refs = refs[0:n], refs[n:2 * n], refs[2 * n:3 * n]
    mod_ref, w_ref, lng_ref, lnb_ref, out_ref = refs[3 * n:]
    gate = mod_ref[0][:, 2 * D_MODEL:3 * D_MODEL]
    for sub in range(n):
        out_ref[0, sub * TOKEN_TILE:(sub + 1) * TOKEN_TILE, :] = _merged_sub_tile(
            o_refs[sub][0], g_refs[sub][0], x_refs[sub][0], gate, w_ref, lng_ref, lnb_ref)


def _final_merge(o, g, xs, mod_i, w_out, ln_g, ln_b):
    B, L, _ = xs.shape
    rows = FINAL_SUBS * TOKEN_TILE
    n_t = (L - CTX_LEN) // rows

    def sub_specs(width):
        return [pl.BlockSpec((1, TOKEN_TILE, width), lambda b, t, s=s: (b, FINAL_SUBS * t + s + 1, 0))
                for s in range(FINAL_SUBS)]

    return pl.pallas_call(
        _final_merge_kernel,
        grid=(B, n_t),
        in_specs=sub_specs(BRANCH) + sub_specs(BRANCH) + sub_specs(D_MODEL) + [
            _mod_specs()[0],
            _const_spec((BRANCH, D_MODEL)), _const_spec((1, D_MODEL)), _const_spec((1, D_MODEL)),
        ],
        out_specs=pl.BlockSpec((1, rows, D_MODEL), lambda b, t: (b, t, 0)),
        out_shape=jax.ShapeDtypeStruct((B, n_t * rows, D_MODEL), _F32),
        compiler_params=pltpu.CompilerParams(
            dimension_semantics=("arbitrary", "arbitrary"), vmem_limit_bytes=VMEM_LIMIT),
        name="final_merge",
    )(*([o] * FINAL_SUBS + [g] * FINAL_SUBS + [xs] * FINAL_SUBS), mod_i, w_out,
      ln_g.reshape(1, D_MODEL), ln_b.reshape(1, D_MODEL))


def _rope_tables(n_tokens):
    pos = jnp.arange(n_tokens, dtype=jnp.int32)
    row = (pos // GRID_W).astype(_F32)
    col = (pos % GRID_W).astype(_F32)
    quarter = GLA_DK // 4
    inv = ROPE_BASE ** (-jnp.arange(quarter, dtype=_F32) / quarter)
    ang = jnp.concatenate([row[:, None] * inv, col[:, None] * inv], -1)
    cos, sin = jnp.cos(ang), jnp.sin(ang)
    cos = jnp.concatenate([cos, cos], -1)
    sin = jnp.concatenate([-sin, sin], -1)
    return (jnp.concatenate([jnp.ones((CTX_LEN, GLA_DK), _F32), cos], 0),
            jnp.concatenate([jnp.zeros((CTX_LEN, GLA_DK), _F32), sin], 0))


def kernel(x, c, ctx, c_ctx, ada_w, ada_b, ln_g, ln_b, w_out, gla_w_in, gla_dec_w1, gla_dec_w2,
           gla_dec_b, gla_norm_g, na_w_in, na_rpb):
    B, S, D = x.shape
    assert (B, S, D) == (8, 2048, D_MODEL) and ctx.shape == (B, CTX_LEN, D)
    cond = jnp.zeros((COND_ROWS, D), _F32).at[0:B].set(c).at[CTX_COND_ROW].set(c_ctx)
    mod = _modulation(cond, ada_w, ada_b).reshape(DEPTH, COND_ROWS, 1, 3 * D_MODEL)
    cos_t, sin_t = _rope_tables(S)
    bias_tables = _na_bias_tables(na_rpb)

    layer_in = (ctx, x)
    merge = None
    for i in range(DEPTH):
        j = i // N_MIXERS
        if i % N_MIXERS == 0:
            w1 = jnp.concatenate([gla_dec_w1[j, 0], gla_dec_w1[j, 1]], axis=1)
            w1 = jnp.pad(w1, ((0, 0), (0, LOW_PAD - 2 * GLA_GATE_RANK)))
            w_cat = jnp.concatenate([gla_w_in[j], w1], axis=1).astype(_BF16)
            w2_cat = jnp.zeros((LOW_PAD, 2 * GLA_KEY_W), _F32)
            w2_cat = w2_cat.at[0:GLA_GATE_RANK, 0:GLA_KEY_W].set(gla_dec_w2[j, 0])
            w2_cat = w2_cat.at[GLA_GATE_RANK:2 * GLA_GATE_RANK, GLA_KEY_W:].set(gla_dec_w2[j, 1])
            dec_b = gla_dec_b[j].reshape(1, 2 * GLA_KEY_W) * LOG2E
            x_new, (qf, kf, kef, qb, kb, keb, v, g, dec) = _gla_project(
                layer_in, merge, mod[i], w_cat, (w2_cat * LOG2E).astype(_BF16), dec_b, cos_t, sin_t)
            o = _gla_mix(qf, kf, kef, qb, kb, keb, v, dec, gla_norm_g[j])
        else:
            x_new, (q, k, v, g) = _na_project(layer_in, merge, mod[i], na_w_in[j].astype(_BF16))
            o = _na_mix(q, k, v, bias_tables, j, ctx_queries=i < DEPTH - 1)
        if x_new is not None:
            layer_in = (None, x_new)
        merge = (o, g) + layer_in + (mod[i], w_out[i].astype(_BF16), ln_g[i], ln_b[i])
    o, g, _, xs, mod_last, w_o, lg, lb = merge
    return _final_merge(o, g, xs, mod_last, w_o, lg, lb)
```

```python
import functools

import numpy as np
import jax
import jax.numpy as jnp
from jax import lax
from jax.experimental import pallas as pl
from jax.experimental.pallas import tpu as pltpu

D_MODEL = 1024
DEPTH = 4
GRID_W = 64
CTX_LEN = 256
N_MIXERS = 2
BRANCH = D_MODEL
GLA_HEADS = 4
GLA_DK = 128
GLA_DV = 256
GLA_KEY_W = GLA_HEADS * GLA_DK
GLA_GATE_RANK = 16
GLA_TAU = 16.0
GLA_CHUNK = 64
NA_HEADS = 16
NA_DH = 64
NA_KH = 8
NA_KW = 16
ROPE_BASE = 10000.0
LN_EPS = 1e-5
NORM_EPS = 1e-6
ALPHA = (2 * DEPTH) ** 0.25

LANES = 128
TOKEN_TILE = 256
SUBS_PER_STEP = 3
STEP_ROWS = SUBS_PER_STEP * TOKEN_TILE
CHUNKS_PER_TILE = TOKEN_TILE // GLA_CHUNK
COND_ROWS = 16
CTX_COND_ROW = 8
LOW_PAD = 128
NEG_BIG = -1e30
LOG2E = 1.4426950408889634
VMEM_LIMIT = 58 * 1024 * 1024

_F32 = jnp.float32
_BF16 = jnp.bfloat16


def _dot(a, b):
    return jnp.dot(a, b, preferred_element_type=_F32)


def _dot_nt(a, b):
    return lax.dot_general(a, b, (((1,), (1,)), ((), ())), preferred_element_type=_F32)


def _dot_tn(a, b):
    return lax.dot_general(a, b, (((0,), (0,)), ((), ())), preferred_element_type=_F32)


def _silu(x):
    return x * (1.0 / (1.0 + jnp.exp(-x)))


def _const_spec(shape):
    return pl.BlockSpec(shape, lambda b, t: (0,) * len(shape), pipeline_mode=pl.Buffered(1))


def _mod_kernel(cond_ref, w_ref, b_ref, out_ref):
    s = _silu(cond_ref[...]).astype(_BF16)
    out_ref[0] = _dot(s, w_ref[0].astype(_BF16)) + b_ref[0]


def _modulation(cond, ada_w, ada_b):
    n_col = 3 * D_MODEL // D_MODEL
    return pl.pallas_call(
        _mod_kernel,
        grid=(DEPTH, n_col),
        in_specs=[
            pl.BlockSpec((COND_ROWS, D_MODEL), lambda i, j: (0, 0)),
            pl.BlockSpec((1, D_MODEL, D_MODEL), lambda i, j: (i, 0, j)),
            pl.BlockSpec((1, 1, D_MODEL), lambda i, j: (i, 0, j)),
        ],
        out_specs=pl.BlockSpec((1, COND_ROWS, D_MODEL), lambda i, j: (i, 0, j)),
        out_shape=jax.ShapeDtypeStruct((DEPTH, COND_ROWS, 3 * D_MODEL), _F32),
        compiler_params=pltpu.CompilerParams(
            dimension_semantics=("arbitrary", "arbitrary"), vmem_limit_bytes=VMEM_LIMIT),
        name="modulation",
    )(cond, ada_w, ada_b.reshape(DEPTH, 1, 3 * D_MODEL))


def _mod_specs():
    return [pl.BlockSpec((1, 1, 3 * D_MODEL), lambda b, t: (b, 0, 0)),
            pl.BlockSpec((1, 1, 3 * D_MODEL), lambda b, t: (CTX_COND_ROW, 0, 0))]


def _sub_tile_mod(modb_ref, modc_ref, is_ctx):
    if is_ctx is False:
        return modb_ref[0]
    return jnp.where(is_ctx, modc_ref[0], modb_ref[0])


def _modulate(x, m):
    return (x * (1.0 + m[:, D_MODEL:2 * D_MODEL]) + m[:, 0:D_MODEL]).astype(_BF16)


N_STREAM_REFS = 1 + SUBS_PER_STEP
N_MERGE_REFS = 2 + N_STREAM_REFS + 5


def _stream_operands(ctx, lat):
    if ctx is None:
        ctx, shift = lat, 0
    else:
        shift = 1
    specs = [pl.BlockSpec((1, TOKEN_TILE, D_MODEL), lambda b, t: (b, 0, 0))]
    for s in range(SUBS_PER_STEP):
        specs.append(pl.BlockSpec(
            (1, TOKEN_TILE, D_MODEL),
            lambda b, t, s=s: (b, jnp.maximum(SUBS_PER_STEP * t + s - shift, 0), 0)))
    return specs, [ctx] + [lat] * SUBS_PER_STEP


def _stream_sub_tile(stream_refs, sub, t):
    x = stream_refs[1 + sub][0]
    if sub == 0:
        x = jnp.where(t == 0, stream_refs[0][0], x)
    return x


def _gated_out_proj(o, g, w_ref):
    return _dot(o * (g * (1.0 / (1.0 + jnp.exp(-g)))), w_ref[...])


def _residual_layer_norm(x, y, gate, lng_ref, lnb_ref):
    z = x + (gate * (1.0 / ALPHA)) * y
    mu = jnp.mean(z, axis=-1, keepdims=True)
    zc = z - mu
    var = jnp.mean(zc * zc, axis=-1, keepdims=True)
    return zc * lax.rsqrt(var + LN_EPS / ALPHA ** 2) * lng_ref[...] + lnb_ref[...]


def _layer_input_tiles(refs, fused):
    t = pl.program_id(1)
    if not fused:
        stream, rest = refs[:N_STREAM_REFS], refs[N_STREAM_REFS:]
        return (lambda sub, xnew_ref: _stream_sub_tile(stream, sub, t)), rest
    o_ref, g_ref = refs[0], refs[1]
    stream = refs[2:2 + N_STREAM_REFS]
    pmodb_ref, pmodc_ref, wout_ref, lng_ref, lnb_ref = refs[2 + N_STREAM_REFS:N_MERGE_REFS]
    ys = [_gated_out_proj(o_ref[0, sub * TOKEN_TILE:(sub + 1) * TOKEN_TILE, :],
                          g_ref[0, sub * TOKEN_TILE:(sub + 1) * TOKEN_TILE, :], wout_ref)
          for sub in range(SUBS_PER_STEP)]

    def tile(sub, xnew_ref):
        rows = slice(sub * TOKEN_TILE, (sub + 1) * TOKEN_TILE)
        m = _sub_tile_mod(pmodb_ref, pmodc_ref, (t == 0) if sub == 0 else False)
        x_new = _residual_layer_norm(_stream_sub_tile(stream, sub, t), ys[sub],
                                     m[:, 2 * D_MODEL:3 * D_MODEL], lng_ref, lnb_ref)
        xnew_ref[0, rows, :] = x_new
        return x_new

    return tile, refs[N_MERGE_REFS:]


def _merge_operands(merge):
    if merge is None:
        return [], []
    o, g, ctx, lat, mod_prev, w_out, ln_g, ln_b = merge
    step = pl.BlockSpec((1, STEP_ROWS, BRANCH), lambda b, t: (b, t, 0))
    s_specs, s_args = _stream_operands(ctx, lat)
    specs = [step, step] + s_specs + _mod_specs() + [
        _const_spec((BRANCH, D_MODEL)), _const_spec((1, D_MODEL)), _const_spec((1, D_MODEL))]
    args = [o, g] + s_args + [mod_prev, mod_prev, w_out, ln_g.reshape(1, D_MODEL), ln_b.reshape(1, D_MODEL)]
    return specs, args


def _gla_proj_kernel(*refs, fused):
    tile_fn, refs = _layer_input_tiles(refs, fused)
    modb_ref, modc_ref, w_ref, w2_ref, decb_ref, cos_ref, sin_ref, tri_ref = refs[:8]
    outs = refs[8:]
    xnew_ref = outs[0] if fused else None
    qf_ref, kf_ref, kef_ref, qb_ref, kb_ref, keb_ref, v_ref, g_ref, dec_ref = outs[1 if fused else 0:]
    t = pl.program_id(1)
    kw = GLA_KEY_W
    tiles = []
    for sub in range(SUBS_PER_STEP):
        rows = slice(sub * TOKEN_TILE, (sub + 1) * TOKEN_TILE)
        m = _sub_tile_mod(modb_ref, modc_ref, (t == 0) if sub == 0 else False)
        hb = _modulate(tile_fn(sub, xnew_ref), m)
        q = _rope(_dot(hb, w_ref[:, 0:kw]) * (GLA_DK ** -0.5), cos_ref[rows, :], sin_ref[rows, :])
        low = _dot(hb, w_ref[:, 2 * kw + 2 * BRANCH:2 * kw + 2 * BRANCH + LOW_PAD])
        y = _dot(low.astype(_BF16), w2_ref[...]) + decb_ref[...]
        la = (jnp.minimum(y, 0.0) - jnp.log2(1.0 + jnp.exp2(-jnp.abs(y)))) * (1.0 / GLA_TAU)
        tiles.append((rows, hb, q, la))
    for sub, (rows, hb, q, la) in enumerate(tiles):
        pre = _dot(tri_ref[...], la.astype(_BF16))
        k = _rope(_dot(hb, w_ref[:, kw:2 * kw]), cos_ref[rows, :], sin_ref[rows, :])
        v_ref[0, rows, :] = _dot(hb, w_ref[:, 2 * kw:2 * kw + BRANCH]).astype(_BF16)
        g_ref[0, rows, :] = _dot(hb, w_ref[:, 2 * kw + BRANCH:2 * kw + 2 * BRANCH]).astype(_BF16)
        key_outs = [r.at[0, rows, :] for r in (qf_ref, kf_ref, kef_ref, qb_ref, kb_ref, keb_ref)]
        _gla_decay_factors(q, k, la, pre, *key_outs, dec_ref.at[0, sub])


def _rope(x, cos, sin):
    heads = []
    for h in range(x.shape[1] // GLA_DK):
        xh = x[:, h * GLA_DK:(h + 1) * GLA_DK]
        heads.append(xh * cos + pltpu.roll(xh, GLA_DK // 2, 1) * sin)
    return jnp.concatenate(heads, axis=1)


def _gla_decay_factors(q, k, la, pre, qf_ref, kf_ref, kef_ref, qb_ref, kb_ref, keb_ref, dec_ref):
    kw = GLA_KEY_W
    tot4 = jnp.concatenate([pre[(c + 1) * GLA_CHUNK - 1:(c + 1) * GLA_CHUNK, :] for c in range(CHUNKS_PER_TILE)],
                           axis=0)

    def chunk_rows(t4):
        return jnp.broadcast_to(t4[:, None, :], (CHUNKS_PER_TILE, GLA_CHUNK, GLA_DK)).reshape(TOKEN_TILE, GLA_DK)

    for h in range(GLA_HEADS):
        sl = slice(h * GLA_DK, (h + 1) * GLA_DK)
        sl_b = slice(kw + h * GLA_DK, kw + (h + 1) * GLA_DK)
        pre_f, tot_f = pre[:, sl], chunk_rows(tot4[:, sl])
        pre_b, tot_b = pre[:, sl_b], chunk_rows(tot4[:, sl_b])
        ante_b = pre_b - la[:, sl_b]
        qh = q[:, sl]
        kh = k[:, sl]
        qf_ref[:, sl] = (qh * jnp.exp2(pre_f)).astype(_BF16)
        kf_ref[:, sl] = (kh * jnp.exp2(-pre_f)).astype(_BF16)
        kef_ref[:, sl] = (kh * jnp.exp2(tot_f - pre_f)).astype(_BF16)
        qb_ref[:, sl] = (qh * jnp.exp2(tot_b - pre_b)).astype(_BF16)
        kb_ref[:, sl] = (kh * jnp.exp2(ante_b - tot_b)).astype(_BF16)
        keb_ref[:, sl] = (kh * jnp.exp2(ante_b)).astype(_BF16)

    dec_ref[...] = jnp.exp2(jnp.concatenate([tot4[:, 0:kw], tot4[:, kw:2 * kw]], axis=0))


def _chunk_prefix_matrix():
    i = np.arange(TOKEN_TILE)
    same = (i[:, None] // GLA_CHUNK) == (i[None, :] // GLA_CHUNK)
    return jnp.asarray(same & (i[None, :] <= i[:, None]), _BF16)


def _gla_project(layer_in, merge, mod_i, w_cat, w2_cat, dec_b, cos_t, sin_t):
    fused = merge is not None
    B = (merge[0] if fused else layer_in[1]).shape[0]
    L = CTX_LEN + 2048
    n_w = w_cat.shape[1]
    tri = _chunk_prefix_matrix()
    in_specs, in_args = _merge_operands(merge) if fused else _stream_operands(*layer_in)
    key_spec = pl.BlockSpec((1, STEP_ROWS, GLA_KEY_W), lambda b, t: (b, t, 0))
    val_spec = pl.BlockSpec((1, STEP_ROWS, BRANCH), lambda b, t: (b, t, 0))
    key_shape = jax.ShapeDtypeStruct((B, L, GLA_KEY_W), _BF16)
    val_shape = jax.ShapeDtypeStruct((B, L, BRANCH), _BF16)
    rope_spec = pl.BlockSpec((STEP_ROWS, GLA_DK), lambda b, t: (t, 0))
    x_specs = [pl.BlockSpec((1, STEP_ROWS, D_MODEL), lambda b, t: (b, t, 0))] if fused else []
    x_shapes = [jax.ShapeDtypeStruct((B, L, D_MODEL), _F32)] if fused else []
    outs = pl.pallas_call(
        functools.partial(_gla_proj_kernel, fused=fused),
        grid=(B, L // STEP_ROWS),
        in_specs=in_specs + _mod_specs() + [
            _const_spec((D_MODEL, n_w)),
            _const_spec((LOW_PAD, 2 * GLA_KEY_W)),
            _const_spec((1, 2 * GLA_KEY_W)),
            rope_spec, rope_spec,
            _const_spec((TOKEN_TILE, TOKEN_TILE)),
        ],
        out_specs=x_specs + [key_spec] * 6 + [val_spec] * 2 + [
            pl.BlockSpec((1, SUBS_PER_STEP, 2 * CHUNKS_PER_TILE, GLA_KEY_W), lambda b, t: (b, t, 0, 0))],
        out_shape=x_shapes + [key_shape] * 6 + [val_shape] * 2 + [
            jax.ShapeDtypeStruct((B, L // TOKEN_TILE, 2 * CHUNKS_PER_TILE, GLA_KEY_W), _F32)],
        compiler_params=pltpu.CompilerParams(
            dimension_semantics=("arbitrary", "arbitrary"), vmem_limit_bytes=VMEM_LIMIT),
        name="gla_project",
    )(*in_args, mod_i, mod_i, w_cat, w2_cat, dec_b, cos_t, sin_t, tri)
    return (outs[0], outs[1:]) if fused else (None, outs)


GLA_HEADS_PER_STEP = 2
GLA_CHUNKS_PER_STEP = 4


def _gla_mix_kernel(qf_ref, kf_ref, kef_ref, qb_ref, kb_ref, keb_ref, v_ref, dec_ref, ng_ref,
                    out_ref, accf_ref, accb_ref, sf_ref, sb_ref):
    n_chunks = accf_ref.shape[0] // GLA_CHUNK
    n_ctx = CTX_LEN // GLA_CHUNK
    C = GLA_CHUNK
    sf_ref[...] = jnp.zeros_like(sf_ref)
    sb_ref[...] = jnp.zeros_like(sb_ref)
    ti = lax.broadcasted_iota(jnp.int32, (C, C), 0)
    si = lax.broadcasted_iota(jnp.int32, (C, C), 1)
    mask_f = si <= ti
    mask_b = si > ti
    sub8 = lax.broadcasted_iota(jnp.int32, (2 * CHUNKS_PER_TILE, GLA_DK), 0)

    fwd = (qf_ref, kf_ref, kef_ref, sf_ref, accf_ref, mask_f, 0)
    bwd = (qb_ref, kb_ref, keb_ref, sb_ref, accb_ref, mask_b, CHUNKS_PER_TILE)

    def body(i, carry):
        chains = []
        for u in range(GLA_CHUNKS_PER_STEP):
            cf = i * GLA_CHUNKS_PER_STEP + u
            cb = jnp.where(cf < n_ctx, n_ctx - 1 - cf, n_chunks + n_ctx - 1 - cf)
            for c, (q_ref, k_ref, ke_ref, s_ref, acc_ref, mask, dec_row) in ((cf, fwd), (cb, bwd)):
                for h in range(GLA_HEADS_PER_STEP):
                    chains.append(dict(u=u, c=c, h=h, q_ref=q_ref, k_ref=k_ref, ke_ref=ke_ref, s_ref=s_ref,
                                       acc_ref=acc_ref, mask=mask, dec_row=dec_row))
        for ch in chains:
            c, h = ch["c"], ch["h"]
            ch["rows"] = pl.ds(pl.multiple_of(c * C, C), C)
            ch["ksl"] = slice(h * GLA_DK, (h + 1) * GLA_DK)
            ch["vsl"] = slice(h * GLA_DV, (h + 1) * GLA_DV)
            ch["q"] = ch["q_ref"][0, ch["rows"], ch["ksl"]]
            ch["v"] = v_ref[0, ch["rows"], ch["vsl"]]
            ch["att"] = _dot_nt(ch["q"], ch["k_ref"][0, ch["rows"], ch["ksl"]])
        for ch in chains:
            ch["kv"] = _dot_tn(ch["ke_ref"][0, ch["rows"], ch["ksl"]], ch["v"])
            tile = lax.shift_right_logical(ch["c"], CHUNKS_PER_TILE.bit_length() - 1)
            dec8 = dec_ref[0, tile, :, ch["ksl"]]
            pick = sub8 == ch["dec_row"] + (ch["c"] & (CHUNKS_PER_TILE - 1))
            dec = jnp.sum(jnp.where(pick, dec8, 0.0), axis=0, keepdims=True)
            dcol = jnp.broadcast_to(dec, (GLA_DK, GLA_DK)).T
            ch["decay"] = jnp.concatenate([dcol, dcol], axis=1)
        state = {}
        for u in range(GLA_CHUNKS_PER_STEP):
            for ch in chains:
                if ch["u"] != u:
                    continue
                key = (id(ch["s_ref"]), ch["h"])
                s = state[key] if u else ch["s_ref"][ch["h"]]
                att = jnp.where(ch["mask"], ch["att"], 0.0).astype(_BF16)
                lhs = jnp.concatenate([ch["q"], att], axis=1)
                rhs = jnp.concatenate([s.astype(_BF16), ch["v"]], axis=0)
                ch["acc_ref"][ch["rows"], ch["vsl"]] = _dot(lhs, rhs)
                state[key] = s * ch["decay"] + ch["kv"]
        for ch in chains:
            if ch["u"] == GLA_CHUNKS_PER_STEP - 1:
                ch["s_ref"][ch["h"]] = state[(id(ch["s_ref"]), ch["h"])]
        return carry

    lax.fori_loop(0, n_chunks // GLA_CHUNKS_PER_STEP, body, 0)

    def norm_body(j, carry):
        rows = pl.ds(pl.multiple_of(j * TOKEN_TILE, TOKEN_TILE), TOKEN_TILE)
        for h in range(GLA_HEADS_PER_STEP):
            vsl = slice(h * GLA_DV, (h + 1) * GLA_DV)
            o = accf_ref[rows, vsl] + accb_ref[rows, vsl]
            ms = jnp.mean(o * o, axis=-1, keepdims=True)
            out_ref[0, rows, vsl] = (o * lax.rsqrt(ms + NORM_EPS) * ng_ref[...]).astype(_BF16)
        return carry

    lax.fori_loop(0, accf_ref.shape[0] // TOKEN_TILE, norm_body, 0)


def _gla_mix(qf, kf, kef, qb, kb, keb, v, dec, norm_g):
    B, L, _ = qf.shape
    hs = GLA_HEADS_PER_STEP
    n_t = dec.shape[1]
    key_spec = pl.BlockSpec((1, L, hs * GLA_DK), lambda b, p: (b, 0, p))
    val_spec = pl.BlockSpec((1, L, hs * GLA_DV), lambda b, p: (b, 0, p))
    return pl.pallas_call(
        _gla_mix_kernel,
        grid=(B, GLA_HEADS // hs),
        in_specs=[key_spec] * 6 + [
            val_spec,
            pl.BlockSpec((1, n_t, 2 * CHUNKS_PER_TILE, hs * GLA_DK), lambda b, p: (b, 0, 0, p)),
            pl.BlockSpec((1, GLA_DV), lambda b, p: (0, 0)),
        ],
        out_specs=val_spec,
        out_shape=jax.ShapeDtypeStruct((B, L, BRANCH), _BF16),
        scratch_shapes=[
            pltpu.VMEM((L, hs * GLA_DV), _F32),
            pltpu.VMEM((L, hs * GLA_DV), _F32),
            pltpu.VMEM((hs, GLA_DK, GLA_DV), _F32),
            pltpu.VMEM((hs, GLA_DK, GLA_DV), _F32),
        ],
        compiler_params=pltpu.CompilerParams(
            dimension_semantics=("arbitrary", "arbitrary"), vmem_limit_bytes=VMEM_LIMIT),
        name="gla_mix",
    )(qf, kf, kef, qb, kb, keb, v, dec, norm_g.reshape(1, GLA_DV))


def _na_proj_kernel(*refs, fused):
    tile_fn, refs = _layer_input_tiles(refs, fused)
    modb_ref, modc_ref, w_ref = refs[:3]
    outs = refs[3:]
    xnew_ref = outs[0] if fused else None
    q_ref, k_ref, v_ref, g_ref = outs[1 if fused else 0:]
    t = pl.program_id(1)
    for sub in range(SUBS_PER_STEP):
        rows = slice(sub * TOKEN_TILE, (sub + 1) * TOKEN_TILE)
        m = _sub_tile_mod(modb_ref, modc_ref, (t == 0) if sub == 0 else False)
        hb = _modulate(tile_fn(sub, xnew_ref), m)
        q_ref[0, rows, :] = (_dot(hb, w_ref[:, 0:BRANCH]) * (NA_DH ** -0.5 * LOG2E)).astype(_BF16)
        k_ref[0, rows, :] = _dot(hb, w_ref[:, BRANCH:2 * BRANCH]).astype(_BF16)
        v_ref[0, rows, :] = _dot(hb, w_ref[:, 2 * BRANCH:3 * BRANCH]).astype(_BF16)
        g_ref[0, rows, :] = _dot(hb, w_ref[:, 3 * BRANCH:4 * BRANCH]).astype(_BF16)


def _na_project(layer_in, merge, mod_i, w_in):
    fused = merge is not None
    B = (merge[0] if fused else layer_in[1]).shape[0]
    L = CTX_LEN + 2048
    in_specs, in_args = _merge_operands(merge) if fused else _stream_operands(*layer_in)
    spec = pl.BlockSpec((1, STEP_ROWS, BRANCH), lambda b, t: (b, t, 0))
    shape = jax.ShapeDtypeStruct((B, L, BRANCH), _BF16)
    x_specs = [pl.BlockSpec((1, STEP_ROWS, D_MODEL), lambda b, t: (b, t, 0))] if fused else []
    x_shapes = [jax.ShapeDtypeStruct((B, L, D_MODEL), _F32)] if fused else []
    outs = pl.pallas_call(
        functools.partial(_na_proj_kernel, fused=fused),
        grid=(B, L // STEP_ROWS),
        in_specs=in_specs + _mod_specs() + [_const_spec((D_MODEL, 4 * BRANCH))],
        out_specs=x_specs + [spec] * 4,
        out_shape=x_shapes + [shape] * 4,
        compiler_params=pltpu.CompilerParams(
            dimension_semantics=("arbitrary", "arbitrary"), vmem_limit_bytes=VMEM_LIMIT),
        name="na_project",
    )(*in_args, mod_i, mod_i, w_in)
    return (outs[0], outs[1:]) if fused else (None, outs)


NA_ROWS = 2048 // GRID_W
NA_PAIR_W = 2 * NA_DH
NA_BAND = NA_KH * GRID_W
NA_ROWS_PER_STEP = 4
NA_BATCH_PER_STEP = 8


def _stack_heads(q2):
    lane = lax.broadcasted_iota(jnp.int32, q2.shape, 1)
    zero = jnp.zeros_like(q2)
    return jnp.concatenate([jnp.where(lane < NA_DH, q2, zero), jnp.where(lane >= NA_DH, q2, zero)], axis=0)


def _unstack_heads(o, n):
    lane = lax.broadcasted_iota(jnp.int32, (n, NA_PAIR_W), 1)
    return jnp.where(lane < NA_DH, o[0:n], o[n:2 * n])


def _na_mix_kernel(q_ref, k_ref, v_ref, bias_ref, out_ref, sl_a, sc_a, sl_b, sc_b, *, ctx_queries):
    for e in range(NA_BATCH_PER_STEP):
        if ctx_queries:
            kc = k_ref[e, 0:CTX_LEN, :]
            vc = v_ref[e, 0:CTX_LEN, :]
            qs = _stack_heads(q_ref[e, 0:CTX_LEN, :])
            s = _dot_nt(qs, kc)
            m = jnp.max(s, axis=-1, keepdims=True)
            p = jnp.exp2(s - m)
            l = jnp.sum(p, axis=-1, keepdims=True)
            o = _dot(p.astype(_BF16), vc) / l
            out_ref[e, 0:CTX_LEN, :] = _unstack_heads(o, CTX_LEN).astype(_BF16)
        else:
            out_ref[e, 0:CTX_LEN, :] = jnp.zeros((CTX_LEN, NA_PAIR_W), _BF16)

    U = NA_ROWS_PER_STEP
    M = 2 * GRID_W
    groups_per_elem = NA_ROWS // U
    n_groups = NA_BATCH_PER_STEP * groups_per_elem

    def row_slices(g, u):
        g = jnp.asarray(g, jnp.int32)
        e = lax.shift_right_logical(g, groups_per_elem.bit_length() - 1)
        r = (g & (groups_per_elem - 1)) * U + u
        r_start = jnp.clip(r - NA_KH // 2, 0, NA_ROWS - NA_KH)
        q_rows = pl.ds(pl.multiple_of(CTX_LEN + r * GRID_W, GRID_W), GRID_W)
        band = pl.ds(pl.multiple_of(CTX_LEN + r_start * GRID_W, GRID_W), NA_BAND)
        return e, q_rows, band, r - r_start

    def scores(g, sl_ref, sc_ref):
        qss = []
        for u in range(U):
            e, q_rows, band, row_class = row_slices(g, u)
            qss.append(_stack_heads(q_ref[e, q_rows, :]))
            sl_ref[u] = _dot_nt(qss[u], k_ref[e, band, :]) + bias_ref[0, row_class]
        sc_ref[...] = _dot_nt(jnp.concatenate(qss, axis=0), k_ref[e, 0:CTX_LEN, :])

    def finish(g, sl_ref, sc_ref):
        p_locs, p_ctxs, ls = [], [], []
        for u in range(U):
            s_loc = sl_ref[u]
            s_ctx = sc_ref[u * M:(u + 1) * M, :]
            m = jnp.maximum(jnp.max(s_loc, axis=-1, keepdims=True), jnp.max(s_ctx, axis=-1, keepdims=True))
            p_loc = jnp.exp2(s_loc - m)
            p_ctx = jnp.exp2(s_ctx - m)
            ls.append(jnp.sum(p_loc, axis=-1, keepdims=True) + jnp.sum(p_ctx, axis=-1, keepdims=True))
            p_locs.append(p_loc.astype(_BF16))
            p_ctxs.append(p_ctx.astype(_BF16))
        e = row_slices(g, 0)[0]
        o_ctx_all = _dot(jnp.concatenate(p_ctxs, axis=0), v_ref[e, 0:CTX_LEN, :])
        for u in range(U):
            _, q_rows, band, _ = row_slices(g, u)
            o = (_dot(p_locs[u], v_ref[e, band, :]) + o_ctx_all[u * M:(u + 1) * M]) / ls[u]
            out_ref[e, q_rows, :] = _unstack_heads(o, GRID_W).astype(_BF16)

    buf_a = (sl_a, sc_a)
    buf_b = (sl_b, sc_b)
    scores(0, *buf_a)

    def pair_body(i, carry):
        g = 2 * i
        scores(g + 1, *buf_b)
        finish(g, *buf_a)
        scores(g + 2, *buf_a)
        finish(g + 1, *buf_b)
        return carry

    lax.fori_loop(0, n_groups // 2 - 1, pair_body, 0)
    scores(n_groups - 1, *buf_b)
    finish(n_groups - 2, *buf_a)
    finish(n_groups - 1, *buf_b)


RPB_ROWS_PAD = 16


def _na_bias_kernel(rpb_ref, out_ref):
    c = lax.broadcasted_iota(jnp.int32, (GRID_W, LANES), 0)
    lane = lax.broadcasted_iota(jnp.int32, (GRID_W, LANES), 1)
    kc = lane & (GRID_W - 1)
    col_start = jnp.clip(c - NA_KW // 2, 0, GRID_W - NA_KW)
    valid = (kc >= col_start) & (kc < col_start + NA_KW)
    low_half = lane < GRID_W
    for a in range(2):
        toe = []
        for ro in range(2 * NA_KH - 1):
            w = jnp.broadcast_to(rpb_ref[0, a, ro:ro + 1, :], (GRID_W, LANES)) * LOG2E
            t_lo = pltpu.roll(w, LANES - (NA_KW - 1), 1, stride=1, stride_axis=0)
            t_hi = pltpu.roll(w, GRID_W - (NA_KW - 1), 1, stride=1, stride_axis=0)
            toe.append((t_lo, t_hi))
        for d in range(NA_KH):
            for jj in range(NA_KH // 2):
                ro = 2 * jj - d + NA_KH - 1
                tile = jnp.where(low_half, toe[ro][0], toe[ro + 1][1])
                out_ref[0, d, a * GRID_W:(a + 1) * GRID_W, jj * LANES:(jj + 1) * LANES] = (
                    jnp.where(valid, tile, NEG_BIG))


NA_PAIRS = NA_HEADS // 2


def _na_bias_tables(rpb):
    n_layers, _, n_ro, n_co = rpb.shape
    rpb = jnp.pad(rpb, ((0, 0), (0, 0), (0, RPB_ROWS_PAD - n_ro), (0, LANES - n_co)))
    n = n_layers * NA_PAIRS
    return pl.pallas_call(
        _na_bias_kernel,
        grid=(n,),
        in_specs=[pl.BlockSpec((1, 2, RPB_ROWS_PAD, LANES), lambda p: (p, 0, 0, 0))],
        out_specs=pl.BlockSpec((1, NA_KH, 2 * GRID_W, NA_BAND), lambda p: (p, 0, 0, 0)),
        out_shape=jax.ShapeDtypeStruct((n, NA_KH, 2 * GRID_W, NA_BAND), _F32),
        compiler_params=pltpu.CompilerParams(dimension_semantics=("arbitrary",)),
        name="na_bias_table",
    )(rpb.reshape(n, 2, RPB_ROWS_PAD, LANES))


def _na_mix(q, k, v, bias_tables, na_layer, ctx_queries):
    B, L, _ = q.shape
    n_pairs = NA_PAIRS
    U = NA_ROWS_PER_STEP
    spec = pl.BlockSpec((NA_BATCH_PER_STEP, L, NA_PAIR_W), lambda p, b: (b, 0, p))
    score_bufs = [pltpu.VMEM((U, 2 * GRID_W, NA_BAND), _F32),
                  pltpu.VMEM((U * 2 * GRID_W, CTX_LEN), _F32)]
    return pl.pallas_call(
        functools.partial(_na_mix_kernel, ctx_queries=ctx_queries),
        grid=(n_pairs, B // NA_BATCH_PER_STEP),
        in_specs=[spec, spec, spec,
                  pl.BlockSpec((1, NA_KH, 2 * GRID_W, NA_BAND),
                               lambda p, b: (na_layer * NA_PAIRS + p, 0, 0, 0))],
        out_specs=spec,
        out_shape=jax.ShapeDtypeStruct((B, L, BRANCH), _BF16),
        scratch_shapes=score_bufs + score_bufs,
        compiler_params=pltpu.CompilerParams(
            dimension_semantics=("arbitrary", "arbitrary"), vmem_limit_bytes=VMEM_LIMIT),
        name="na_mix",
    )(q, k, v, bias_tables)


FINAL_SUBS = 4


def _final_merge_kernel(*refs):
    n = FINAL_SUBS
    o_refs, g_refs, x_refs = refs[0:n], refs[n:2 * n], refs[2 * n:3 * n]
    mod_ref, w_ref, lng_ref, lnb_ref, out_ref = refs[3 * n:]
    gate = mod_ref[0][:, 2 * D_MODEL:3 * D_MODEL]
    for sub in range(n):
        out_ref[0, sub * TOKEN_TILE:(sub + 1) * TOKEN_TILE, :] = _residual_layer_norm(
            x_refs[sub][0], _gated_out_proj(o_refs[sub][0], g_refs[sub][0], w_ref), gate, lng_ref, lnb_ref)


def _final_merge(o, g, xs, mod_i, w_out, ln_g, ln_b):
    B, L, _ = xs.shape
    rows = FINAL_SUBS * TOKEN_TILE
    n_t = (L - CTX_LEN) // rows

    def sub_specs(width):
        return [pl.BlockSpec((1, TOKEN_TILE, width), lambda b, t, s=s: (b, FINAL_SUBS * t + s + 1, 0))
                for s in range(FINAL_SUBS)]

    return pl.pallas_call(
        _final_merge_kernel,
        grid=(B, n_t),
        in_specs=sub_specs(BRANCH) + sub_specs(BRANCH) + sub_specs(D_MODEL) + [
            _mod_specs()[0],
            _const_spec((BRANCH, D_MODEL)), _const_spec((1, D_MODEL)), _const_spec((1, D_MODEL)),
        ],
        out_specs=pl.BlockSpec((1, rows, D_MODEL), lambda b, t: (b, t, 0)),
        out_shape=jax.ShapeDtypeStruct((B, n_t * rows, D_MODEL), _F32),
        compiler_params=pltpu.CompilerParams(
            dimension_semantics=("arbitrary", "arbitrary"), vmem_limit_bytes=VMEM_LIMIT),
        name="final_merge",
    )(*([o] * FINAL_SUBS + [g] * FINAL_SUBS + [xs] * FINAL_SUBS), mod_i, w_out,
      ln_g.reshape(1, D_MODEL), ln_b.reshape(1, D_MODEL))


def _rope_tables(n_tokens):
    pos = jnp.arange(n_tokens, dtype=jnp.int32)
    row = (pos // GRID_W).astype(_F32)
    col = (pos % GRID_W).astype(_F32)
    quarter = GLA_DK // 4
    inv = ROPE_BASE ** (-jnp.arange(quarter, dtype=_F32) / quarter)
    ang = jnp.concatenate([row[:, None] * inv, col[:, None] * inv], -1)
    cos, sin = jnp.cos(ang), jnp.sin(ang)
    cos = jnp.concatenate([cos, cos], -1)
    sin = jnp.concatenate([-sin, sin], -1)
    return (jnp.concatenate([jnp.ones((CTX_LEN, GLA_DK), _F32), cos], 0),
            jnp.concatenate([jnp.zeros((CTX_LEN, GLA_DK), _F32), sin], 0))


def kernel(x, c, ctx, c_ctx, ada_w, ada_b, ln_g, ln_b, w_out, gla_w_in, gla_dec_w1, gla_dec_w2,
           gla_dec_b, gla_norm_g, na_w_in, na_rpb):
    B, S, D = x.shape
    assert (B, S, D) == (8, 2048, D_MODEL) and ctx.shape == (B, CTX_LEN, D)
    cond = jnp.zeros((COND_ROWS, D), _F32).at[0:B].set(c).at[CTX_COND_ROW].set(c_ctx)
    mod = _modulation(cond, ada_w, ada_b).reshape(DEPTH, COND_ROWS, 1, 3 * D_MODEL)
    cos_t, sin_t = _rope_tables(S)
    bias_tables = _na_bias_tables(na_rpb)

    layer_in = (ctx, x)
    merge = None
    for i in range(DEPTH):
        j = i // N_MIXERS
        if i % N_MIXERS == 0:
            w1 = jnp.concatenate([gla_dec_w1[j, 0], gla_dec_w1[j, 1]], axis=1)
            w1 = jnp.pad(w1, ((0, 0), (0, LOW_PAD - 2 * GLA_GATE_RANK)))
            w_cat = jnp.concatenate([gla_w_in[j], w1], axis=1).astype(_BF16)
            w2_cat = jnp.zeros((LOW_PAD, 2 * GLA_KEY_W), _F32)
            w2_cat = w2_cat.at[0:GLA_GATE_RANK, 0:GLA_KEY_W].set(gla_dec_w2[j, 0])
            w2_cat = w2_cat.at[GLA_GATE_RANK:2 * GLA_GATE_RANK, GLA_KEY_W:].set(gla_dec_w2[j, 1])
            dec_b = gla_dec_b[j].reshape(1, 2 * GLA_KEY_W) * LOG2E
            x_new, (qf, kf, kef, qb, kb, keb, v, g, dec) = _gla_project(
                layer_in, merge, mod[i], w_cat, (w2_cat * LOG2E).astype(_BF16), dec_b, cos_t, sin_t)
            o = _gla_mix(qf, kf, kef, qb, kb, keb, v, dec, gla_norm_g[j])
        else:
            x_new, (q, k, v, g) = _na_project(layer_in, merge, mod[i], na_w_in[j].astype(_BF16))
            o = _na_mix(q, k, v, bias_tables, j, ctx_queries=i < DEPTH - 1)
        if x_new is not None:
            layer_in = (None, x_new)
        merge = (o, g) + layer_in + (mod[i], w_out[i].astype(_BF16), ln_g[i], ln_b[i])
    o, g, _, xs, mod_last, w_o, lg, lb = merge
    return _final_merge(o, g, xs, mod_last, w_o, lg, lb)
```

```python
import functools

import numpy as np
import jax
import jax.numpy as jnp
from jax import lax
from jax.experimental import pallas as pl
from jax.experimental.pallas import tpu as pltpu

D_MODEL = 1024
DEPTH = 4
GRID_W = 64
CTX_LEN = 256
N_MIXERS = 2
BRANCH = D_MODEL
GLA_HEADS = 4
GLA_DK = 128
GLA_DV = 256
GLA_KEY_W = GLA_HEADS * GLA_DK
GLA_GATE_RANK = 16
GLA_TAU = 16.0
GLA_CHUNK = 64
NA_HEADS = 16
NA_DH = 64
NA_KH = 8
NA_KW = 16
ROPE_BASE = 10000.0
LN_EPS = 1e-5
NORM_EPS = 1e-6
ALPHA = (2 * DEPTH) ** 0.25

LANES = 128
TOKEN_TILE = 256
SUBS_PER_STEP = 3
STEP_ROWS = SUBS_PER_STEP * TOKEN_TILE
CHUNKS_PER_TILE = TOKEN_TILE // GLA_CHUNK
COND_ROWS = 16
CTX_COND_ROW = 8
LOW_PAD = 128
NEG_BIG = -1e30
LOG2E = 1.4426950408889634
VMEM_LIMIT = 58 * 1024 * 1024

_F32 = jnp.float32
_BF16 = jnp.bfloat16


def _dot(a, b):
    return jnp.dot(a, b, preferred_element_type=_F32)


def _dot_nt(a, b):
    return lax.dot_general(a, b, (((1,), (1,)), ((), ())), preferred_element_type=_F32)


def _dot_tn(a, b):
    return lax.dot_general(a, b, (((0,), (0,)), ((), ())), preferred_element_type=_F32)


def _silu(x):
    return x * (1.0 / (1.0 + jnp.exp(-x)))


def _const_spec(shape):
    return pl.BlockSpec(shape, lambda b, t: (0,) * len(shape), pipeline_mode=pl.Buffered(1))


def _mod_kernel(cond_ref, w_ref, b_ref, out_ref):
    s = _silu(cond_ref[...]).astype(_BF16)
    out_ref[0] = _dot(s, w_ref[0].astype(_BF16)) + b_ref[0]


def _modulation(cond, ada_w, ada_b):
    n_col = 1
    width = 3 * D_MODEL // n_col
    return pl.pallas_call(
        _mod_kernel,
        grid=(DEPTH, n_col),
        in_specs=[
            pl.BlockSpec((COND_ROWS, D_MODEL), lambda i, j: (0, 0)),
            pl.BlockSpec((1, D_MODEL, width), lambda i, j: (i, 0, j)),
            pl.BlockSpec((1, 1, width), lambda i, j: (i, 0, j)),
        ],
        out_specs=pl.BlockSpec((1, COND_ROWS, width), lambda i, j: (i, 0, j)),
        out_shape=jax.ShapeDtypeStruct((DEPTH, COND_ROWS, 3 * D_MODEL), _F32),
        compiler_params=pltpu.CompilerParams(
            dimension_semantics=("arbitrary", "arbitrary"), vmem_limit_bytes=VMEM_LIMIT),
        name="modulation",
    )(cond, ada_w, ada_b.reshape(DEPTH, 1, 3 * D_MODEL))


def _mod_specs():
    return [pl.BlockSpec((1, 1, 3 * D_MODEL), lambda b, t: (b, 0, 0)),
            pl.BlockSpec((1, 1, 3 * D_MODEL), lambda b, t: (CTX_COND_ROW, 0, 0))]


def _sub_tile_mod(modb_ref, modc_ref, is_ctx):
    if is_ctx is False:
        return modb_ref[0]
    return jnp.where(is_ctx, modc_ref[0], modb_ref[0])


def _modulate(x, m):
    return (x * (1.0 + m[:, D_MODEL:2 * D_MODEL]) + m[:, 0:D_MODEL]).astype(_BF16)


N_STREAM_REFS = 1 + SUBS_PER_STEP
N_MERGE_REFS = 2 + N_STREAM_REFS + 5


def _stream_operands(ctx, lat):
    if ctx is None:
        ctx, shift = lat, 0
    else:
        shift = 1
    specs = [pl.BlockSpec((1, TOKEN_TILE, D_MODEL), lambda b, t: (b, 0, 0))]
    for s in range(SUBS_PER_STEP):
        specs.append(pl.BlockSpec(
            (1, TOKEN_TILE, D_MODEL),
            lambda b, t, s=s: (b, jnp.maximum(SUBS_PER_STEP * t + s - shift, 0), 0)))
    return specs, [ctx] + [lat] * SUBS_PER_STEP


def _stream_sub_tile(stream_refs, sub, t):
    x = stream_refs[1 + sub][0]
    if sub == 0:
        x = jnp.where(t == 0, stream_refs[0][0], x)
    return x


def _gated_out_proj(o, g, w_ref):
    return _dot(o * (g * (1.0 / (1.0 + jnp.exp(-g)))), w_ref[...])


def _residual_layer_norm(x, y, gate, lng_ref, lnb_ref):
    z = x + (gate * (1.0 / ALPHA)) * y
    mu = jnp.mean(z, axis=-1, keepdims=True)
    zc = z - mu
    var = jnp.mean(zc * zc, axis=-1, keepdims=True)
    return zc * lax.rsqrt(var + LN_EPS / ALPHA ** 2) * lng_ref[...] + lnb_ref[...]


def _layer_input_tiles(refs, fused):
    t = pl.program_id(1)
    if not fused:
        stream, rest = refs[:N_STREAM_REFS], refs[N_STREAM_REFS:]
        return (lambda sub, xnew_ref: _stream_sub_tile(stream, sub, t)), rest
    o_ref, g_ref = refs[0], refs[1]
    stream = refs[2:2 + N_STREAM_REFS]
    pmodb_ref, pmodc_ref, wout_ref, lng_ref, lnb_ref = refs[2 + N_STREAM_REFS:N_MERGE_REFS]
    ys = [_gated_out_proj(o_ref[0, sub * TOKEN_TILE:(sub + 1) * TOKEN_TILE, :],
                          g_ref[0, sub * TOKEN_TILE:(sub + 1) * TOKEN_TILE, :], wout_ref)
          for sub in range(SUBS_PER_STEP)]

    def tile(sub, xnew_ref):
        rows = slice(sub * TOKEN_TILE, (sub + 1) * TOKEN_TILE)
        m = _sub_tile_mod(pmodb_ref, pmodc_ref, (t == 0) if sub == 0 else False)
        x_new = _residual_layer_norm(_stream_sub_tile(stream, sub, t), ys[sub],
                                     m[:, 2 * D_MODEL:3 * D_MODEL], lng_ref, lnb_ref)
        xnew_ref[0, rows, :] = x_new
        return x_new

    return tile, refs[N_MERGE_REFS:]


def _merge_operands(merge):
    if merge is None:
        return [], []
    o, g, ctx, lat, mod_prev, w_out, ln_g, ln_b = merge
    step = pl.BlockSpec((1, STEP_ROWS, BRANCH), lambda b, t: (b, t, 0))
    s_specs, s_args = _stream_operands(ctx, lat)
    specs = [step, step] + s_specs + _mod_specs() + [
        _const_spec((BRANCH, D_MODEL)), _const_spec((1, D_MODEL)), _const_spec((1, D_MODEL))]
    args = [o, g] + s_args + [mod_prev, mod_prev, w_out, ln_g.reshape(1, D_MODEL), ln_b.reshape(1, D_MODEL)]
    return specs, args


def _gla_proj_kernel(*refs, fused):
    tile_fn, refs = _layer_input_tiles(refs, fused)
    modb_ref, modc_ref, w_ref, w2_ref, decb_ref, cos_ref, sin_ref, tri_ref = refs[:8]
    outs = refs[8:]
    xnew_ref = outs[0] if fused else None
    qf_ref, kf_ref, kef_ref, qb_ref, kb_ref, keb_ref, v_ref, g_ref, dec_ref = outs[1 if fused else 0:]
    t = pl.program_id(1)
    kw = GLA_KEY_W
    tiles = []
    for sub in range(SUBS_PER_STEP):
        rows = slice(sub * TOKEN_TILE, (sub + 1) * TOKEN_TILE)
        m = _sub_tile_mod(modb_ref, modc_ref, (t == 0) if sub == 0 else False)
        hb = _modulate(tile_fn(sub, xnew_ref), m)
        q = _rope(_dot(hb, w_ref[:, 0:kw]) * (GLA_DK ** -0.5), cos_ref[rows, :], sin_ref[rows, :])
        low = _dot(hb, w_ref[:, 2 * kw + 2 * BRANCH:2 * kw + 2 * BRANCH + LOW_PAD])
        y = _dot(low.astype(_BF16), w2_ref[...]) + decb_ref[...]
        la = (jnp.minimum(y, 0.0) - jnp.log2(1.0 + jnp.exp2(-jnp.abs(y)))) * (1.0 / GLA_TAU)
        tiles.append((rows, hb, q, la))
    for sub, (rows, hb, q, la) in enumerate(tiles):
        pre = _dot(tri_ref[...], la.astype(_BF16))
        k = _rope(_dot(hb, w_ref[:, kw:2 * kw]), cos_ref[rows, :], sin_ref[rows, :])
        v_ref[0, rows, :] = _dot(hb, w_ref[:, 2 * kw:2 * kw + BRANCH]).astype(_BF16)
        g_ref[0, rows, :] = _dot(hb, w_ref[:, 2 * kw + BRANCH:2 * kw + 2 * BRANCH]).astype(_BF16)
        key_outs = [r.at[0, rows, :] for r in (qf_ref, kf_ref, kef_ref, qb_ref, kb_ref, keb_ref)]
        _gla_decay_factors(q, k, la, pre, *key_outs, dec_ref.at[0, sub])


def _rope(x, cos, sin):
    heads = []
    for h in range(x.shape[1] // GLA_DK):
        xh = x[:, h * GLA_DK:(h + 1) * GLA_DK]
        heads.append(xh * cos + pltpu.roll(xh, GLA_DK // 2, 1) * sin)
    return jnp.concatenate(heads, axis=1)


def _gla_decay_factors(q, k, la, pre, qf_ref, kf_ref, kef_ref, qb_ref, kb_ref, keb_ref, dec_ref):
    kw = GLA_KEY_W
    tot4 = jnp.concatenate([pre[(c + 1) * GLA_CHUNK - 1:(c + 1) * GLA_CHUNK, :] for c in range(CHUNKS_PER_TILE)],
                           axis=0)

    def chunk_rows(t4):
        return jnp.broadcast_to(t4[:, None, :], (CHUNKS_PER_TILE, GLA_CHUNK, GLA_DK)).reshape(TOKEN_TILE, GLA_DK)

    for h in range(GLA_HEADS):
        sl = slice(h * GLA_DK, (h + 1) * GLA_DK)
        sl_b = slice(kw + h * GLA_DK, kw + (h + 1) * GLA_DK)
        pre_f, tot_f = pre[:, sl], chunk_rows(tot4[:, sl])
        pre_b, tot_b = pre[:, sl_b], chunk_rows(tot4[:, sl_b])
        ante_b = pre_b - la[:, sl_b]
        qh = q[:, sl]
        kh = k[:, sl]
        qf_ref[:, sl] = (qh * jnp.exp2(pre_f)).astype(_BF16)
        kf_ref[:, sl] = (kh * jnp.exp2(-pre_f)).astype(_BF16)
        kef_ref[:, sl] = (kh * jnp.exp2(tot_f - pre_f)).astype(_BF16)
        qb_ref[:, sl] = (qh * jnp.exp2(tot_b - pre_b)).astype(_BF16)
        kb_ref[:, sl] = (kh * jnp.exp2(ante_b - tot_b)).astype(_BF16)
        keb_ref[:, sl] = (kh * jnp.exp2(ante_b)).astype(_BF16)

    dec_ref[...] = jnp.exp2(jnp.concatenate([tot4[:, 0:kw], tot4[:, kw:2 * kw]], axis=0))


def _chunk_prefix_matrix():
    i = np.arange(TOKEN_TILE)
    same = (i[:, None] // GLA_CHUNK) == (i[None, :] // GLA_CHUNK)
    return jnp.asarray(same & (i[None, :] <= i[:, None]), _BF16)


def _gla_project(layer_in, merge, mod_i, w_cat, w2_cat, dec_b, cos_t, sin_t):
    fused = merge is not None
    B = (merge[0] if fused else layer_in[1]).shape[0]
    L = CTX_LEN + 2048
    n_w = w_cat.shape[1]
    tri = _chunk_prefix_matrix()
    in_specs, in_args = _merge_operands(merge) if fused else _stream_operands(*layer_in)
    key_spec = pl.BlockSpec((1, STEP_ROWS, GLA_KEY_W), lambda b, t: (b, t, 0))
    val_spec = pl.BlockSpec((1, STEP_ROWS, BRANCH), lambda b, t: (b, t, 0))
    key_shape = jax.ShapeDtypeStruct((B, L, GLA_KEY_W), _BF16)
    val_shape = jax.ShapeDtypeStruct((B, L, BRANCH), _BF16)
    rope_spec = pl.BlockSpec((STEP_ROWS, GLA_DK), lambda b, t: (t, 0))
    x_specs = [pl.BlockSpec((1, STEP_ROWS, D_MODEL), lambda b, t: (b, t, 0))] if fused else []
    x_shapes = [jax.ShapeDtypeStruct((B, L, D_MODEL), _F32)] if fused else []
    outs = pl.pallas_call(
        functools.partial(_gla_proj_kernel, fused=fused),
        grid=(B, L // STEP_ROWS),
        in_specs=in_specs + _mod_specs() + [
            _const_spec((D_MODEL, n_w)),
            _const_spec((LOW_PAD, 2 * GLA_KEY_W)),
            _const_spec((1, 2 * GLA_KEY_W)),
            rope_spec, rope_spec,
            _const_spec((TOKEN_TILE, TOKEN_TILE)),
        ],
        out_specs=x_specs + [key_spec] * 6 + [val_spec] * 2 + [
            pl.BlockSpec((1, SUBS_PER_STEP, 2 * CHUNKS_PER_TILE, GLA_KEY_W), lambda b, t: (b, t, 0, 0))],
        out_shape=x_shapes + [key_shape] * 6 + [val_shape] * 2 + [
            jax.ShapeDtypeStruct((B, L // TOKEN_TILE, 2 * CHUNKS_PER_TILE, GLA_KEY_W), _F32)],
        compiler_params=pltpu.CompilerParams(
            dimension_semantics=("arbitrary", "arbitrary"), vmem_limit_bytes=VMEM_LIMIT),
        name="gla_project",
    )(*in_args, mod_i, mod_i, w_cat, w2_cat, dec_b, cos_t, sin_t, tri)
    return (outs[0], outs[1:]) if fused else (None, outs)


GLA_HEADS_PER_STEP = 2
GLA_CHUNKS_PER_STEP = 6


def _gla_mix_kernel(qf_ref, kf_ref, kef_ref, qb_ref, kb_ref, keb_ref, v_ref, dec_ref, ng_ref,
                    out_ref, accf_ref, accb_ref, sf_ref, sb_ref):
    n_chunks = accf_ref.shape[0] // GLA_CHUNK
    n_ctx = CTX_LEN // GLA_CHUNK
    C = GLA_CHUNK
    sf_ref[...] = jnp.zeros_like(sf_ref)
    sb_ref[...] = jnp.zeros_like(sb_ref)
    ti = lax.broadcasted_iota(jnp.int32, (C, C), 0)
    si = lax.broadcasted_iota(jnp.int32, (C, C), 1)
    mask_f = si <= ti
    mask_b = si > ti
    sub8 = lax.broadcasted_iota(jnp.int32, (2 * CHUNKS_PER_TILE, GLA_DK), 0)

    fwd = (qf_ref, kf_ref, kef_ref, sf_ref, accf_ref, mask_f, 0)
    bwd = (qb_ref, kb_ref, keb_ref, sb_ref, accb_ref, mask_b, CHUNKS_PER_TILE)

    def body(i, carry):
        chains = []
        for u in range(GLA_CHUNKS_PER_STEP):
            cf = i * GLA_CHUNKS_PER_STEP + u
            cb = jnp.where(cf < n_ctx, n_ctx - 1 - cf, n_chunks + n_ctx - 1 - cf)
            for c, (q_ref, k_ref, ke_ref, s_ref, acc_ref, mask, dec_row) in ((cf, fwd), (cb, bwd)):
                for h in range(GLA_HEADS_PER_STEP):
                    chains.append(dict(u=u, c=c, h=h, q_ref=q_ref, k_ref=k_ref, ke_ref=ke_ref, s_ref=s_ref,
                                       acc_ref=acc_ref, mask=mask, dec_row=dec_row))
        for ch in chains:
            c, h = ch["c"], ch["h"]
            ch["rows"] = pl.ds(pl.multiple_of(c * C, C), C)
            ch["ksl"] = slice(h * GLA_DK, (h + 1) * GLA_DK)
            ch["vsl"] = slice(h * GLA_DV, (h + 1) * GLA_DV)
            ch["q"] = ch["q_ref"][0, ch["rows"], ch["ksl"]]
            ch["v"] = v_ref[0, ch["rows"], ch["vsl"]]
            ch["att"] = _dot_nt(ch["q"], ch["k_ref"][0, ch["rows"], ch["ksl"]])
        for ch in chains:
            ch["kv"] = _dot_tn(ch["ke_ref"][0, ch["rows"], ch["ksl"]], ch["v"])
            tile = lax.shift_right_logical(ch["c"], CHUNKS_PER_TILE.bit_length() - 1)
            dec8 = dec_ref[0, tile, :, ch["ksl"]]
            pick = sub8 == ch["dec_row"] + (ch["c"] & (CHUNKS_PER_TILE - 1))
            dec = jnp.sum(jnp.where(pick, dec8, 0.0), axis=0, keepdims=True)
            dcol = jnp.broadcast_to(dec, (GLA_DK, GLA_DK)).T
            ch["decay"] = jnp.concatenate([dcol, dcol], axis=1)
        state = {}
        for u in range(GLA_CHUNKS_PER_STEP):
            for ch in chains:
                if ch["u"] != u:
                    continue
                key = (id(ch["s_ref"]), ch["h"])
                s = state[key] if u else ch["s_ref"][ch["h"]]
                att = jnp.where(ch["mask"], ch["att"], 0.0).astype(_BF16)
                lhs = jnp.concatenate([ch["q"], att], axis=1)
                rhs = jnp.concatenate([s.astype(_BF16), ch["v"]], axis=0)
                ch["acc_ref"][ch["rows"], ch["vsl"]] = _dot(lhs, rhs)
                state[key] = s * ch["decay"] + ch["kv"]
        for ch in chains:
            if ch["u"] == GLA_CHUNKS_PER_STEP - 1:
                ch["s_ref"][ch["h"]] = state[(id(ch["s_ref"]), ch["h"])]
        return carry

    lax.fori_loop(0, n_chunks // GLA_CHUNKS_PER_STEP, body, 0)

    def norm_body(j, carry):
        rows = pl.ds(pl.multiple_of(j * TOKEN_TILE, TOKEN_TILE), TOKEN_TILE)
        for h in range(GLA_HEADS_PER_STEP):
            vsl = slice(h * GLA_DV, (h + 1) * GLA_DV)
            o = accf_ref[rows, vsl] + accb_ref[rows, vsl]
            ms = jnp.mean(o * o, axis=-1, keepdims=True)
            out_ref[0, rows, vsl] = (o * lax.rsqrt(ms + NORM_EPS) * ng_ref[...]).astype(_BF16)
        return carry

    lax.fori_loop(0, accf_ref.shape[0] // TOKEN_TILE, norm_body, 0)


def _gla_mix(qf, kf, kef, qb, kb, keb, v, dec, norm_g):
    B, L, _ = qf.shape
    hs = GLA_HEADS_PER_STEP
    n_t = dec.shape[1]
    key_spec = pl.BlockSpec((1, L, hs * GLA_DK), lambda b, p: (b, 0, p))
    val_spec = pl.BlockSpec((1, L, hs * GLA_DV), lambda b, p: (b, 0, p))
    return pl.pallas_call(
        _gla_mix_kernel,
        grid=(B, GLA_HEADS // hs),
        in_specs=[key_spec] * 6 + [
            val_spec,
            pl.BlockSpec((1, n_t, 2 * CHUNKS_PER_TILE, hs * GLA_DK), lambda b, p: (b, 0, 0, p)),
            pl.BlockSpec((1, GLA_DV), lambda b, p: (0, 0)),
        ],
        out_specs=val_spec,
        out_shape=jax.ShapeDtypeStruct((B, L, BRANCH), _BF16),
        scratch_shapes=[
            pltpu.VMEM((L, hs * GLA_DV), _F32),
            pltpu.VMEM((L, hs * GLA_DV), _F32),
            pltpu.VMEM((hs, GLA_DK, GLA_DV), _F32),
            pltpu.VMEM((hs, GLA_DK, GLA_DV), _F32),
        ],
        compiler_params=pltpu.CompilerParams(
            dimension_semantics=("arbitrary", "arbitrary"), vmem_limit_bytes=VMEM_LIMIT),
        name="gla_mix",
    )(qf, kf, kef, qb, kb, keb, v, dec, norm_g.reshape(1, GLA_DV))


def _na_proj_kernel(*refs, fused):
    tile_fn, refs = _layer_input_tiles(refs, fused)
    modb_ref, modc_ref, w_ref = refs[:3]
    outs = refs[3:]
    xnew_ref = outs[0] if fused else None
    q_ref, k_ref, v_ref, g_ref = outs[1 if fused else 0:]
    t = pl.program_id(1)
    for sub in range(SUBS_PER_STEP):
        rows = slice(sub * TOKEN_TILE, (sub + 1) * TOKEN_TILE)
        m = _sub_tile_mod(modb_ref, modc_ref, (t == 0) if sub == 0 else False)
        hb = _modulate(tile_fn(sub, xnew_ref), m)
        q_ref[0, rows, :] = (_dot(hb, w_ref[:, 0:BRANCH]) * (NA_DH ** -0.5 * LOG2E)).astype(_BF16)
        k_ref[0, rows, :] = _dot(hb, w_ref[:, BRANCH:2 * BRANCH]).astype(_BF16)
        v_ref[0, rows, :] = _dot(hb, w_ref[:, 2 * BRANCH:3 * BRANCH]).astype(_BF16)
        g_ref[0, rows, :] = _dot(hb, w_ref[:, 3 * BRANCH:4 * BRANCH]).astype(_BF16)


def _na_project(layer_in, merge, mod_i, w_in):
    fused = merge is not None
    B = (merge[0] if fused else layer_in[1]).shape[0]
    L = CTX_LEN + 2048
    in_specs, in_args = _merge_operands(merge) if fused else _stream_operands(*layer_in)
    spec = pl.BlockSpec((1, STEP_ROWS, BRANCH), lambda b, t: (b, t, 0))
    shape = jax.ShapeDtypeStruct((B, L, BRANCH), _BF16)
    x_specs = [pl.BlockSpec((1, STEP_ROWS, D_MODEL), lambda b, t: (b, t, 0))] if fused else []
    x_shapes = [jax.ShapeDtypeStruct((B, L, D_MODEL), _F32)] if fused else []
    outs = pl.pallas_call(
        functools.partial(_na_proj_kernel, fused=fused),
        grid=(B, L // STEP_ROWS),
        in_specs=in_specs + _mod_specs() + [_const_spec((D_MODEL, 4 * BRANCH))],
        out_specs=x_specs + [spec] * 4,
        out_shape=x_shapes + [shape] * 4,
        compiler_params=pltpu.CompilerParams(
            dimension_semantics=("arbitrary", "arbitrary"), vmem_limit_bytes=VMEM_LIMIT),
        name="na_project",
    )(*in_args, mod_i, mod_i, w_in)
    return (outs[0], outs[1:]) if fused else (None, outs)


NA_ROWS = 2048 // GRID_W
NA_PAIR_W = 2 * NA_DH
NA_BAND = NA_KH * GRID_W
NA_ROWS_PER_STEP = 4
NA_BATCH_PER_STEP = 8


def _stack_heads(q2):
    lane = lax.broadcasted_iota(jnp.int32, q2.shape, 1)
    zero = jnp.zeros_like(q2)
    return jnp.concatenate([jnp.where(lane < NA_DH, q2, zero), jnp.where(lane >= NA_DH, q2, zero)], axis=0)


def _unstack_heads(o, n):
    lane = lax.broadcasted_iota(jnp.int32, (n, NA_PAIR_W), 1)
    return jnp.where(lane < NA_DH, o[0:n], o[n:2 * n])


def _na_mix_kernel(q_ref, k_ref, v_ref, bias_ref, out_ref, sl_a, sc_a, sl_b, sc_b,
                   pl_a, pc_a, l_a, pl_b, pc_b, l_b, *, ctx_queries):
    for e in range(NA_BATCH_PER_STEP):
        if ctx_queries:
            kc = k_ref[e, 0:CTX_LEN, :]
            vc = v_ref[e, 0:CTX_LEN, :]
            qs = _stack_heads(q_ref[e, 0:CTX_LEN, :])
            s = _dot_nt(qs, kc)
            m = jnp.max(s, axis=-1, keepdims=True)
            p = jnp.exp2(s - m)
            l = jnp.sum(p, axis=-1, keepdims=True)
            o = _dot(p.astype(_BF16), vc) / l
            out_ref[e, 0:CTX_LEN, :] = _unstack_heads(o, CTX_LEN).astype(_BF16)
        else:
            out_ref[e, 0:CTX_LEN, :] = jnp.zeros((CTX_LEN, NA_PAIR_W), _BF16)

    U = NA_ROWS_PER_STEP
    M = 2 * GRID_W
    groups_per_elem = NA_ROWS // U
    n_groups = NA_BATCH_PER_STEP * groups_per_elem

    def row_slices(g, u):
        g = jnp.asarray(g, jnp.int32)
        e = lax.shift_right_logical(g, groups_per_elem.bit_length() - 1)
        r = (g & (groups_per_elem - 1)) * U + u
        r_start = jnp.clip(r - NA_KH // 2, 0, NA_ROWS - NA_KH)
        q_rows = pl.ds(pl.multiple_of(CTX_LEN + r * GRID_W, GRID_W), GRID_W)
        band = pl.ds(pl.multiple_of(CTX_LEN + r_start * GRID_W, GRID_W), NA_BAND)
        return e, q_rows, band, r - r_start

    def scores(g, sl_ref, sc_ref):
        qss = []
        for u in range(U):
            e, q_rows, band, row_class = row_slices(g, u)
            qss.append(_stack_heads(q_ref[e, q_rows, :]))
            sl_ref[u] = _dot_nt(qss[u], k_ref[e, band, :]) + bias_ref[0, row_class]
        sc_ref[...] = _dot_nt(jnp.concatenate(qss, axis=0), k_ref[e, 0:CTX_LEN, :])

    def softmax(sl_ref, sc_ref, pl_ref, pc_ref, l_ref):
        for u in range(U):
            s_loc = sl_ref[u]
            s_ctx = sc_ref[u * M:(u + 1) * M, :]
            m = jnp.maximum(jnp.max(s_loc, axis=-1, keepdims=True), jnp.max(s_ctx, axis=-1, keepdims=True))
            p_loc = jnp.exp2(s_loc - m)
            p_ctx = jnp.exp2(s_ctx - m)
            l = jnp.sum(p_loc, axis=-1, keepdims=True) + jnp.sum(p_ctx, axis=-1, keepdims=True)
            pl_ref[u] = p_loc.astype(_BF16)
            pc_ref[u * M:(u + 1) * M, :] = p_ctx.astype(_BF16)
            l_ref[u * M:(u + 1) * M, :] = jnp.broadcast_to(l, (M, LANES))

    def weighted_values(g, pl_ref, pc_ref, l_ref):
        e = row_slices(g, 0)[0]
        o_ctx_all = _dot(pc_ref[...], v_ref[e, 0:CTX_LEN, :])
        for u in range(U):
            _, q_rows, band, _ = row_slices(g, u)
            o = (_dot(pl_ref[u], v_ref[e, band, :]) + o_ctx_all[u * M:(u + 1) * M]) / l_ref[u * M:(u + 1) * M, :]
            out_ref[e, q_rows, :] = _unstack_heads(o, GRID_W).astype(_BF16)

    s_bufs = ((sl_a, sc_a), (sl_b, sc_b))
    p_bufs = ((pl_a, pc_a, l_a), (pl_b, pc_b, l_b))
    scores(0, *s_bufs[0])
    scores(1, *s_bufs[1])
    softmax(*s_bufs[0], *p_bufs[0])

    def pair_body(i, carry):
        g = 2 * i
        for par in range(2):
            scores(g + par + 2, *s_bufs[par])
            weighted_values(g + par, *p_bufs[par])
            softmax(*s_bufs[1 - par], *p_bufs[1 - par])
        return carry

    lax.fori_loop(0, n_groups // 2 - 1, pair_body, 0)
    weighted_values(n_groups - 2, *p_bufs[0])
    softmax(*s_bufs[1], *p_bufs[1])
    weighted_values(n_groups - 1, *p_bufs[1])


RPB_ROWS_PAD = 16


def _na_bias_kernel(rpb_ref, out_ref):
    c = lax.broadcasted_iota(jnp.int32, (GRID_W, LANES), 0)
    lane = lax.broadcasted_iota(jnp.int32, (GRID_W, LANES), 1)
    kc = lane & (GRID_W - 1)
    col_start = jnp.clip(c - NA_KW // 2, 0, GRID_W - NA_KW)
    valid = (kc >= col_start) & (kc < col_start + NA_KW)
    low_half = lane < GRID_W
    for a in range(2):
        toe = []
        for ro in range(2 * NA_KH - 1):
            w = jnp.broadcast_to(rpb_ref[0, a, ro:ro + 1, :], (GRID_W, LANES)) * LOG2E
            t_lo = pltpu.roll(w, LANES - (NA_KW - 1), 1, stride=1, stride_axis=0)
            t_hi = pltpu.roll(w, GRID_W - (NA_KW - 1), 1, stride=1, stride_axis=0)
            toe.append((t_lo, t_hi))
        for d in range(NA_KH):
            for jj in range(NA_KH // 2):
                ro = 2 * jj - d + NA_KH - 1
                tile = jnp.where(low_half, toe[ro][0], toe[ro + 1][1])
                out_ref[0, d, a * GRID_W:(a + 1) * GRID_W, jj * LANES:(jj + 1) * LANES] = (
                    jnp.where(valid, tile, NEG_BIG))


NA_PAIRS = NA_HEADS // 2


def _na_bias_tables(rpb):
    n_layers, _, n_ro, n_co = rpb.shape
    rpb = jnp.pad(rpb, ((0, 0), (0, 0), (0, RPB_ROWS_PAD - n_ro), (0, LANES - n_co)))
    n = n_layers * NA_PAIRS
    return pl.pallas_call(
        _na_bias_kernel,
        grid=(n,),
        in_specs=[pl.BlockSpec((1, 2, RPB_ROWS_PAD, LANES), lambda p: (p, 0, 0, 0))],
        out_specs=pl.BlockSpec((1, NA_KH, 2 * GRID_W, NA_BAND), lambda p: (p, 0, 0, 0)),
        out_shape=jax.ShapeDtypeStruct((n, NA_KH, 2 * GRID_W, NA_BAND), _F32),
        compiler_params=pltpu.CompilerParams(dimension_semantics=("arbitrary",)),
        name="na_bias_table",
    )(rpb.reshape(n, 2, RPB_ROWS_PAD, LANES))


def _na_mix(q, k, v, bias_tables, na_layer, ctx_queries):
    B, L, _ = q.shape
    n_pairs = NA_PAIRS
    U = NA_ROWS_PER_STEP
    spec = pl.BlockSpec((NA_BATCH_PER_STEP, L, NA_PAIR_W), lambda p, b: (b, 0, p))
    score_bufs = [pltpu.VMEM((U, 2 * GRID_W, NA_BAND), _F32),
                  pltpu.VMEM((U * 2 * GRID_W, CTX_LEN), _F32)]
    prob_bufs = [pltpu.VMEM((U, 2 * GRID_W, NA_BAND), _BF16),
                 pltpu.VMEM((U * 2 * GRID_W, CTX_LEN), _BF16),
                 pltpu.VMEM((U * 2 * GRID_W, LANES), _F32)]
    return pl.pallas_call(
        functools.partial(_na_mix_kernel, ctx_queries=ctx_queries),
        grid=(n_pairs, B // NA_BATCH_PER_STEP),
        in_specs=[spec, spec, spec,
                  pl.BlockSpec((1, NA_KH, 2 * GRID_W, NA_BAND),
                               lambda p, b: (na_layer * NA_PAIRS + p, 0, 0, 0))],
        out_specs=spec,
        out_shape=jax.ShapeDtypeStruct((B, L, BRANCH), _BF16),
        scratch_shapes=score_bufs + score_bufs + prob_bufs + prob_bufs,
        compiler_params=pltpu.CompilerParams(
            dimension_semantics=("arbitrary", "arbitrary"), vmem_limit_bytes=VMEM_LIMIT),
        name="na_mix",
    )(q, k, v, bias_tables)


FINAL_SUBS = 4


def _final_merge_kernel(*refs):
    n = FINAL_SUBS
    o_refs, g_refs, x_refs = refs[0:n], refs[n:2 * n], refs[2 * n:3 * n]
    mod_ref, w_ref, lng_ref, lnb_ref, out_ref = refs[3 * n:]
    gate = mod_ref[0][:, 2 * D_MODEL:3 * D_MODEL]
    for sub in range(n):
        out_ref[0, sub * TOKEN_TILE:(sub + 1) * TOKEN_TILE, :] = _residual_layer_norm(
            x_refs[sub][0], _gated_out_proj(o_refs[sub][0], g_refs[sub][0], w_ref), gate, lng_ref, lnb_ref)


def _final_merge(o, g, xs, mod_i, w_out, ln_g, ln_b):
    B, L, _ = xs.shape
    rows = FINAL_SUBS * TOKEN_TILE
    n_t = (L - CTX_LEN) // rows

    def sub_specs(width):
        return [pl.BlockSpec((1, TOKEN_TILE, width), lambda b, t, s=s: (b, FINAL_SUBS * t + s + 1, 0))
                for s in range(FINAL_SUBS)]

    return pl.pallas_call(
        _final_merge_kernel,
        grid=(B, n_t),
        in_specs=sub_specs(BRANCH) + sub_specs(BRANCH) + sub_specs(D_MODEL) + [
            _mod_specs()[0],
            _const_spec((BRANCH, D_MODEL)), _const_spec((1, D_MODEL)), _const_spec((1, D_MODEL)),
        ],
        out_specs=pl.BlockSpec((1, rows, D_MODEL), lambda b, t: (b, t, 0)),
        out_shape=jax.ShapeDtypeStruct((B, n_t * rows, D_MODEL), _F32),
        compiler_params=pltpu.CompilerParams(
            dimension_semantics=("arbitrary", "arbitrary"), vmem_limit_bytes=VMEM_LIMIT),
        name="final_merge",
    )(*([o] * FINAL_SUBS + [g] * FINAL_SUBS + [xs] * FINAL_SUBS), mod_i, w_out,
      ln_g.reshape(1, D_MODEL), ln_b.reshape(1, D_MODEL))


def _rope_tables(n_tokens):
    pos = jnp.arange(n_tokens, dtype=jnp.int32)
    row = (pos // GRID_W).astype(_F32)
    col = (pos % GRID_W).astype(_F32)
    quarter = GLA_DK // 4
    inv = ROPE_BASE ** (-jnp.arange(quarter, dtype=_F32) / quarter)
    ang = jnp.concatenate([row[:, None] * inv, col[:, None] * inv], -1)
    cos, sin = jnp.cos(ang), jnp.sin(ang)
    cos = jnp.concatenate([cos, cos], -1)
    sin = jnp.concatenate([-sin, sin], -1)
    return (jnp.concatenate([jnp.ones((CTX_LEN, GLA_DK), _F32), cos], 0),
            jnp.concatenate([jnp.zeros((CTX_LEN, GLA_DK), _F32), sin], 0))


def kernel(x, c, ctx, c_ctx, ada_w, ada_b, ln_g, ln_b, w_out, gla_w_in, gla_dec_w1, gla_dec_w2,
           gla_dec_b, gla_norm_g, na_w_in, na_rpb):
    B, S, D = x.shape
    assert (B, S, D) == (8, 2048, D_MODEL) and ctx.shape == (B, CTX_LEN, D)
    cond = jnp.zeros((COND_ROWS, D), _F32).at[0:B].set(c).at[CTX_COND_ROW].set(c_ctx)
    mod = _modulation(cond, ada_w, ada_b).reshape(DEPTH, COND_ROWS, 1, 3 * D_MODEL)
    cos_t, sin_t = _rope_tables(S)
    bias_tables = _na_bias_tables(na_rpb)

    layer_in = (ctx, x)
    merge = None
    for i in range(DEPTH):
        j = i // N_MIXERS
        if i % N_MIXERS == 0:
            w1 = jnp.concatenate([gla_dec_w1[j, 0], gla_dec_w1[j, 1]], axis=1)
            w1 = jnp.pad(w1, ((0, 0), (0, LOW_PAD - 2 * GLA_GATE_RANK)))
            w_cat = jnp.concatenate([gla_w_in[j], w1], axis=1).astype(_BF16)
            w2_cat = jnp.zeros((LOW_PAD, 2 * GLA_KEY_W), _F32)
            w2_cat = w2_cat.at[0:GLA_GATE_RANK, 0:GLA_KEY_W].set(gla_dec_w2[j, 0])
            w2_cat = w2_cat.at[GLA_GATE_RANK:2 * GLA_GATE_RANK, GLA_KEY_W:].set(gla_dec_w2[j, 1])
            dec_b = gla_dec_b[j].reshape(1, 2 * GLA_KEY_W) * LOG2E
            x_new, (qf, kf, kef, qb, kb, keb, v, g, dec) = _gla_project(
                layer_in, merge, mod[i], w_cat, (w2_cat * LOG2E).astype(_BF16), dec_b, cos_t, sin_t)
            o = _gla_mix(qf, kf, kef, qb, kb, keb, v, dec, gla_norm_g[j])
        else:
            x_new, (q, k, v, g) = _na_project(layer_in, merge, mod[i], na_w_in[j].astype(_BF16))
            o = _na_mix(q, k, v, bias_tables, j, ctx_queries=i < DEPTH - 1)
        if x_new is not None:
            layer_in = (None, x_new)
        merge = (o, g) + layer_in + (mod[i], w_out[i].astype(_BF16), ln_g[i], ln_b[i])
    o, g, _, xs, mod_last, w_o, lg, lb = merge
    return _final_merge(o, g, xs, mod_last, w_o, lg, lb)
```

```python
import functools

import numpy as np
import jax
import jax.numpy as jnp
from jax import lax
from jax.experimental import pallas as pl
from jax.experimental.pallas import tpu as pltpu

D_MODEL = 1024
DEPTH = 4
GRID_W = 64
CTX_LEN = 256
N_MIXERS = 2
BRANCH = D_MODEL
GLA_HEADS = 4
GLA_DK = 128
GLA_DV = 256
GLA_KEY_W = GLA_HEADS * GLA_DK
GLA_GATE_RANK = 16
GLA_TAU = 16.0
GLA_CHUNK = 64
NA_HEADS = 16
NA_DH = 64
NA_KH = 8
NA_KW = 16
ROPE_BASE = 10000.0
LN_EPS = 1e-5
NORM_EPS = 1e-6
ALPHA = (2 * DEPTH) ** 0.25

LANES = 128
TOKEN_TILE = 256
SUBS_PER_STEP = 3
STEP_ROWS = SUBS_PER_STEP * TOKEN_TILE
CHUNKS_PER_TILE = TOKEN_TILE // GLA_CHUNK
COND_ROWS = 16
CTX_COND_ROW = 8
LOW_PAD = 128
NEG_BIG = -1e30
LOG2E = 1.4426950408889634
VMEM_LIMIT = 58 * 1024 * 1024

_F32 = jnp.float32
_BF16 = jnp.bfloat16


def _dot(a, b):
    return jnp.dot(a, b, preferred_element_type=_F32)


def _dot_nt(a, b):
    return lax.dot_general(a, b, (((1,), (1,)), ((), ())), preferred_element_type=_F32)


def _dot_tn(a, b):
    return lax.dot_general(a, b, (((0,), (0,)), ((), ())), preferred_element_type=_F32)


def _silu(x):
    return x * (1.0 / (1.0 + jnp.exp(-x)))


def _const_spec(shape):
    return pl.BlockSpec(shape, lambda b, t: (0,) * len(shape), pipeline_mode=pl.Buffered(1))


def _mod_kernel(cond_ref, w_ref, b_ref, out_ref):
    s = _silu(cond_ref[...]).astype(_BF16)
    out_ref[0] = _dot(s, w_ref[0].astype(_BF16)) + b_ref[0]


def _modulation(cond, ada_w, ada_b):
    n_col = 1
    width = 3 * D_MODEL // n_col
    return pl.pallas_call(
        _mod_kernel,
        grid=(DEPTH, n_col),
        in_specs=[
            pl.BlockSpec((COND_ROWS, D_MODEL), lambda i, j: (0, 0)),
            pl.BlockSpec((1, D_MODEL, width), lambda i, j: (i, 0, j)),
            pl.BlockSpec((1, 1, width), lambda i, j: (i, 0, j)),
        ],
        out_specs=pl.BlockSpec((1, COND_ROWS, width), lambda i, j: (i, 0, j)),
        out_shape=jax.ShapeDtypeStruct((DEPTH, COND_ROWS, 3 * D_MODEL), _F32),
        compiler_params=pltpu.CompilerParams(
            dimension_semantics=("arbitrary", "arbitrary"), vmem_limit_bytes=VMEM_LIMIT),
        name="modulation",
    )(cond, ada_w, ada_b.reshape(DEPTH, 1, 3 * D_MODEL))


def _mod_specs():
    return [pl.BlockSpec((1, 1, 3 * D_MODEL), lambda b, t: (b, 0, 0)),
            pl.BlockSpec((1, 1, 3 * D_MODEL), lambda b, t: (CTX_COND_ROW, 0, 0))]


def _sub_tile_mod(modb_ref, modc_ref, is_ctx):
    if is_ctx is False:
        return modb_ref[0]
    return jnp.where(is_ctx, modc_ref[0], modb_ref[0])


def _modulate(x, m):
    return (x * (1.0 + m[:, D_MODEL:2 * D_MODEL]) + m[:, 0:D_MODEL]).astype(_BF16)


N_STREAM_REFS = 1 + SUBS_PER_STEP
N_MERGE_REFS = 2 + N_STREAM_REFS + 5


def _stream_operands(ctx, lat):
    if ctx is None:
        ctx, shift = lat, 0
    else:
        shift = 1
    specs = [pl.BlockSpec((1, TOKEN_TILE, D_MODEL), lambda b, t: (b, 0, 0))]
    for s in range(SUBS_PER_STEP):
        specs.append(pl.BlockSpec(
            (1, TOKEN_TILE, D_MODEL),
            lambda b, t, s=s: (b, jnp.maximum(SUBS_PER_STEP * t + s - shift, 0), 0)))
    return specs, [ctx] + [lat] * SUBS_PER_STEP


def _stream_sub_tile(stream_refs, sub, t):
    x = stream_refs[1 + sub][0]
    if sub == 0:
        x = jnp.where(t == 0, stream_refs[0][0], x)
    return x


def _gated_out_proj(o, g, w_ref):
    return _dot(o * (g * (1.0 / (1.0 + jnp.exp(-g)))), w_ref[...])


def _residual_layer_norm(x, y, gate, lng_ref, lnb_ref):
    z = x + (gate * (1.0 / ALPHA)) * y
    mu = jnp.mean(z, axis=-1, keepdims=True)
    zc = z - mu
    var = jnp.mean(zc * zc, axis=-1, keepdims=True)
    return zc * lax.rsqrt(var + LN_EPS / ALPHA ** 2) * lng_ref[...] + lnb_ref[...]


def _layer_input_tiles(refs, fused):
    t = pl.program_id(1)
    if not fused:
        stream, rest = refs[:N_STREAM_REFS], refs[N_STREAM_REFS:]
        return (lambda sub, xnew_ref: _stream_sub_tile(stream, sub, t)), rest
    o_ref, g_ref = refs[0], refs[1]
    stream = refs[2:2 + N_STREAM_REFS]
    pmodb_ref, pmodc_ref, wout_ref, lng_ref, lnb_ref = refs[2 + N_STREAM_REFS:N_MERGE_REFS]
    ys = [_gated_out_proj(o_ref[0, sub * TOKEN_TILE:(sub + 1) * TOKEN_TILE, :],
                          g_ref[0, sub * TOKEN_TILE:(sub + 1) * TOKEN_TILE, :], wout_ref)
          for sub in range(SUBS_PER_STEP)]

    def tile(sub, xnew_ref):
        rows = slice(sub * TOKEN_TILE, (sub + 1) * TOKEN_TILE)
        m = _sub_tile_mod(pmodb_ref, pmodc_ref, (t == 0) if sub == 0 else False)
        x_new = _residual_layer_norm(_stream_sub_tile(stream, sub, t), ys[sub],
                                     m[:, 2 * D_MODEL:3 * D_MODEL], lng_ref, lnb_ref)
        xnew_ref[0, rows, :] = x_new
        return x_new

    return tile, refs[N_MERGE_REFS:]


def _merge_operands(merge):
    if merge is None:
        return [], []
    o, g, ctx, lat, mod_prev, w_out, ln_g, ln_b = merge
    step = pl.BlockSpec((1, STEP_ROWS, BRANCH), lambda b, t: (b, t, 0))
    s_specs, s_args = _stream_operands(ctx, lat)
    specs = [step, step] + s_specs + _mod_specs() + [
        _const_spec((BRANCH, D_MODEL)), _const_spec((1, D_MODEL)), _const_spec((1, D_MODEL))]
    args = [o, g] + s_args + [mod_prev, mod_prev, w_out, ln_g.reshape(1, D_MODEL), ln_b.reshape(1, D_MODEL)]
    return specs, args


def _gla_proj_kernel(*refs, fused):
    tile_fn, refs = _layer_input_tiles(refs, fused)
    modb_ref, modc_ref, w_ref, w2_ref, decb_ref, cos_ref, sin_ref, tri_ref = refs[:8]
    outs = refs[8:]
    xnew_ref = outs[0] if fused else None
    qf_ref, kf_ref, kef_ref, qb_ref, kb_ref, keb_ref, v_ref, g_ref, dec_ref = outs[1 if fused else 0:]
    t = pl.program_id(1)
    kw = GLA_KEY_W
    tiles = []
    for sub in range(SUBS_PER_STEP):
        rows = slice(sub * TOKEN_TILE, (sub + 1) * TOKEN_TILE)
        m = _sub_tile_mod(modb_ref, modc_ref, (t == 0) if sub == 0 else False)
        hb = _modulate(tile_fn(sub, xnew_ref), m)
        q = _rope(_dot(hb, w_ref[:, 0:kw]) * (GLA_DK ** -0.5), cos_ref[rows, :], sin_ref[rows, :])
        low = _dot(hb, w_ref[:, 2 * kw + 2 * BRANCH:2 * kw + 2 * BRANCH + LOW_PAD])
        y = _dot(low.astype(_BF16), w2_ref[...]) + decb_ref[...]
        la = (jnp.minimum(y, 0.0) - jnp.log2(1.0 + jnp.exp2(-jnp.abs(y)))) * (1.0 / GLA_TAU)
        tiles.append((rows, hb, q, la))
    for sub, (rows, hb, q, la) in enumerate(tiles):
        pre = _dot(tri_ref[...], la.astype(_BF16))
        k = _rope(_dot(hb, w_ref[:, kw:2 * kw]), cos_ref[rows, :], sin_ref[rows, :])
        v_ref[0, rows, :] = _dot(hb, w_ref[:, 2 * kw:2 * kw + BRANCH]).astype(_BF16)
        g_ref[0, rows, :] = _dot(hb, w_ref[:, 2 * kw + BRANCH:2 * kw + 2 * BRANCH]).astype(_BF16)
        key_outs = [r.at[0, rows, :] for r in (qf_ref, kf_ref, kef_ref, qb_ref, kb_ref, keb_ref)]
        _gla_decay_factors(q, k, la, pre, *key_outs, dec_ref.at[0, sub])


def _rope(x, cos, sin):
    heads = []
    for h in range(x.shape[1] // GLA_DK):
        xh = x[:, h * GLA_DK:(h + 1) * GLA_DK]
        heads.append(xh * cos + pltpu.roll(xh, GLA_DK // 2, 1) * sin)
    return jnp.concatenate(heads, axis=1)


def _gla_decay_factors(q, k, la, pre, qf_ref, kf_ref, kef_ref, qb_ref, kb_ref, keb_ref, dec_ref):
    kw = GLA_KEY_W
    tot4 = jnp.concatenate([pre[(c + 1) * GLA_CHUNK - 1:(c + 1) * GLA_CHUNK, :] for c in range(CHUNKS_PER_TILE)],
                           axis=0)

    def chunk_rows(t4):
        return jnp.broadcast_to(t4[:, None, :], (CHUNKS_PER_TILE, GLA_CHUNK, GLA_DK)).reshape(TOKEN_TILE, GLA_DK)

    for h in range(GLA_HEADS):
        sl = slice(h * GLA_DK, (h + 1) * GLA_DK)
        sl_b = slice(kw + h * GLA_DK, kw + (h + 1) * GLA_DK)
        pre_f, tot_f = pre[:, sl], chunk_rows(tot4[:, sl])
        pre_b, tot_b = pre[:, sl_b], chunk_rows(tot4[:, sl_b])
        ante_b = pre_b - la[:, sl_b]
        qh = q[:, sl]
        kh = k[:, sl]
        qf_ref[:, sl] = (qh * jnp.exp2(pre_f)).astype(_BF16)
        kf_ref[:, sl] = (kh * jnp.exp2(-pre_f)).astype(_BF16)
        kef_ref[:, sl] = (kh * jnp.exp2(tot_f - pre_f)).astype(_BF16)
        qb_ref[:, sl] = (qh * jnp.exp2(tot_b - pre_b)).astype(_BF16)
        kb_ref[:, sl] = (kh * jnp.exp2(ante_b - tot_b)).astype(_BF16)
        keb_ref[:, sl] = (kh * jnp.exp2(ante_b)).astype(_BF16)

    dec_ref[...] = jnp.exp2(jnp.concatenate([tot4[:, 0:kw], tot4[:, kw:2 * kw]], axis=0))


def _chunk_prefix_matrix():
    i = np.arange(TOKEN_TILE)
    same = (i[:, None] // GLA_CHUNK) == (i[None, :] // GLA_CHUNK)
    return jnp.asarray(same & (i[None, :] <= i[:, None]), _BF16)


def _gla_project(layer_in, merge, mod_i, w_cat, w2_cat, dec_b, cos_t, sin_t):
    fused = merge is not None
    B = (merge[0] if fused else layer_in[1]).shape[0]
    L = CTX_LEN + 2048
    n_w = w_cat.shape[1]
    tri = _chunk_prefix_matrix()
    in_specs, in_args = _merge_operands(merge) if fused else _stream_operands(*layer_in)
    key_spec = pl.BlockSpec((1, STEP_ROWS, GLA_KEY_W), lambda b, t: (b, t, 0))
    val_spec = pl.BlockSpec((1, STEP_ROWS, BRANCH), lambda b, t: (b, t, 0))
    key_shape = jax.ShapeDtypeStruct((B, L, GLA_KEY_W), _BF16)
    val_shape = jax.ShapeDtypeStruct((B, L, BRANCH), _BF16)
    rope_spec = pl.BlockSpec((STEP_ROWS, GLA_DK), lambda b, t: (t, 0))
    x_specs = [pl.BlockSpec((1, STEP_ROWS, D_MODEL), lambda b, t: (b, t, 0))] if fused else []
    x_shapes = [jax.ShapeDtypeStruct((B, L, D_MODEL), _F32)] if fused else []
    outs = pl.pallas_call(
        functools.partial(_gla_proj_kernel, fused=fused),
        grid=(B, L // STEP_ROWS),
        in_specs=in_specs + _mod_specs() + [
            _const_spec((D_MODEL, n_w)),
            _const_spec((LOW_PAD, 2 * GLA_KEY_W)),
            _const_spec((1, 2 * GLA_KEY_W)),
            rope_spec, rope_spec,
            _const_spec((TOKEN_TILE, TOKEN_TILE)),
        ],
        out_specs=x_specs + [key_spec] * 6 + [val_spec] * 2 + [
            pl.BlockSpec((1, SUBS_PER_STEP, 2 * CHUNKS_PER_TILE, GLA_KEY_W), lambda b, t: (b, t, 0, 0))],
        out_shape=x_shapes + [key_shape] * 6 + [val_shape] * 2 + [
            jax.ShapeDtypeStruct((B, L // TOKEN_TILE, 2 * CHUNKS_PER_TILE, GLA_KEY_W), _F32)],
        compiler_params=pltpu.CompilerParams(
            dimension_semantics=("arbitrary", "arbitrary"), vmem_limit_bytes=VMEM_LIMIT),
        name="gla_project",
    )(*in_args, mod_i, mod_i, w_cat, w2_cat, dec_b, cos_t, sin_t, tri)
    return (outs[0], outs[1:]) if fused else (None, outs)


GLA_HEADS_PER_STEP = 2
GLA_CHUNKS_PER_STEP = 6


def _gla_mix_kernel(qf_ref, kf_ref, kef_ref, qb_ref, kb_ref, keb_ref, v_ref, dec_ref, ng_ref,
                    out_ref, accf_ref, accb_ref, sf_ref, sb_ref):
    n_chunks = accf_ref.shape[0] // GLA_CHUNK
    n_ctx = CTX_LEN // GLA_CHUNK
    C = GLA_CHUNK
    sf_ref[...] = jnp.zeros_like(sf_ref)
    sb_ref[...] = jnp.zeros_like(sb_ref)
    ti = lax.broadcasted_iota(jnp.int32, (C, C), 0)
    si = lax.broadcasted_iota(jnp.int32, (C, C), 1)
    mask_f = si <= ti
    mask_b = si > ti
    sub8 = lax.broadcasted_iota(jnp.int32, (2 * CHUNKS_PER_TILE, GLA_DK), 0)

    fwd = (qf_ref, kf_ref, kef_ref, sf_ref, accf_ref, mask_f, 0)
    bwd = (qb_ref, kb_ref, keb_ref, sb_ref, accb_ref, mask_b, CHUNKS_PER_TILE)

    def body(i, carry):
        chains = []
        for u in range(GLA_CHUNKS_PER_STEP):
            cf = i * GLA_CHUNKS_PER_STEP + u
            cb = jnp.where(cf < n_ctx, n_ctx - 1 - cf, n_chunks + n_ctx - 1 - cf)
            for c, (q_ref, k_ref, ke_ref, s_ref, acc_ref, mask, dec_row) in ((cf, fwd), (cb, bwd)):
                for h in range(GLA_HEADS_PER_STEP):
                    chains.append(dict(u=u, c=c, h=h, q_ref=q_ref, k_ref=k_ref, ke_ref=ke_ref, s_ref=s_ref,
                                       acc_ref=acc_ref, mask=mask, dec_row=dec_row))
        for ch in chains:
            c, h = ch["c"], ch["h"]
            ch["rows"] = pl.ds(pl.multiple_of(c * C, C), C)
            ch["ksl"] = slice(h * GLA_DK, (h + 1) * GLA_DK)
            ch["vsl"] = slice(h * GLA_DV, (h + 1) * GLA_DV)
            ch["q"] = ch["q_ref"][0, ch["rows"], ch["ksl"]]
            ch["v"] = v_ref[0, ch["rows"], ch["vsl"]]
            ch["att"] = _dot_nt(ch["q"], ch["k_ref"][0, ch["rows"], ch["ksl"]])
        for ch in chains:
            ch["kv"] = _dot_tn(ch["ke_ref"][0, ch["rows"], ch["ksl"]], ch["v"])
            tile = lax.shift_right_logical(ch["c"], CHUNKS_PER_TILE.bit_length() - 1)
            dec8 = dec_ref[0, tile, :, ch["ksl"]]
            pick = sub8 == ch["dec_row"] + (ch["c"] & (CHUNKS_PER_TILE - 1))
            dec = jnp.sum(jnp.where(pick, dec8, 0.0), axis=0, keepdims=True)
            dcol = jnp.broadcast_to(dec, (GLA_DK, GLA_DK)).T
            ch["decay"] = jnp.concatenate([dcol, dcol], axis=1)
        state = {}
        for u in range(GLA_CHUNKS_PER_STEP):
            for ch in chains:
                if ch["u"] != u:
                    continue
                key = (id(ch["s_ref"]), ch["h"])
                s = state[key] if u else ch["s_ref"][ch["h"]]
                att = jnp.where(ch["mask"], ch["att"], 0.0).astype(_BF16)
                lhs = jnp.concatenate([ch["q"], att], axis=1)
                rhs = jnp.concatenate([s.astype(_BF16), ch["v"]], axis=0)
                ch["acc_ref"][ch["rows"], ch["vsl"]] = _dot(lhs, rhs)
                state[key] = s * ch["decay"] + ch["kv"]
        for ch in chains:
            if ch["u"] == GLA_CHUNKS_PER_STEP - 1:
                ch["s_ref"][ch["h"]] = state[(id(ch["s_ref"]), ch["h"])]
        return carry

    lax.fori_loop(0, n_chunks // GLA_CHUNKS_PER_STEP, body, 0)

    def norm_body(j, carry):
        rows = pl.ds(pl.multiple_of(j * TOKEN_TILE, TOKEN_TILE), TOKEN_TILE)
        for h in range(GLA_HEADS_PER_STEP):
            vsl = slice(h * GLA_DV, (h + 1) * GLA_DV)
            o = accf_ref[rows, vsl] + accb_ref[rows, vsl]
            ms = jnp.mean(o * o, axis=-1, keepdims=True)
            out_ref[0, rows, vsl] = (o * lax.rsqrt(ms + NORM_EPS) * ng_ref[...]).astype(_BF16)
        return carry

    lax.fori_loop(0, accf_ref.shape[0] // TOKEN_TILE, norm_body, 0)


def _gla_mix(qf, kf, kef, qb, kb, keb, v, dec, norm_g):
    B, L, _ = qf.shape
    hs = GLA_HEADS_PER_STEP
    n_t = dec.shape[1]
    key_spec = pl.BlockSpec((1, L, hs * GLA_DK), lambda b, p: (b, 0, p))
    val_spec = pl.BlockSpec((1, L, hs * GLA_DV), lambda b, p: (b, 0, p))
    return pl.pallas_call(
        _gla_mix_kernel,
        grid=(B, GLA_HEADS // hs),
        in_specs=[key_spec] * 6 + [
            val_spec,
            pl.BlockSpec((1, n_t, 2 * CHUNKS_PER_TILE, hs * GLA_DK), lambda b, p: (b, 0, 0, p)),
            pl.BlockSpec((1, GLA_DV), lambda b, p: (0, 0)),
        ],
        out_specs=val_spec,
        out_shape=jax.ShapeDtypeStruct((B, L, BRANCH), _BF16),
        scratch_shapes=[
            pltpu.VMEM((L, hs * GLA_DV), _F32),
            pltpu.VMEM((L, hs * GLA_DV), _F32),
            pltpu.VMEM((hs, GLA_DK, GLA_DV), _F32),
            pltpu.VMEM((hs, GLA_DK, GLA_DV), _F32),
        ],
        compiler_params=pltpu.CompilerParams(
            dimension_semantics=("arbitrary", "arbitrary"), vmem_limit_bytes=VMEM_LIMIT),
        name="gla_mix",
    )(qf, kf, kef, qb, kb, keb, v, dec, norm_g.reshape(1, GLA_DV))


def _na_proj_kernel(*refs, fused):
    tile_fn, refs = _layer_input_tiles(refs, fused)
    modb_ref, modc_ref, w_ref = refs[:3]
    outs = refs[3:]
    xnew_ref = outs[0] if fused else None
    q_ref, k_ref, v_ref, g_ref = outs[1 if fused else 0:]
    t = pl.program_id(1)
    for sub in range(SUBS_PER_STEP):
        rows = slice(sub * TOKEN_TILE, (sub + 1) * TOKEN_TILE)
        m = _sub_tile_mod(modb_ref, modc_ref, (t == 0) if sub == 0 else False)
        hb = _modulate(tile_fn(sub, xnew_ref), m)
        q_ref[0, rows, :] = (_dot(hb, w_ref[:, 0:BRANCH]) * (NA_DH ** -0.5 * LOG2E)).astype(_BF16)
        k_ref[0, rows, :] = _dot(hb, w_ref[:, BRANCH:2 * BRANCH]).astype(_BF16)
        v_ref[0, rows, :] = _dot(hb, w_ref[:, 2 * BRANCH:3 * BRANCH]).astype(_BF16)
        g_ref[0, rows, :] = _dot(hb, w_ref[:, 3 * BRANCH:4 * BRANCH]).astype(_BF16)


def _na_project(layer_in, merge, mod_i, w_in):
    fused = merge is not None
    B = (merge[0] if fused else layer_in[1]).shape[0]
    L = CTX_LEN + 2048
    in_specs, in_args = _merge_operands(merge) if fused else _stream_operands(*layer_in)
    spec = pl.BlockSpec((1, STEP_ROWS, BRANCH), lambda b, t: (b, t, 0))
    shape = jax.ShapeDtypeStruct((B, L, BRANCH), _BF16)
    x_specs = [pl.BlockSpec((1, STEP_ROWS, D_MODEL), lambda b, t: (b, t, 0))] if fused else []
    x_shapes = [jax.ShapeDtypeStruct((B, L, D_MODEL), _F32)] if fused else []
    outs = pl.pallas_call(
        functools.partial(_na_proj_kernel, fused=fused),
        grid=(B, L // STEP_ROWS),
        in_specs=in_specs + _mod_specs() + [_const_spec((D_MODEL, 4 * BRANCH))],
        out_specs=x_specs + [spec] * 4,
        out_shape=x_shapes + [shape] * 4,
        compiler_params=pltpu.CompilerParams(
            dimension_semantics=("arbitrary", "arbitrary"), vmem_limit_bytes=VMEM_LIMIT),
        name="na_project",
    )(*in_args, mod_i, mod_i, w_in)
    return (outs[0], outs[1:]) if fused else (None, outs)


NA_ROWS = 2048 // GRID_W
NA_PAIR_W = 2 * NA_DH
NA_BAND = NA_KH * GRID_W
NA_ROWS_PER_STEP = 4
NA_BATCH_PER_STEP = 8


def _stack_heads(q2):
    lane = lax.broadcasted_iota(jnp.int32, q2.shape, 1)
    zero = jnp.zeros_like(q2)
    return jnp.concatenate([jnp.where(lane < NA_DH, q2, zero), jnp.where(lane >= NA_DH, q2, zero)], axis=0)


def _unstack_heads(o, n):
    lane = lax.broadcasted_iota(jnp.int32, (n, NA_PAIR_W), 1)
    return jnp.where(lane < NA_DH, o[0:n], o[n:2 * n])


def _na_mix_kernel(q_ref, k_ref, v_ref, bias_ref, out_ref, sl_a, sc_a, sl_b, sc_b, *, ctx_queries):
    for e in range(NA_BATCH_PER_STEP):
        if ctx_queries:
            kc = k_ref[e, 0:CTX_LEN, :]
            vc = v_ref[e, 0:CTX_LEN, :]
            qs = _stack_heads(q_ref[e, 0:CTX_LEN, :])
            s = _dot_nt(qs, kc)
            m = jnp.max(s, axis=-1, keepdims=True)
            p = jnp.exp2(s - m)
            l = jnp.sum(p, axis=-1, keepdims=True)
            o = _dot(p.astype(_BF16), vc) / l
            out_ref[e, 0:CTX_LEN, :] = _unstack_heads(o, CTX_LEN).astype(_BF16)
        else:
            out_ref[e, 0:CTX_LEN, :] = jnp.zeros((CTX_LEN, NA_PAIR_W), _BF16)

    U = NA_ROWS_PER_STEP
    M = 2 * GRID_W
    groups_per_elem = NA_ROWS // U
    n_groups = NA_BATCH_PER_STEP * groups_per_elem

    def row_slices(g, u):
        g = jnp.asarray(g, jnp.int32)
        e = lax.shift_right_logical(g, groups_per_elem.bit_length() - 1)
        r = (g & (groups_per_elem - 1)) * U + u
        r_start = jnp.clip(r - NA_KH // 2, 0, NA_ROWS - NA_KH)
        q_rows = pl.ds(pl.multiple_of(CTX_LEN + r * GRID_W, GRID_W), GRID_W)
        band = pl.ds(pl.multiple_of(CTX_LEN + r_start * GRID_W, GRID_W), NA_BAND)
        return e, q_rows, band, r - r_start

    def scores(g, sl_ref, sc_ref):
        qss = []
        for u in range(U):
            e, q_rows, band, row_class = row_slices(g, u)
            qss.append(_stack_heads(q_ref[e, q_rows, :]))
            sl_ref[u] = _dot_nt(qss[u], k_ref[e, band, :]) + bias_ref[0, row_class]
        sc_ref[...] = _dot_nt(jnp.concatenate(qss, axis=0), k_ref[e, 0:CTX_LEN, :])

    def finish(g, sl_ref, sc_ref):
        p_locs, p_ctxs, ls = [], [], []
        for u in range(U):
            s_loc = sl_ref[u]
            s_ctx = sc_ref[u * M:(u + 1) * M, :]
            m = jnp.maximum(jnp.max(s_loc, axis=-1, keepdims=True), jnp.max(s_ctx, axis=-1, keepdims=True))
            p_loc = jnp.exp2(s_loc - m)
            p_ctx = jnp.exp2(s_ctx - m)
            ls.append(jnp.sum(p_loc, axis=-1, keepdims=True) + jnp.sum(p_ctx, axis=-1, keepdims=True))
            p_locs.append(p_loc.astype(_BF16))
            p_ctxs.append(p_ctx.astype(_BF16))
        e = row_slices(g, 0)[0]
        o_ctx_all = _dot(jnp.concatenate(p_ctxs, axis=0), v_ref[e, 0:CTX_LEN, :])
        for u in range(U):
            _, q_rows, band, _ = row_slices(g, u)
            o = (_dot(p_locs[u], v_ref[e, band, :]) + o_ctx_all[u * M:(u + 1) * M]) / ls[u]
            out_ref[e, q_rows, :] = _unstack_heads(o, GRID_W).astype(_BF16)

    buf_a = (sl_a, sc_a)
    buf_b = (sl_b, sc_b)
    scores(0, *buf_a)

    def pair_body(i, carry):
        g = 2 * i
        scores(g + 1, *buf_b)
        finish(g, *buf_a)
        scores(g + 2, *buf_a)
        finish(g + 1, *buf_b)
        return carry

    lax.fori_loop(0, n_groups // 2 - 1, pair_body, 0)
    scores(n_groups - 1, *buf_b)
    finish(n_groups - 2, *buf_a)
    finish(n_groups - 1, *buf_b)


RPB_ROWS_PAD = 16


def _na_bias_kernel(rpb_ref, out_ref):
    c = lax.broadcasted_iota(jnp.int32, (GRID_W, LANES), 0)
    lane = lax.broadcasted_iota(jnp.int32, (GRID_W, LANES), 1)
    kc = lane & (GRID_W - 1)
    col_start = jnp.clip(c - NA_KW // 2, 0, GRID_W - NA_KW)
    valid = (kc >= col_start) & (kc < col_start + NA_KW)
    low_half = lane < GRID_W
    for a in range(2):
        toe = []
        for ro in range(2 * NA_KH - 1):
            w = jnp.broadcast_to(rpb_ref[0, a, ro:ro + 1, :], (GRID_W, LANES)) * LOG2E
            t_lo = pltpu.roll(w, LANES - (NA_KW - 1), 1, stride=1, stride_axis=0)
            t_hi = pltpu.roll(w, GRID_W - (NA_KW - 1), 1, stride=1, stride_axis=0)
            toe.append((t_lo, t_hi))
        for d in range(NA_KH):
            for jj in range(NA_KH // 2):
                ro = 2 * jj - d + NA_KH - 1
                tile = jnp.where(low_half, toe[ro][0], toe[ro + 1][1])
                out_ref[0, d, a * GRID_W:(a + 1) * GRID_W, jj * LANES:(jj + 1) * LANES] = (
                    jnp.where(valid, tile, NEG_BIG))


NA_PAIRS = NA_HEADS // 2


def _na_bias_tables(rpb):
    n_layers, _, n_ro, n_co = rpb.shape
    rpb = jnp.pad(rpb, ((0, 0), (0, 0), (0, RPB_ROWS_PAD - n_ro), (0, LANES - n_co)))
    n = n_layers * NA_PAIRS
    return pl.pallas_call(
        _na_bias_kernel,
        grid=(n,),
        in_specs=[pl.BlockSpec((1, 2, RPB_ROWS_PAD, LANES), lambda p: (p, 0, 0, 0))],
        out_specs=pl.BlockSpec((1, NA_KH, 2 * GRID_W, NA_BAND), lambda p: (p, 0, 0, 0)),
        out_shape=jax.ShapeDtypeStruct((n, NA_KH, 2 * GRID_W, NA_BAND), _F32),
        compiler_params=pltpu.CompilerParams(dimension_semantics=("arbitrary",)),
        name="na_bias_table",
    )(rpb.reshape(n, 2, RPB_ROWS_PAD, LANES))


def _na_mix(q, k, v, bias_tables, na_layer, ctx_queries):
    B, L, _ = q.shape
    n_pairs = NA_PAIRS
    U = NA_ROWS_PER_STEP
    spec = pl.BlockSpec((NA_BATCH_PER_STEP, L, NA_PAIR_W), lambda p, b: (b, 0, p))
    score_bufs = [pltpu.VMEM((U, 2 * GRID_W, NA_BAND), _F32),
                  pltpu.VMEM((U * 2 * GRID_W, CTX_LEN), _F32)]
    return pl.pallas_call(
        functools.partial(_na_mix_kernel, ctx_queries=ctx_queries),
        grid=(n_pairs, B // NA_BATCH_PER_STEP),
        in_specs=[spec, spec, spec,
                  pl.BlockSpec((1, NA_KH, 2 * GRID_W, NA_BAND),
                               lambda p, b: (na_layer * NA_PAIRS + p, 0, 0, 0))],
        out_specs=spec,
        out_shape=jax.ShapeDtypeStruct((B, L, BRANCH), _BF16),
        scratch_shapes=score_bufs + score_bufs,
        compiler_params=pltpu.CompilerParams(
            dimension_semantics=("arbitrary", "arbitrary"), vmem_limit_bytes=VMEM_LIMIT),
        name="na_mix",
    )(q, k, v, bias_tables)


FINAL_SUBS = 4


def _final_merge_kernel(*refs):
    n = FINAL_SUBS
    o_refs, g_refs, x_refs = refs[0:n], refs[n:2 * n], refs[2 * n:3 * n]
    mod_ref, w_ref, lng_ref, lnb_ref, out_ref = refs[3 * n:]
    gate = mod_ref[0][:, 2 * D_MODEL:3 * D_MODEL]
    for sub in range(n):
        out_ref[0, sub * TOKEN_TILE:(sub + 1) * TOKEN_TILE, :] = _residual_layer_norm(
            x_refs[sub][0], _gated_out_proj(o_refs[sub][0], g_refs[sub][0], w_ref), gate, lng_ref, lnb_ref)


def _final_merge(o, g, xs, mod_i, w_out, ln_g, ln_b):
    B, L, _ = xs.shape
    rows = FINAL_SUBS * TOKEN_TILE
    n_t = (L - CTX_LEN) // rows

    def sub_specs(width):
        return [pl.BlockSpec((1, TOKEN_TILE, width), lambda b, t, s=s: (b, FINAL_SUBS * t + s + 1, 0))
                for s in range(FINAL_SUBS)]

    return pl.pallas_call(
        _final_merge_kernel,
        grid=(B, n_t),
        in_specs=sub_specs(BRANCH) + sub_specs(BRANCH) + sub_specs(D_MODEL) + [
            _mod_specs()[0],
            _const_spec((BRANCH, D_MODEL)), _const_spec((1, D_MODEL)), _const_spec((1, D_MODEL)),
        ],
        out_specs=pl.BlockSpec((1, rows, D_MODEL), lambda b, t: (b, t, 0)),
        out_shape=jax.ShapeDtypeStruct((B, n_t * rows, D_MODEL), _F32),
        compiler_params=pltpu.CompilerParams(
            dimension_semantics=("arbitrary", "arbitrary"), vmem_limit_bytes=VMEM_LIMIT),
        name="final_merge",
    )(*([o] * FINAL_SUBS + [g] * FINAL_SUBS + [xs] * FINAL_SUBS), mod_i, w_out,
      ln_g.reshape(1, D_MODEL), ln_b.reshape(1, D_MODEL))


def _rope_tables(n_tokens):
    pos = jnp.arange(n_tokens, dtype=jnp.int32)
    row = (pos // GRID_W).astype(_F32)
    col = (pos % GRID_W).astype(_F32)
    quarter = GLA_DK // 4
    inv = ROPE_BASE ** (-jnp.arange(quarter, dtype=_F32) / quarter)
    ang = jnp.concatenate([row[:, None] * inv, col[:, None] * inv], -1)
    cos, sin = jnp.cos(ang), jnp.sin(ang)
    cos = jnp.concatenate([cos, cos], -1)
    sin = jnp.concatenate([-sin, sin], -1)
    return (jnp.concatenate([jnp.ones((CTX_LEN, GLA_DK), _F32), cos], 0),
            jnp.concatenate([jnp.zeros((CTX_LEN, GLA_DK), _F32), sin], 0))


def kernel(x, c, ctx, c_ctx, ada_w, ada_b, ln_g, ln_b, w_out, gla_w_in, gla_dec_w1, gla_dec_w2,
           gla_dec_b, gla_norm_g, na_w_in, na_rpb):
    B, S, D = x.shape
    assert (B, S, D) == (8, 2048, D_MODEL) and ctx.shape == (B, CTX_LEN, D)
    cond = jnp.zeros((COND_ROWS, D), _F32).at[0:B].set(c).at[CTX_COND_ROW].set(c_ctx)
    mod = _modulation(cond, ada_w, ada_b).reshape(DEPTH, COND_ROWS, 1, 3 * D_MODEL)
    cos_t, sin_t = _rope_tables(S)
    bias_tables = _na_bias_tables(na_rpb)

    layer_in = (ctx, x)
    merge = None
    for i in range(DEPTH):
        j = i // N_MIXERS
        if i % N_MIXERS == 0:
            w1 = jnp.concatenate([gla_dec_w1[j, 0], gla_dec_w1[j, 1]], axis=1)
            w1 = jnp.pad(w1, ((0, 0), (0, LOW_PAD - 2 * GLA_GATE_RANK)))
            w_cat = jnp.concatenate([gla_w_in[j], w1], axis=1).astype(_BF16)
            w2_cat = jnp.zeros((LOW_PAD, 2 * GLA_KEY_W), _F32)
            w2_cat = w2_cat.at[0:GLA_GATE_RANK, 0:GLA_KEY_W].set(gla_dec_w2[j, 0])
            w2_cat = w2_cat.at[GLA_GATE_RANK:2 * GLA_GATE_RANK, GLA_KEY_W:].set(gla_dec_w2[j, 1])
            dec_b = gla_dec_b[j].reshape(1, 2 * GLA_KEY_W) * LOG2E
            x_new, (qf, kf, kef, qb, kb, keb, v, g, dec) = _gla_project(
                layer_in, merge, mod[i], w_cat, (w2_cat * LOG2E).astype(_BF16), dec_b, cos_t, sin_t)
            o = _gla_mix(qf, kf, kef, qb, kb, keb, v, dec, gla_norm_g[j])
        else:
            x_new, (q, k, v, g) = _na_project(layer_in, merge, mod[i], na_w_in[j].astype(_BF16))
            o = _na_mix(q, k, v, bias_tables, j, ctx_queries=i < DEPTH - 1)
        if x_new is not None:
            layer_in = (None, x_new)
        merge = (o, g) + layer_in + (mod[i], w_out[i].astype(_BF16), ln_g[i], ln_b[i])
    o, g, _, xs, mod_last, w_o, lg, lb = merge
    return _final_merge(o, g, xs, mod_last, w_o, lg, lb)
```

```python
import functools

import numpy as np
import jax
import jax.numpy as jnp
from jax import lax
from jax.experimental import pallas as pl
from jax.experimental.pallas import tpu as pltpu

D_MODEL = 1024
BATCH = 8
SEQ_LEN = 2048
DEPTH = 4
GRID_W = 64
CTX_LEN = 256
STREAM_LEN = CTX_LEN + SEQ_LEN
N_MIXERS = 2
BRANCH = D_MODEL
GLA_HEADS = 4
GLA_DK = 128
GLA_DV = 256
GLA_KEY_W = GLA_HEADS * GLA_DK
GLA_GATE_RANK = 16
GLA_TAU = 16.0
GLA_CHUNK = 64
NA_HEADS = 16
NA_DH = 64
NA_KH = 8
NA_KW = 16
ROPE_BASE = 10000.0
LN_EPS = 1e-5
NORM_EPS = 1e-6
ALPHA = (2 * DEPTH) ** 0.25

LANES = 128
TOKEN_TILE = 256
SUBS_PER_STEP = 3
STEP_ROWS = SUBS_PER_STEP * TOKEN_TILE
CHUNKS_PER_TILE = TOKEN_TILE // GLA_CHUNK
COND_ROWS = 16
CTX_COND_ROW = 8
LOW_PAD = 128
NEG_BIG = -1e30
LOG2E = 1.4426950408889634
VMEM_LIMIT = 58 * 1024 * 1024

_F32 = jnp.float32
_BF16 = jnp.bfloat16


def _dot(a, b):
    return jnp.dot(a, b, preferred_element_type=_F32)


def _dot_nt(a, b):
    return lax.dot_general(a, b, (((1,), (1,)), ((), ())), preferred_element_type=_F32)


def _dot_tn(a, b):
    return lax.dot_general(a, b, (((0,), (0,)), ((), ())), preferred_element_type=_F32)


def _silu(x):
    return x * (1.0 / (1.0 + jnp.exp(-x)))


def _const_spec(shape):
    return pl.BlockSpec(shape, lambda b, t: (0,) * len(shape), pipeline_mode=pl.Buffered(1))


def _mod_kernel(cond_ref, w_ref, b_ref, out_ref):
    s = _silu(cond_ref[...]).astype(_BF16)
    out_ref[0] = _dot(s, w_ref[0].astype(_BF16)) + b_ref[0]


def _modulation(cond, ada_w, ada_b):
    n_col = 1
    width = 3 * D_MODEL // n_col
    return pl.pallas_call(
        _mod_kernel,
        grid=(DEPTH, n_col),
        in_specs=[
            pl.BlockSpec((COND_ROWS, D_MODEL), lambda i, j: (0, 0)),
            pl.BlockSpec((1, D_MODEL, width), lambda i, j: (i, 0, j)),
            pl.BlockSpec((1, 1, width), lambda i, j: (i, 0, j)),
        ],
        out_specs=pl.BlockSpec((1, COND_ROWS, width), lambda i, j: (i, 0, j)),
        out_shape=jax.ShapeDtypeStruct((DEPTH, COND_ROWS, 3 * D_MODEL), _F32),
        compiler_params=pltpu.CompilerParams(
            dimension_semantics=("arbitrary", "arbitrary"), vmem_limit_bytes=VMEM_LIMIT),
        name="modulation",
    )(cond, ada_w, ada_b.reshape(DEPTH, 1, 3 * D_MODEL))


def _mod_specs():
    return [pl.BlockSpec((1, 1, 3 * D_MODEL), lambda b, t: (b, 0, 0)),
            pl.BlockSpec((1, 1, 3 * D_MODEL), lambda b, t: (CTX_COND_ROW, 0, 0))]


def _sub_tile_mod(modb_ref, modc_ref, is_ctx):
    if is_ctx is False:
        return modb_ref[0]
    return jnp.where(is_ctx, modc_ref[0], modb_ref[0])


def _modulate(x, m):
    return (x * (1.0 + m[:, D_MODEL:2 * D_MODEL]) + m[:, 0:D_MODEL]).astype(_BF16)


N_STREAM_REFS = 1 + SUBS_PER_STEP
N_MERGE_REFS = 2 + N_STREAM_REFS + 5


def _stream_operands(ctx, lat):
    if ctx is None:
        ctx, shift = lat, 0
    else:
        shift = 1
    specs = [pl.BlockSpec((1, TOKEN_TILE, D_MODEL), lambda b, t: (b, 0, 0))]
    for s in range(SUBS_PER_STEP):
        specs.append(pl.BlockSpec(
            (1, TOKEN_TILE, D_MODEL),
            lambda b, t, s=s: (b, jnp.maximum(SUBS_PER_STEP * t + s - shift, 0), 0)))
    return specs, [ctx] + [lat] * SUBS_PER_STEP


def _stream_sub_tile(stream_refs, sub, t):
    x = stream_refs[1 + sub][0]
    if sub == 0:
        x = jnp.where(t == 0, stream_refs[0][0], x)
    return x


def _gated_out_proj(o, g, w_ref):
    return _dot(o * (g * (1.0 / (1.0 + jnp.exp(-g)))), w_ref[...])


def _residual_layer_norm(x, y, gate, lng_ref, lnb_ref):
    z = x + (gate * (1.0 / ALPHA)) * y
    mu = jnp.mean(z, axis=-1, keepdims=True)
    zc = z - mu
    var = jnp.mean(zc * zc, axis=-1, keepdims=True)
    return zc * lax.rsqrt(var + LN_EPS / ALPHA ** 2) * lng_ref[...] + lnb_ref[...]


def _layer_input_tiles(refs, fused):
    t = pl.program_id(1)
    if not fused:
        stream, rest = refs[:N_STREAM_REFS], refs[N_STREAM_REFS:]
        return (lambda sub, xnew_ref: _stream_sub_tile(stream, sub, t)), rest
    o_ref, g_ref = refs[0], refs[1]
    stream = refs[2:2 + N_STREAM_REFS]
    pmodb_ref, pmodc_ref, wout_ref, lng_ref, lnb_ref = refs[2 + N_STREAM_REFS:N_MERGE_REFS]
    ys = [_gated_out_proj(o_ref[0, sub * TOKEN_TILE:(sub + 1) * TOKEN_TILE, :],
                          g_ref[0, sub * TOKEN_TILE:(sub + 1) * TOKEN_TILE, :], wout_ref)
          for sub in range(SUBS_PER_STEP)]

    def tile(sub, xnew_ref):
        rows = slice(sub * TOKEN_TILE, (sub + 1) * TOKEN_TILE)
        m = _sub_tile_mod(pmodb_ref, pmodc_ref, (t == 0) if sub == 0 else False)
        x_new = _residual_layer_norm(_stream_sub_tile(stream, sub, t), ys[sub],
                                     m[:, 2 * D_MODEL:3 * D_MODEL], lng_ref, lnb_ref)
        xnew_ref[0, rows, :] = x_new
        return x_new

    return tile, refs[N_MERGE_REFS:]


def _merge_operands(merge):
    if merge is None:
        return [], []
    o, g, ctx, lat, mod_prev, w_out, ln_g, ln_b = merge
    step = pl.BlockSpec((1, STEP_ROWS, BRANCH), lambda b, t: (b, t, 0))
    s_specs, s_args = _stream_operands(ctx, lat)
    specs = [step, step] + s_specs + _mod_specs() + [
        _const_spec((BRANCH, D_MODEL)), _const_spec((1, D_MODEL)), _const_spec((1, D_MODEL))]
    args = [o, g] + s_args + [mod_prev, mod_prev, w_out, ln_g.reshape(1, D_MODEL), ln_b.reshape(1, D_MODEL)]
    return specs, args


def _gla_proj_kernel(*refs, fused):
    tile_fn, refs = _layer_input_tiles(refs, fused)
    modb_ref, modc_ref, w_ref, w2_ref, decb_ref, cos_ref, sin_ref, tri_ref = refs[:8]
    outs = refs[8:]
    xnew_ref = outs[0] if fused else None
    qf_ref, kf_ref, qb_ref, kb_ref, v_ref, g_ref, dec_ref = outs[1 if fused else 0:]
    t = pl.program_id(1)
    kw = GLA_KEY_W
    tiles = []
    for sub in range(SUBS_PER_STEP):
        rows = slice(sub * TOKEN_TILE, (sub + 1) * TOKEN_TILE)
        m = _sub_tile_mod(modb_ref, modc_ref, (t == 0) if sub == 0 else False)
        hb = _modulate(tile_fn(sub, xnew_ref), m)
        q = _rope(_dot(hb, w_ref[:, 0:kw]) * (GLA_DK ** -0.5), cos_ref[rows, :], sin_ref[rows, :])
        low = _dot(hb, w_ref[:, 2 * kw + 2 * BRANCH:2 * kw + 2 * BRANCH + LOW_PAD])
        y = _dot(low.astype(_BF16), w2_ref[...]) + decb_ref[...]
        la = (jnp.minimum(y, 0.0) - jnp.log2(1.0 + jnp.exp2(-jnp.abs(y)))) * (1.0 / GLA_TAU)
        tiles.append((rows, hb, q, la))
    for sub, (rows, hb, q, la) in enumerate(tiles):
        pre = _dot(tri_ref[...], la.astype(_BF16))
        k = _rope(_dot(hb, w_ref[:, kw:2 * kw]), cos_ref[rows, :], sin_ref[rows, :])
        v_ref[0, rows, :] = _dot(hb, w_ref[:, 2 * kw:2 * kw + BRANCH]).astype(_BF16)
        g_ref[0, rows, :] = _dot(hb, w_ref[:, 2 * kw + BRANCH:2 * kw + 2 * BRANCH]).astype(_BF16)
        key_outs = [r.at[0, rows, :] for r in (qf_ref, kf_ref, qb_ref, kb_ref)]
        _gla_decay_factors(q, k, la, pre, *key_outs, dec_ref.at[0, sub])


def _rope(x, cos, sin):
    heads = []
    for h in range(x.shape[1] // GLA_DK):
        xh = x[:, h * GLA_DK:(h + 1) * GLA_DK]
        heads.append(xh * cos + pltpu.roll(xh, GLA_DK // 2, 1) * sin)
    return jnp.concatenate(heads, axis=1)


def _gla_decay_factors(q, k, la, pre, qf_ref, kf_ref, qb_ref, kb_ref, dec_ref):
    kw = GLA_KEY_W
    tot4 = jnp.concatenate([pre[(c + 1) * GLA_CHUNK - 1:(c + 1) * GLA_CHUNK, :] for c in range(CHUNKS_PER_TILE)],
                           axis=0)

    def chunk_rows(t4):
        return jnp.broadcast_to(t4[:, None, :], (CHUNKS_PER_TILE, GLA_CHUNK, GLA_DK)).reshape(TOKEN_TILE, GLA_DK)

    for h in range(GLA_HEADS):
        sl = slice(h * GLA_DK, (h + 1) * GLA_DK)
        sl_b = slice(kw + h * GLA_DK, kw + (h + 1) * GLA_DK)
        pre_f, tot_f = pre[:, sl], chunk_rows(tot4[:, sl])
        pre_b, tot_b = pre[:, sl_b], chunk_rows(tot4[:, sl_b])
        ante_b = pre_b - la[:, sl_b]
        qh = q[:, sl]
        kh = k[:, sl]
        qf_ref[:, sl] = (qh * jnp.exp2(pre_f)).astype(_BF16)
        kf_ref[:, sl] = (kh * jnp.exp2(-pre_f)).astype(_BF16)
        qb_ref[:, sl] = (qh * jnp.exp2(tot_b - pre_b)).astype(_BF16)
        kb_ref[:, sl] = (kh * jnp.exp2(ante_b - tot_b)).astype(_BF16)

    dec_ref[...] = jnp.exp2(jnp.concatenate([tot4[:, 0:kw], tot4[:, kw:2 * kw]], axis=0))


def _chunk_prefix_matrix():
    i = np.arange(TOKEN_TILE)
    same = (i[:, None] // GLA_CHUNK) == (i[None, :] // GLA_CHUNK)
    return jnp.asarray(same & (i[None, :] <= i[:, None]), _BF16)


def _gla_project(layer_in, merge, mod_i, w_cat, w2_cat, dec_b, cos_t, sin_t):
    fused = merge is not None
    B = (merge[0] if fused else layer_in[1]).shape[0]
    L = STREAM_LEN
    n_w = w_cat.shape[1]
    tri = _chunk_prefix_matrix()
    in_specs, in_args = _merge_operands(merge) if fused else _stream_operands(*layer_in)
    key_spec = pl.BlockSpec((1, STEP_ROWS, GLA_KEY_W), lambda b, t: (b, t, 0))
    val_spec = pl.BlockSpec((1, STEP_ROWS, BRANCH), lambda b, t: (b, t, 0))
    key_shape = jax.ShapeDtypeStruct((B, L, GLA_KEY_W), _BF16)
    val_shape = jax.ShapeDtypeStruct((B, L, BRANCH), _BF16)
    rope_spec = pl.BlockSpec((STEP_ROWS, GLA_DK), lambda b, t: (t, 0))
    x_specs = [pl.BlockSpec((1, STEP_ROWS, D_MODEL), lambda b, t: (b, t, 0))] if fused else []
    x_shapes = [jax.ShapeDtypeStruct((B, L, D_MODEL), _F32)] if fused else []
    outs = pl.pallas_call(
        functools.partial(_gla_proj_kernel, fused=fused),
        grid=(B, L // STEP_ROWS),
        in_specs=in_specs + _mod_specs() + [
            _const_spec((D_MODEL, n_w)),
            _const_spec((LOW_PAD, 2 * GLA_KEY_W)),
            _const_spec((1, 2 * GLA_KEY_W)),
            rope_spec, rope_spec,
            _const_spec((TOKEN_TILE, TOKEN_TILE)),
        ],
        out_specs=x_specs + [key_spec] * 4 + [val_spec] * 2 + [
            pl.BlockSpec((1, SUBS_PER_STEP, 2 * CHUNKS_PER_TILE, GLA_KEY_W), lambda b, t: (b, t, 0, 0))],
        out_shape=x_shapes + [key_shape] * 4 + [val_shape] * 2 + [
            jax.ShapeDtypeStruct((B, L // TOKEN_TILE, 2 * CHUNKS_PER_TILE, GLA_KEY_W), _F32)],
        compiler_params=pltpu.CompilerParams(
            dimension_semantics=("arbitrary", "arbitrary"), vmem_limit_bytes=VMEM_LIMIT),
        name="gla_project",
    )(*in_args, mod_i, mod_i, w_cat, w2_cat, dec_b, cos_t, sin_t, tri)
    return (outs[0], outs[1:]) if fused else (None, outs)


GLA_HEADS_PER_STEP = 2
GLA_CHUNKS_PER_STEP = 6


def _gla_mix_kernel(qf_ref, kf_ref, qb_ref, kb_ref, v_ref, dec_ref, ng_ref,
                    out_ref, accf_ref, accb_ref, sf_ref, sb_ref):
    n_chunks = accf_ref.shape[0] // GLA_CHUNK
    n_ctx = CTX_LEN // GLA_CHUNK
    C = GLA_CHUNK
    sf_ref[...] = jnp.zeros_like(sf_ref)
    sb_ref[...] = jnp.zeros_like(sb_ref)
    ti = lax.broadcasted_iota(jnp.int32, (C, C), 0)
    si = lax.broadcasted_iota(jnp.int32, (C, C), 1)
    mask_f = si <= ti
    mask_b = si > ti
    sub8 = lax.broadcasted_iota(jnp.int32, (2 * CHUNKS_PER_TILE, GLA_DK), 0)

    fwd = (qf_ref, kf_ref, sf_ref, accf_ref, mask_f, 0)
    bwd = (qb_ref, kb_ref, sb_ref, accb_ref, mask_b, CHUNKS_PER_TILE)

    def body(i, carry):
        chains = []
        for u in range(GLA_CHUNKS_PER_STEP):
            cf = i * GLA_CHUNKS_PER_STEP + u
            cb = jnp.where(cf < n_ctx, n_ctx - 1 - cf, n_chunks + n_ctx - 1 - cf)
            for c, (q_ref, k_ref, s_ref, acc_ref, mask, dec_row) in ((cf, fwd), (cb, bwd)):
                for h in range(GLA_HEADS_PER_STEP):
                    chains.append(dict(u=u, c=c, h=h, q_ref=q_ref, k_ref=k_ref, s_ref=s_ref,
                                       acc_ref=acc_ref, mask=mask, dec_row=dec_row))
        for ch in chains:
            c, h = ch["c"], ch["h"]
            ch["rows"] = pl.ds(pl.multiple_of(c * C, C), C)
            ch["ksl"] = slice(h * GLA_DK, (h + 1) * GLA_DK)
            ch["vsl"] = slice(h * GLA_DV, (h + 1) * GLA_DV)
            ch["q"] = ch["q_ref"][0, ch["rows"], ch["ksl"]]
            ch["k"] = ch["k_ref"][0, ch["rows"], ch["ksl"]]
            ch["v"] = v_ref[0, ch["rows"], ch["vsl"]]
            ch["att"] = _dot_nt(ch["q"], ch["k"])
        for ch in chains:
            tile = lax.shift_right_logical(ch["c"], CHUNKS_PER_TILE.bit_length() - 1)
            dec8 = dec_ref[0, tile, :, ch["ksl"]]
            pick = sub8 == ch["dec_row"] + (ch["c"] & (CHUNKS_PER_TILE - 1))
            dec = jnp.sum(jnp.where(pick, dec8, 0.0), axis=0, keepdims=True)
            k_end = (ch["k"].astype(_F32) * dec).astype(_BF16)
            ch["kv"] = _dot_tn(k_end, ch["v"])
            dcol = jnp.broadcast_to(dec, (GLA_DK, GLA_DK)).T
            ch["decay"] = jnp.concatenate([dcol, dcol], axis=1)
        state = {}
        for u in range(GLA_CHUNKS_PER_STEP):
            for ch in chains:
                if ch["u"] != u:
                    continue
                key = (id(ch["s_ref"]), ch["h"])
                s = state[key] if u else ch["s_ref"][ch["h"]]
                att = jnp.where(ch["mask"], ch["att"], 0.0).astype(_BF16)
                lhs = jnp.concatenate([ch["q"], att], axis=1)
                rhs = jnp.concatenate([s.astype(_BF16), ch["v"]], axis=0)
                ch["acc_ref"][ch["rows"], ch["vsl"]] = _dot(lhs, rhs)
                state[key] = s * ch["decay"] + ch["kv"]
        for ch in chains:
            if ch["u"] == GLA_CHUNKS_PER_STEP - 1:
                ch["s_ref"][ch["h"]] = state[(id(ch["s_ref"]), ch["h"])]
        return carry

    lax.fori_loop(0, n_chunks // GLA_CHUNKS_PER_STEP, body, 0)

    def norm_body(j, carry):
        rows = pl.ds(pl.multiple_of(j * TOKEN_TILE, TOKEN_TILE), TOKEN_TILE)
        for h in range(GLA_HEADS_PER_STEP):
            vsl = slice(h * GLA_DV, (h + 1) * GLA_DV)
            o = accf_ref[rows, vsl] + accb_ref[rows, vsl]
            ms = jnp.mean(o * o, axis=-1, keepdims=True)
            out_ref[0, rows, vsl] = (o * lax.rsqrt(ms + NORM_EPS) * ng_ref[...]).astype(_BF16)
        return carry

    lax.fori_loop(0, accf_ref.shape[0] // TOKEN_TILE, norm_body, 0)


def _gla_mix(qf, kf, qb, kb, v, dec, norm_g):
    B, L, _ = qf.shape
    hs = GLA_HEADS_PER_STEP
    n_t = dec.shape[1]
    key_spec = pl.BlockSpec((1, L, hs * GLA_DK), lambda b, p: (b, 0, p))
    val_spec = pl.BlockSpec((1, L, hs * GLA_DV), lambda b, p: (b, 0, p))
    return pl.pallas_call(
        _gla_mix_kernel,
        grid=(B, GLA_HEADS // hs),
        in_specs=[key_spec] * 4 + [
            val_spec,
            pl.BlockSpec((1, n_t, 2 * CHUNKS_PER_TILE, hs * GLA_DK), lambda b, p: (b, 0, 0, p)),
            pl.BlockSpec((1, GLA_DV), lambda b, p: (0, 0)),
        ],
        out_specs=val_spec,
        out_shape=jax.ShapeDtypeStruct((B, L, BRANCH), _BF16),
        scratch_shapes=[
            pltpu.VMEM((L, hs * GLA_DV), _F32),
            pltpu.VMEM((L, hs * GLA_DV), _F32),
            pltpu.VMEM((hs, GLA_DK, GLA_DV), _F32),
            pltpu.VMEM((hs, GLA_DK, GLA_DV), _F32),
        ],
        compiler_params=pltpu.CompilerParams(
            dimension_semantics=("arbitrary", "arbitrary"), vmem_limit_bytes=VMEM_LIMIT),
        name="gla_mix",
    )(qf, kf, qb, kb, v, dec, norm_g.reshape(1, GLA_DV))


def _na_proj_kernel(*refs, fused):
    tile_fn, refs = _layer_input_tiles(refs, fused)
    modb_ref, modc_ref, w_ref = refs[:3]
    outs = refs[3:]
    xnew_ref = outs[0] if fused else None
    q_ref, k_ref, v_ref, g_ref = outs[1 if fused else 0:]
    t = pl.program_id(1)
    for sub in range(SUBS_PER_STEP):
        rows = slice(sub * TOKEN_TILE, (sub + 1) * TOKEN_TILE)
        m = _sub_tile_mod(modb_ref, modc_ref, (t == 0) if sub == 0 else False)
        hb = _modulate(tile_fn(sub, xnew_ref), m)
        q_ref[0, rows, :] = (_dot(hb, w_ref[:, 0:BRANCH]) * (NA_DH ** -0.5 * LOG2E)).astype(_BF16)
        k_ref[0, rows, :] = _dot(hb, w_ref[:, BRANCH:2 * BRANCH]).astype(_BF16)
        v_ref[0, rows, :] = _dot(hb, w_ref[:, 2 * BRANCH:3 * BRANCH]).astype(_BF16)
        g_ref[0, rows, :] = _dot(hb, w_ref[:, 3 * BRANCH:4 * BRANCH]).astype(_BF16)


def _na_project(layer_in, merge, mod_i, w_in):
    fused = merge is not None
    B = (merge[0] if fused else layer_in[1]).shape[0]
    L = STREAM_LEN
    in_specs, in_args = _merge_operands(merge) if fused else _stream_operands(*layer_in)
    spec = pl.BlockSpec((1, STEP_ROWS, BRANCH), lambda b, t: (b, t, 0))
    shape = jax.ShapeDtypeStruct((B, L, BRANCH), _BF16)
    x_specs = [pl.BlockSpec((1, STEP_ROWS, D_MODEL), lambda b, t: (b, t, 0))] if fused else []
    x_shapes = [jax.ShapeDtypeStruct((B, L, D_MODEL), _F32)] if fused else []
    outs = pl.pallas_call(
        functools.partial(_na_proj_kernel, fused=fused),
        grid=(B, L // STEP_ROWS),
        in_specs=in_specs + _mod_specs() + [_const_spec((D_MODEL, 4 * BRANCH))],
        out_specs=x_specs + [spec] * 4,
        out_shape=x_shapes + [shape] * 4,
        compiler_params=pltpu.CompilerParams(
            dimension_semantics=("arbitrary", "arbitrary"), vmem_limit_bytes=VMEM_LIMIT),
        name="na_project",
    )(*in_args, mod_i, mod_i, w_in)
    return (outs[0], outs[1:]) if fused else (None, outs)


NA_ROWS = SEQ_LEN // GRID_W
NA_PAIR_W = 2 * NA_DH
NA_BAND = NA_KH * GRID_W
NA_ROWS_PER_STEP = 4
NA_BATCH_PER_STEP = 8


def _stack_heads(q2):
    lane = lax.broadcasted_iota(jnp.int32, q2.shape, 1)
    zero = jnp.zeros_like(q2)
    return jnp.concatenate([jnp.where(lane < NA_DH, q2, zero), jnp.where(lane >= NA_DH, q2, zero)], axis=0)


def _unstack_heads(o, n):
    lane = lax.broadcasted_iota(jnp.int32, (n, NA_PAIR_W), 1)
    return jnp.where(lane < NA_DH, o[0:n], o[n:2 * n])


def _na_mix_kernel(q_ref, k_ref, v_ref, bias_ref, out_ref, sl_a, sc_a, sl_b, sc_b, *, ctx_queries):
    for e in range(NA_BATCH_PER_STEP):
        if ctx_queries:
            kc = k_ref[e, 0:CTX_LEN, :]
            vc = v_ref[e, 0:CTX_LEN, :]
            qs = _stack_heads(q_ref[e, 0:CTX_LEN, :])
            s = _dot_nt(qs, kc)
            m = jnp.max(s, axis=-1, keepdims=True)
            p = jnp.exp2(s - m)
            l = jnp.sum(p, axis=-1, keepdims=True)
            o = _dot(p.astype(_BF16), vc) / l
            out_ref[e, 0:CTX_LEN, :] = _unstack_heads(o, CTX_LEN).astype(_BF16)
        else:
            out_ref[e, 0:CTX_LEN, :] = jnp.zeros((CTX_LEN, NA_PAIR_W), _BF16)

    U = NA_ROWS_PER_STEP
    M = 2 * GRID_W
    groups_per_elem = NA_ROWS // U
    n_groups = NA_BATCH_PER_STEP * groups_per_elem

    def row_slices(g, u):
        g = jnp.asarray(g, jnp.int32)
        e = lax.shift_right_logical(g, groups_per_elem.bit_length() - 1)
        r = (g & (groups_per_elem - 1)) * U + u
        r_start = jnp.clip(r - NA_KH // 2, 0, NA_ROWS - NA_KH)
        q_rows = pl.ds(pl.multiple_of(CTX_LEN + r * GRID_W, GRID_W), GRID_W)
        band = pl.ds(pl.multiple_of(CTX_LEN + r_start * GRID_W, GRID_W), NA_BAND)
        return e, q_rows, band, r - r_start

    def scores(g, sl_ref, sc_ref):
        qss = []
        for u in range(U):
            e, q_rows, band, row_class = row_slices(g, u)
            qss.append(_stack_heads(q_ref[e, q_rows, :]))
            sl_ref[u] = _dot_nt(qss[u], k_ref[e, band, :]) + bias_ref[0, row_class]
        sc_ref[...] = _dot_nt(jnp.concatenate(qss, axis=0), k_ref[e, 0:CTX_LEN, :])

    def finish(g, sl_ref, sc_ref):
        p_locs, p_ctxs, ls = [], [], []
        for u in range(U):
            s_loc = sl_ref[u]
            s_ctx = sc_ref[u * M:(u + 1) * M, :]
            m = jnp.maximum(jnp.max(s_loc, axis=-1, keepdims=True), jnp.max(s_ctx, axis=-1, keepdims=True))
            p_loc = jnp.exp2(s_loc - m)
            p_ctx = jnp.exp2(s_ctx - m)
            ls.append(jnp.sum(p_loc, axis=-1, keepdims=True) + jnp.sum(p_ctx, axis=-1, keepdims=True))
            p_locs.append(p_loc.astype(_BF16))
            p_ctxs.append(p_ctx.astype(_BF16))
        e = row_slices(g, 0)[0]
        o_ctx_all = _dot(jnp.concatenate(p_ctxs, axis=0), v_ref[e, 0:CTX_LEN, :])
        for u in range(U):
            _, q_rows, band, _ = row_slices(g, u)
            o = (_dot(p_locs[u], v_ref[e, band, :]) + o_ctx_all[u * M:(u + 1) * M]) / ls[u]
            out_ref[e, q_rows, :] = _unstack_heads(o, GRID_W).astype(_BF16)

    buf_a = (sl_a, sc_a)
    buf_b = (sl_b, sc_b)
    scores(0, *buf_a)

    def pair_body(i, carry):
        g = 2 * i
        scores(g + 1, *buf_b)
        finish(g, *buf_a)
        scores(g + 2, *buf_a)
        finish(g + 1, *buf_b)
        return carry

    lax.fori_loop(0, n_groups // 2 - 1, pair_body, 0)
    scores(n_groups - 1, *buf_b)
    finish(n_groups - 2, *buf_a)
    finish(n_groups - 1, *buf_b)


RPB_ROWS_PAD = 16


def _na_bias_kernel(rpb_ref, out_ref):
    c = lax.broadcasted_iota(jnp.int32, (GRID_W, LANES), 0)
    lane = lax.broadcasted_iota(jnp.int32, (GRID_W, LANES), 1)
    kc = lane & (GRID_W - 1)
    col_start = jnp.clip(c - NA_KW // 2, 0, GRID_W - NA_KW)
    valid = (kc >= col_start) & (kc < col_start + NA_KW)
    low_half = lane < GRID_W
    for a in range(2):
        toe = []
        for ro in range(2 * NA_KH - 1):
            w = jnp.broadcast_to(rpb_ref[0, a, ro:ro + 1, :], (GRID_W, LANES)) * LOG2E
            t_lo = pltpu.roll(w, LANES - (NA_KW - 1), 1, stride=1, stride_axis=0)
            t_hi = pltpu.roll(w, GRID_W - (NA_KW - 1), 1, stride=1, stride_axis=0)
            toe.append((t_lo, t_hi))
        for d in range(NA_KH):
            for jj in range(NA_KH // 2):
                ro = 2 * jj - d + NA_KH - 1
                tile = jnp.where(low_half, toe[ro][0], toe[ro + 1][1])
                out_ref[0, d, a * GRID_W:(a + 1) * GRID_W, jj * LANES:(jj + 1) * LANES] = (
                    jnp.where(valid, tile, NEG_BIG))


NA_PAIRS = NA_HEADS // 2


def _na_bias_tables(rpb):
    n_layers, _, n_ro, n_co = rpb.shape
    rpb = jnp.pad(rpb, ((0, 0), (0, 0), (0, RPB_ROWS_PAD - n_ro), (0, LANES - n_co)))
    n = n_layers * NA_PAIRS
    return pl.pallas_call(
        _na_bias_kernel,
        grid=(n,),
        in_specs=[pl.BlockSpec((1, 2, RPB_ROWS_PAD, LANES), lambda p: (p, 0, 0, 0))],
        out_specs=pl.BlockSpec((1, NA_KH, 2 * GRID_W, NA_BAND), lambda p: (p, 0, 0, 0)),
        out_shape=jax.ShapeDtypeStruct((n, NA_KH, 2 * GRID_W, NA_BAND), _F32),
        compiler_params=pltpu.CompilerParams(dimension_semantics=("arbitrary",)),
        name="na_bias_table",
    )(rpb.reshape(n, 2, RPB_ROWS_PAD, LANES))


def _na_mix(q, k, v, bias_tables, na_layer, ctx_queries):
    B, L, _ = q.shape
    n_pairs = NA_PAIRS
    U = NA_ROWS_PER_STEP
    spec = pl.BlockSpec((NA_BATCH_PER_STEP, L, NA_PAIR_W), lambda p, b: (b, 0, p))
    score_bufs = [pltpu.VMEM((U, 2 * GRID_W, NA_BAND), _F32),
                  pltpu.VMEM((U * 2 * GRID_W, CTX_LEN), _F32)]
    return pl.pallas_call(
        functools.partial(_na_mix_kernel, ctx_queries=ctx_queries),
        grid=(n_pairs, B // NA_BATCH_PER_STEP),
        in_specs=[spec, spec, spec,
                  pl.BlockSpec((1, NA_KH, 2 * GRID_W, NA_BAND),
                               lambda p, b: (na_layer * NA_PAIRS + p, 0, 0, 0))],
        out_specs=spec,
        out_shape=jax.ShapeDtypeStruct((B, L, BRANCH), _BF16),
        scratch_shapes=score_bufs + score_bufs,
        compiler_params=pltpu.CompilerParams(
            dimension_semantics=("arbitrary", "arbitrary"), vmem_limit_bytes=VMEM_LIMIT),
        name="na_mix",
    )(q, k, v, bias_tables)


FINAL_SUBS = 4


def _final_merge_kernel(*refs):
    n = FINAL_SUBS
    o_refs, g_refs, x_refs = refs[0:n], refs[n:2 * n], refs[2 * n:3 * n]
    mod_ref, w_ref, lng_ref, lnb_ref, out_ref = refs[3 * n:]
    gate = mod_ref[0][:, 2 * D_MODEL:3 * D_MODEL]
    for sub in range(n):
        out_ref[0, sub * TOKEN_TILE:(sub + 1) * TOKEN_TILE, :] = _residual_layer_norm(
            x_refs[sub][0], _gated_out_proj(o_refs[sub][0], g_refs[sub][0], w_ref), gate, lng_ref, lnb_ref)


def _final_merge(o, g, xs, mod_i, w_out, ln_g, ln_b):
    B, L, _ = xs.shape
    rows = FINAL_SUBS * TOKEN_TILE
    n_t = (L - CTX_LEN) // rows

    def sub_specs(width):
        return [pl.BlockSpec((1, TOKEN_TILE, width), lambda b, t, s=s: (b, FINAL_SUBS * t + s + 1, 0))
                for s in range(FINAL_SUBS)]

    return pl.pallas_call(
        _final_merge_kernel,
        grid=(B, n_t),
        in_specs=sub_specs(BRANCH) + sub_specs(BRANCH) + sub_specs(D_MODEL) + [
            _mod_specs()[0],
            _const_spec((BRANCH, D_MODEL)), _const_spec((1, D_MODEL)), _const_spec((1, D_MODEL)),
        ],
        out_specs=pl.BlockSpec((1, rows, D_MODEL), lambda b, t: (b, t, 0)),
        out_shape=jax.ShapeDtypeStruct((B, n_t * rows, D_MODEL), _F32),
        compiler_params=pltpu.CompilerParams(
            dimension_semantics=("arbitrary", "arbitrary"), vmem_limit_bytes=VMEM_LIMIT),
        name="final_merge",
    )(*([o] * FINAL_SUBS + [g] * FINAL_SUBS + [xs] * FINAL_SUBS), mod_i, w_out,
      ln_g.reshape(1, D_MODEL), ln_b.reshape(1, D_MODEL))


def _rope_tables(n_tokens):
    pos = jnp.arange(n_tokens, dtype=jnp.int32)
    row = (pos // GRID_W).astype(_F32)
    col = (pos % GRID_W).astype(_F32)
    quarter = GLA_DK // 4
    inv = ROPE_BASE ** (-jnp.arange(quarter, dtype=_F32) / quarter)
    ang = jnp.concatenate([row[:, None] * inv, col[:, None] * inv], -1)
    cos, sin = jnp.cos(ang), jnp.sin(ang)
    cos = jnp.concatenate([cos, cos], -1)
    sin = jnp.concatenate([-sin, sin], -1)
    return (jnp.concatenate([jnp.ones((CTX_LEN, GLA_DK), _F32), cos], 0),
            jnp.concatenate([jnp.zeros((CTX_LEN, GLA_DK), _F32), sin], 0))


def kernel(x, c, ctx, c_ctx, ada_w, ada_b, ln_g, ln_b, w_out, gla_w_in, gla_dec_w1, gla_dec_w2,
           gla_dec_b, gla_norm_g, na_w_in, na_rpb):
    B, S, D = x.shape
    assert (B, S, D) == (BATCH, SEQ_LEN, D_MODEL) and ctx.shape == (B, CTX_LEN, D)
    cond = jnp.zeros((COND_ROWS, D), _F32).at[0:B].set(c).at[CTX_COND_ROW].set(c_ctx)
    mod = _modulation(cond, ada_w, ada_b).reshape(DEPTH, COND_ROWS, 1, 3 * D_MODEL)
    cos_t, sin_t = _rope_tables(S)
    bias_tables = _na_bias_tables(na_rpb)

    layer_in = (ctx, x)
    merge = None
    for i in range(DEPTH):
        j = i // N_MIXERS
        if i % N_MIXERS == 0:
            w1 = jnp.concatenate([gla_dec_w1[j, 0], gla_dec_w1[j, 1]], axis=1)
            w1 = jnp.pad(w1, ((0, 0), (0, LOW_PAD - 2 * GLA_GATE_RANK)))
            w_cat = jnp.concatenate([gla_w_in[j], w1], axis=1).astype(_BF16)
            w2_cat = jnp.zeros((LOW_PAD, 2 * GLA_KEY_W), _F32)
            w2_cat = w2_cat.at[0:GLA_GATE_RANK, 0:GLA_KEY_W].set(gla_dec_w2[j, 0])
            w2_cat = w2_cat.at[GLA_GATE_RANK:2 * GLA_GATE_RANK, GLA_KEY_W:].set(gla_dec_w2[j, 1])
            dec_b = gla_dec_b[j].reshape(1, 2 * GLA_KEY_W) * LOG2E
            x_new, (qf, kf, qb, kb, v, g, dec) = _gla_project(
                layer_in, merge, mod[i], w_cat, (w2_cat * LOG2E).astype(_BF16), dec_b, cos_t, sin_t)
            o = _gla_mix(qf, kf, qb, kb, v, dec, gla_norm_g[j])
        else:
            x_new, (q, k, v, g) = _na_project(layer_in, merge, mod[i], na_w_in[j].astype(_BF16))
            o = _na_mix(q, k, v, bias_tables, j, ctx_queries=i < DEPTH - 1)
        if x_new is not None:
            layer_in = (None, x_new)
        merge = (o, g) + layer_in + (mod[i], w_out[i].astype(_BF16), ln_g[i], ln_b[i])
    o, g, _, xs, mod_last, w_o, lg, lb = merge
    return _final_merge(o, g, xs, mod_last, w_o, lg, lb)
```

```python
import functools

import numpy as np
import jax
import jax.numpy as jnp
from jax import lax
from jax.experimental import pallas as pl
from jax.experimental.pallas import tpu as pltpu

D_MODEL = 1024
BATCH = 8
SEQ_LEN = 2048
DEPTH = 4
GRID_W = 64
CTX_LEN = 256
STREAM_LEN = CTX_LEN + SEQ_LEN
N_MIXERS = 2
BRANCH = D_MODEL
GLA_HEADS = 4
GLA_DK = 128
GLA_DV = 256
GLA_KEY_W = GLA_HEADS * GLA_DK
GLA_GATE_RANK = 16
GLA_TAU = 16.0
GLA_CHUNK = 64
NA_HEADS = 16
NA_DH = 64
NA_KH = 8
NA_KW = 16
ROPE_BASE = 10000.0
LN_EPS = 1e-5
NORM_EPS = 1e-6
ALPHA = (2 * DEPTH) ** 0.25

LANES = 128
TOKEN_TILE = 256
SUBS_PER_STEP = 3
STEP_ROWS = SUBS_PER_STEP * TOKEN_TILE
CHUNKS_PER_TILE = TOKEN_TILE // GLA_CHUNK
COND_ROWS = 16
CTX_COND_ROW = 8
LOW_PAD = 128
NEG_BIG = -1e30
LOG2E = 1.4426950408889634
VMEM_LIMIT = 58 * 1024 * 1024

_F32 = jnp.float32
_BF16 = jnp.bfloat16


def _dot(a, b):
    return jnp.dot(a, b, preferred_element_type=_F32)


def _dot_nt(a, b):
    return lax.dot_general(a, b, (((1,), (1,)), ((), ())), preferred_element_type=_F32)


def _dot_tn(a, b):
    return lax.dot_general(a, b, (((0,), (0,)), ((), ())), preferred_element_type=_F32)


def _silu(x):
    return x * (1.0 / (1.0 + jnp.exp(-x)))


def _const_spec(shape):
    return pl.BlockSpec(shape, lambda b, t: (0,) * len(shape), pipeline_mode=pl.Buffered(1))


def _mod_kernel(cond_ref, w_ref, b_ref, out_ref):
    s = _silu(cond_ref[...]).astype(_BF16)
    out_ref[0] = _dot(s, w_ref[0].astype(_BF16)) + b_ref[0]


def _modulation(cond, ada_w, ada_b):
    n_col = 1
    width = 3 * D_MODEL // n_col
    return pl.pallas_call(
        _mod_kernel,
        grid=(DEPTH, n_col),
        in_specs=[
            pl.BlockSpec((COND_ROWS, D_MODEL), lambda i, j: (0, 0)),
            pl.BlockSpec((1, D_MODEL, width), lambda i, j: (i, 0, j)),
            pl.BlockSpec((1, 1, width), lambda i, j: (i, 0, j)),
        ],
        out_specs=pl.BlockSpec((1, COND_ROWS, width), lambda i, j: (i, 0, j)),
        out_shape=jax.ShapeDtypeStruct((DEPTH, COND_ROWS, 3 * D_MODEL), _F32),
        compiler_params=pltpu.CompilerParams(
            dimension_semantics=("arbitrary", "arbitrary"), vmem_limit_bytes=VMEM_LIMIT),
        name="modulation",
    )(cond, ada_w, ada_b.reshape(DEPTH, 1, 3 * D_MODEL))


def _mod_specs():
    return [pl.BlockSpec((1, 1, 3 * D_MODEL), lambda b, t: (b, 0, 0)),
            pl.BlockSpec((1, 1, 3 * D_MODEL), lambda b, t: (CTX_COND_ROW, 0, 0))]


def _sub_tile_mod(modb_ref, modc_ref, is_ctx):
    if is_ctx is False:
        return modb_ref[0]
    return jnp.where(is_ctx, modc_ref[0], modb_ref[0])


def _modulate(x, m):
    return (x * (1.0 + m[:, D_MODEL:2 * D_MODEL]) + m[:, 0:D_MODEL]).astype(_BF16)


N_STREAM_REFS = 1 + SUBS_PER_STEP
N_MERGE_REFS = 2 + N_STREAM_REFS + 6


def _stream_operands(ctx, lat):
    if ctx is None:
        ctx, shift = lat, 0
    else:
        shift = 1
    specs = [pl.BlockSpec((1, TOKEN_TILE, D_MODEL), lambda b, t: (b, 0, 0))]
    for s in range(SUBS_PER_STEP):
        specs.append(pl.BlockSpec(
            (1, TOKEN_TILE, D_MODEL),
            lambda b, t, s=s: (b, jnp.maximum(SUBS_PER_STEP * t + s - shift, 0), 0)))
    return specs, [ctx] + [lat] * SUBS_PER_STEP


def _stream_sub_tile(stream_refs, sub, t):
    x = stream_refs[1 + sub][0]
    if sub == 0:
        x = jnp.where(t == 0, stream_refs[0][0], x)
    return x


def _gated_out_proj(o, g, w_ref, head_gain_ref=None):
    if head_gain_ref is not None:
        of = o.astype(_F32)
        heads = []
        for h in range(GLA_HEADS):
            oh = of[:, h * GLA_DV:(h + 1) * GLA_DV]
            heads.append(oh * lax.rsqrt(jnp.mean(oh * oh, axis=-1, keepdims=True) + NORM_EPS))
        o = (jnp.concatenate(heads, axis=1) * head_gain_ref[...]).astype(_BF16)
    return _dot(o * (g * (1.0 / (1.0 + jnp.exp(-g)))), w_ref[...])


def _residual_layer_norm(x, y, gate, lng_ref, lnb_ref):
    z = x + (gate * (1.0 / ALPHA)) * y
    mu = jnp.mean(z, axis=-1, keepdims=True)
    zc = z - mu
    var = jnp.mean(zc * zc, axis=-1, keepdims=True)
    return zc * lax.rsqrt(var + LN_EPS / ALPHA ** 2) * lng_ref[...] + lnb_ref[...]


def _layer_input_tiles(refs, fused, head_norm=False):
    t = pl.program_id(1)
    if not fused:
        stream, rest = refs[:N_STREAM_REFS], refs[N_STREAM_REFS:]
        return (lambda sub, xnew_ref: _stream_sub_tile(stream, sub, t)), rest
    o_ref, g_ref = refs[0], refs[1]
    stream = refs[2:2 + N_STREAM_REFS]
    pmodb_ref, pmodc_ref, gain_ref, wout_ref, lng_ref, lnb_ref = refs[2 + N_STREAM_REFS:N_MERGE_REFS]
    ys = [_gated_out_proj(o_ref[0, sub * TOKEN_TILE:(sub + 1) * TOKEN_TILE, :],
                          g_ref[0, sub * TOKEN_TILE:(sub + 1) * TOKEN_TILE, :], wout_ref,
                          gain_ref if head_norm else None)
          for sub in range(SUBS_PER_STEP)]

    def tile(sub, xnew_ref):
        rows = slice(sub * TOKEN_TILE, (sub + 1) * TOKEN_TILE)
        m = _sub_tile_mod(pmodb_ref, pmodc_ref, (t == 0) if sub == 0 else False)
        x_new = _residual_layer_norm(_stream_sub_tile(stream, sub, t), ys[sub],
                                     m[:, 2 * D_MODEL:3 * D_MODEL], lng_ref, lnb_ref)
        xnew_ref[0, rows, :] = x_new
        return x_new

    return tile, refs[N_MERGE_REFS:]


def _merge_operands(merge):
    if merge is None:
        return [], []
    o, g, ctx, lat, mod_prev, head_gain, w_out, ln_g, ln_b = merge
    step = pl.BlockSpec((1, STEP_ROWS, BRANCH), lambda b, t: (b, t, 0))
    s_specs, s_args = _stream_operands(ctx, lat)
    specs = [step, step] + s_specs + _mod_specs() + [
        _const_spec((1, BRANCH)),
        _const_spec((BRANCH, D_MODEL)), _const_spec((1, D_MODEL)), _const_spec((1, D_MODEL))]
    args = [o, g] + s_args + [mod_prev, mod_prev, head_gain.reshape(1, BRANCH), w_out,
                              ln_g.reshape(1, D_MODEL), ln_b.reshape(1, D_MODEL)]
    return specs, args


def _gla_proj_kernel(*refs, fused):
    tile_fn, refs = _layer_input_tiles(refs, fused)
    modb_ref, modc_ref, w_ref, w2_ref, decb_ref, cos_ref, sin_ref, tri_ref = refs[:8]
    outs = refs[8:]
    xnew_ref = outs[0] if fused else None
    qf_ref, kf_ref, qb_ref, kb_ref, v_ref, g_ref, dec_ref = outs[1 if fused else 0:]
    t = pl.program_id(1)
    kw = GLA_KEY_W
    tiles = []
    for sub in range(SUBS_PER_STEP):
        rows = slice(sub * TOKEN_TILE, (sub + 1) * TOKEN_TILE)
        m = _sub_tile_mod(modb_ref, modc_ref, (t == 0) if sub == 0 else False)
        hb = _modulate(tile_fn(sub, xnew_ref), m)
        q = _rope(_dot(hb, w_ref[:, 0:kw]) * (GLA_DK ** -0.5), cos_ref[rows, :], sin_ref[rows, :])
        low = _dot(hb, w_ref[:, 2 * kw + 2 * BRANCH:2 * kw + 2 * BRANCH + LOW_PAD])
        y = _dot(low.astype(_BF16), w2_ref[...]) + decb_ref[...]
        la = (jnp.minimum(y, 0.0) - jnp.log2(1.0 + jnp.exp2(-jnp.abs(y)))) * (1.0 / GLA_TAU)
        tiles.append((rows, hb, q, la))
    for sub, (rows, hb, q, la) in enumerate(tiles):
        pre = _dot(tri_ref[...], la.astype(_BF16))
        k = _rope(_dot(hb, w_ref[:, kw:2 * kw]), cos_ref[rows, :], sin_ref[rows, :])
        v_ref[0, rows, :] = _dot(hb, w_ref[:, 2 * kw:2 * kw + BRANCH]).astype(_BF16)
        g_ref[0, rows, :] = _dot(hb, w_ref[:, 2 * kw + BRANCH:2 * kw + 2 * BRANCH]).astype(_BF16)
        key_outs = [r.at[0, rows, :] for r in (qf_ref, kf_ref, qb_ref, kb_ref)]
        _gla_decay_factors(q, k, la, pre, *key_outs, dec_ref.at[0, sub])


def _rope(x, cos, sin):
    heads = []
    for h in range(x.shape[1] // GLA_DK):
        xh = x[:, h * GLA_DK:(h + 1) * GLA_DK]
        heads.append(xh * cos + pltpu.roll(xh, GLA_DK // 2, 1) * sin)
    return jnp.concatenate(heads, axis=1)


def _gla_decay_factors(q, k, la, pre, qf_ref, kf_ref, qb_ref, kb_ref, dec_ref):
    kw = GLA_KEY_W
    tot4 = jnp.concatenate([pre[(c + 1) * GLA_CHUNK - 1:(c + 1) * GLA_CHUNK, :] for c in range(CHUNKS_PER_TILE)],
                           axis=0)

    def chunk_rows(t4):
        return jnp.broadcast_to(t4[:, None, :], (CHUNKS_PER_TILE, GLA_CHUNK, GLA_DK)).reshape(TOKEN_TILE, GLA_DK)

    for h in range(GLA_HEADS):
        sl = slice(h * GLA_DK, (h + 1) * GLA_DK)
        sl_b = slice(kw + h * GLA_DK, kw + (h + 1) * GLA_DK)
        pre_f, tot_f = pre[:, sl], chunk_rows(tot4[:, sl])
        pre_b, tot_b = pre[:, sl_b], chunk_rows(tot4[:, sl_b])
        ante_b = pre_b - la[:, sl_b]
        qh = q[:, sl]
        kh = k[:, sl]
        qf_ref[:, sl] = (qh * jnp.exp2(pre_f)).astype(_BF16)
        kf_ref[:, sl] = (kh * jnp.exp2(-pre_f)).astype(_BF16)
        qb_ref[:, sl] = (qh * jnp.exp2(tot_b - pre_b)).astype(_BF16)
        kb_ref[:, sl] = (kh * jnp.exp2(ante_b - tot_b)).astype(_BF16)

    dec_ref[...] = jnp.exp2(jnp.concatenate([tot4[:, 0:kw], tot4[:, kw:2 * kw]], axis=0))


def _chunk_prefix_matrix():
    i = np.arange(TOKEN_TILE)
    same = (i[:, None] // GLA_CHUNK) == (i[None, :] // GLA_CHUNK)
    return jnp.asarray(same & (i[None, :] <= i[:, None]), _BF16)


def _gla_project(layer_in, merge, mod_i, w_cat, w2_cat, dec_b, cos_t, sin_t):
    fused = merge is not None
    B = (merge[0] if fused else layer_in[1]).shape[0]
    L = STREAM_LEN
    n_w = w_cat.shape[1]
    tri = _chunk_prefix_matrix()
    in_specs, in_args = _merge_operands(merge) if fused else _stream_operands(*layer_in)
    key_spec = pl.BlockSpec((1, STEP_ROWS, GLA_KEY_W), lambda b, t: (b, t, 0))
    val_spec = pl.BlockSpec((1, STEP_ROWS, BRANCH), lambda b, t: (b, t, 0))
    key_shape = jax.ShapeDtypeStruct((B, L, GLA_KEY_W), _BF16)
    val_shape = jax.ShapeDtypeStruct((B, L, BRANCH), _BF16)
    rope_spec = pl.BlockSpec((STEP_ROWS, GLA_DK), lambda b, t: (t, 0))
    x_specs = [pl.BlockSpec((1, STEP_ROWS, D_MODEL), lambda b, t: (b, t, 0))] if fused else []
    x_shapes = [jax.ShapeDtypeStruct((B, L, D_MODEL), _F32)] if fused else []
    outs = pl.pallas_call(
        functools.partial(_gla_proj_kernel, fused=fused),
        grid=(B, L // STEP_ROWS),
        in_specs=in_specs + _mod_specs() + [
            _const_spec((D_MODEL, n_w)),
            _const_spec((LOW_PAD, 2 * GLA_KEY_W)),
            _const_spec((1, 2 * GLA_KEY_W)),
            rope_spec, rope_spec,
            _const_spec((TOKEN_TILE, TOKEN_TILE)),
        ],
        out_specs=x_specs + [key_spec] * 4 + [val_spec] * 2 + [
            pl.BlockSpec((1, SUBS_PER_STEP, 2 * CHUNKS_PER_TILE, GLA_KEY_W), lambda b, t: (b, t, 0, 0))],
        out_shape=x_shapes + [key_shape] * 4 + [val_shape] * 2 + [
            jax.ShapeDtypeStruct((B, L // TOKEN_TILE, 2 * CHUNKS_PER_TILE, GLA_KEY_W), _F32)],
        compiler_params=pltpu.CompilerParams(
            dimension_semantics=("arbitrary", "arbitrary"), vmem_limit_bytes=VMEM_LIMIT),
        name="gla_project",
    )(*in_args, mod_i, mod_i, w_cat, w2_cat, dec_b, cos_t, sin_t, tri)
    return (outs[0], outs[1:]) if fused else (None, outs)


GLA_HEADS_PER_STEP = 2
GLA_CHUNKS_PER_STEP = 6


def _gla_mix_kernel(qf_ref, kf_ref, qb_ref, kb_ref, v_ref, dec_ref,
                    out_ref, accf_ref, accb_ref, sf_ref, sb_ref):
    n_chunks = accf_ref.shape[0] // GLA_CHUNK
    n_ctx = CTX_LEN // GLA_CHUNK
    C = GLA_CHUNK
    sf_ref[...] = jnp.zeros_like(sf_ref)
    sb_ref[...] = jnp.zeros_like(sb_ref)
    ti = lax.broadcasted_iota(jnp.int32, (C, C), 0)
    si = lax.broadcasted_iota(jnp.int32, (C, C), 1)
    mask_f = si <= ti
    mask_b = si > ti
    sub8 = lax.broadcasted_iota(jnp.int32, (2 * CHUNKS_PER_TILE, GLA_DK), 0)

    fwd = (qf_ref, kf_ref, sf_ref, accf_ref, mask_f, 0)
    bwd = (qb_ref, kb_ref, sb_ref, accb_ref, mask_b, CHUNKS_PER_TILE)

    def body(i, carry):
        chains = []
        for u in range(GLA_CHUNKS_PER_STEP):
            cf = i * GLA_CHUNKS_PER_STEP + u
            cb = jnp.where(cf < n_ctx, n_ctx - 1 - cf, n_chunks + n_ctx - 1 - cf)
            for c, (q_ref, k_ref, s_ref, acc_ref, mask, dec_row) in ((cf, fwd), (cb, bwd)):
                for h in range(GLA_HEADS_PER_STEP):
                    chains.append(dict(u=u, c=c, h=h, q_ref=q_ref, k_ref=k_ref, s_ref=s_ref,
                                       acc_ref=acc_ref, mask=mask, dec_row=dec_row))
        for ch in chains:
            c, h = ch["c"], ch["h"]
            ch["rows"] = pl.ds(pl.multiple_of(c * C, C), C)
            ch["ksl"] = slice(h * GLA_DK, (h + 1) * GLA_DK)
            ch["vsl"] = slice(h * GLA_DV, (h + 1) * GLA_DV)
            ch["q"] = ch["q_ref"][0, ch["rows"], ch["ksl"]]
            ch["k"] = ch["k_ref"][0, ch["rows"], ch["ksl"]]
            ch["v"] = v_ref[0, ch["rows"], ch["vsl"]]
            ch["att"] = _dot_nt(ch["q"], ch["k"])
        for ch in chains:
            tile = lax.shift_right_logical(ch["c"], CHUNKS_PER_TILE.bit_length() - 1)
            dec8 = dec_ref[0, tile, :, ch["ksl"]]
            pick = sub8 == ch["dec_row"] + (ch["c"] & (CHUNKS_PER_TILE - 1))
            dec = jnp.sum(jnp.where(pick, dec8, 0.0), axis=0, keepdims=True)
            k_end = (ch["k"].astype(_F32) * dec).astype(_BF16)
            ch["kv"] = _dot_tn(k_end, ch["v"])
            dcol = jnp.broadcast_to(dec, (GLA_DK, GLA_DK)).T
            ch["decay"] = jnp.concatenate([dcol, dcol], axis=1)
        state = {}
        for u in range(GLA_CHUNKS_PER_STEP):
            for ch in chains:
                if ch["u"] != u:
                    continue
                key = (id(ch["s_ref"]), ch["h"])
                s = state[key] if u else ch["s_ref"][ch["h"]]
                att = jnp.where(ch["mask"], ch["att"], 0.0).astype(_BF16)
                lhs = jnp.concatenate([ch["q"], att], axis=1)
                rhs = jnp.concatenate([s.astype(_BF16), ch["v"]], axis=0)
                ch["acc_ref"][ch["rows"], ch["vsl"]] = _dot(lhs, rhs)
                state[key] = s * ch["decay"] + ch["kv"]
        for ch in chains:
            if ch["u"] == GLA_CHUNKS_PER_STEP - 1:
                ch["s_ref"][ch["h"]] = state[(id(ch["s_ref"]), ch["h"])]
        return carry

    lax.fori_loop(0, n_chunks // GLA_CHUNKS_PER_STEP, body, 0)

    def sum_body(j, carry):
        rows = pl.ds(pl.multiple_of(j * TOKEN_TILE, TOKEN_TILE), TOKEN_TILE)
        out_ref[0, rows, :] = (accf_ref[rows, :] + accb_ref[rows, :]).astype(_BF16)
        return carry

    lax.fori_loop(0, accf_ref.shape[0] // TOKEN_TILE, sum_body, 0)


def _gla_mix(qf, kf, qb, kb, v, dec):
    B, L, _ = qf.shape
    hs = GLA_HEADS_PER_STEP
    n_t = dec.shape[1]
    key_spec = pl.BlockSpec((1, L, hs * GLA_DK), lambda b, p: (b, 0, p))
    val_spec = pl.BlockSpec((1, L, hs * GLA_DV), lambda b, p: (b, 0, p))
    return pl.pallas_call(
        _gla_mix_kernel,
        grid=(B, GLA_HEADS // hs),
        in_specs=[key_spec] * 4 + [
            val_spec,
            pl.BlockSpec((1, n_t, 2 * CHUNKS_PER_TILE, hs * GLA_DK), lambda b, p: (b, 0, 0, p)),
        ],
        out_specs=val_spec,
        out_shape=jax.ShapeDtypeStruct((B, L, BRANCH), _BF16),
        scratch_shapes=[
            pltpu.VMEM((L, hs * GLA_DV), _F32),
            pltpu.VMEM((L, hs * GLA_DV), _F32),
            pltpu.VMEM((hs, GLA_DK, GLA_DV), _F32),
            pltpu.VMEM((hs, GLA_DK, GLA_DV), _F32),
        ],
        compiler_params=pltpu.CompilerParams(
            dimension_semantics=("arbitrary", "arbitrary"), vmem_limit_bytes=VMEM_LIMIT),
        name="gla_mix",
    )(qf, kf, qb, kb, v, dec)


def _na_proj_kernel(*refs, fused):
    tile_fn, refs = _layer_input_tiles(refs, fused, head_norm=True)
    modb_ref, modc_ref, w_ref = refs[:3]
    outs = refs[3:]
    xnew_ref = outs[0] if fused else None
    q_ref, k_ref, v_ref, g_ref = outs[1 if fused else 0:]
    t = pl.program_id(1)
    for sub in range(SUBS_PER_STEP):
        rows = slice(sub * TOKEN_TILE, (sub + 1) * TOKEN_TILE)
        m = _sub_tile_mod(modb_ref, modc_ref, (t == 0) if sub == 0 else False)
        hb = _modulate(tile_fn(sub, xnew_ref), m)
        q_ref[0, rows, :] = (_dot(hb, w_ref[:, 0:BRANCH]) * (NA_DH ** -0.5 * LOG2E)).astype(_BF16)
        k_ref[0, rows, :] = _dot(hb, w_ref[:, BRANCH:2 * BRANCH]).astype(_BF16)
        v_ref[0, rows, :] = _dot(hb, w_ref[:, 2 * BRANCH:3 * BRANCH]).astype(_BF16)
        g_ref[0, rows, :] = _dot(hb, w_ref[:, 3 * BRANCH:4 * BRANCH]).astype(_BF16)


def _na_project(layer_in, merge, mod_i, w_in):
    fused = merge is not None
    B = (merge[0] if fused else layer_in[1]).shape[0]
    L = STREAM_LEN
    in_specs, in_args = _merge_operands(merge) if fused else _stream_operands(*layer_in)
    spec = pl.BlockSpec((1, STEP_ROWS, BRANCH), lambda b, t: (b, t, 0))
    shape = jax.ShapeDtypeStruct((B, L, BRANCH), _BF16)
    x_specs = [pl.BlockSpec((1, STEP_ROWS, D_MODEL), lambda b, t: (b, t, 0))] if fused else []
    x_shapes = [jax.ShapeDtypeStruct((B, L, D_MODEL), _F32)] if fused else []
    outs = pl.pallas_call(
        functools.partial(_na_proj_kernel, fused=fused),
        grid=(B, L // STEP_ROWS),
        in_specs=in_specs + _mod_specs() + [_const_spec((D_MODEL, 4 * BRANCH))],
        out_specs=x_specs + [spec] * 4,
        out_shape=x_shapes + [shape] * 4,
        compiler_params=pltpu.CompilerParams(
            dimension_semantics=("arbitrary", "arbitrary"), vmem_limit_bytes=VMEM_LIMIT),
        name="na_project",
    )(*in_args, mod_i, mod_i, w_in)
    return (outs[0], outs[1:]) if fused else (None, outs)


NA_ROWS = SEQ_LEN // GRID_W
NA_PAIR_W = 2 * NA_DH
NA_BAND = NA_KH * GRID_W
NA_ROWS_PER_STEP = 4
NA_BATCH_PER_STEP = 8


def _stack_heads(q2):
    lane = lax.broadcasted_iota(jnp.int32, q2.shape, 1)
    zero = jnp.zeros_like(q2)
    return jnp.concatenate([jnp.where(lane < NA_DH, q2, zero), jnp.where(lane >= NA_DH, q2, zero)], axis=0)


def _unstack_heads(o, n):
    lane = lax.broadcasted_iota(jnp.int32, (n, NA_PAIR_W), 1)
    return jnp.where(lane < NA_DH, o[0:n], o[n:2 * n])


def _na_mix_kernel(q_ref, k_ref, v_ref, bias_ref, out_ref, sl_a, sc_a, sl_b, sc_b, *, ctx_queries):
    for e in range(NA_BATCH_PER_STEP):
        if ctx_queries:
            kc = k_ref[e, 0:CTX_LEN, :]
            vc = v_ref[e, 0:CTX_LEN, :]
            qs = _stack_heads(q_ref[e, 0:CTX_LEN, :])
            s = _dot_nt(qs, kc)
            m = jnp.max(s, axis=-1, keepdims=True)
            p = jnp.exp2(s - m)
            l = jnp.sum(p, axis=-1, keepdims=True)
            o = _dot(p.astype(_BF16), vc) / l
            out_ref[e, 0:CTX_LEN, :] = _unstack_heads(o, CTX_LEN).astype(_BF16)
        else:
            out_ref[e, 0:CTX_LEN, :] = jnp.zeros((CTX_LEN, NA_PAIR_W), _BF16)

    U = NA_ROWS_PER_STEP
    M = 2 * GRID_W
    groups_per_elem = NA_ROWS // U
    n_groups = NA_BATCH_PER_STEP * groups_per_elem

    def row_slices(g, u):
        g = jnp.asarray(g, jnp.int32)
        e = lax.shift_right_logical(g, groups_per_elem.bit_length() - 1)
        r = (g & (groups_per_elem - 1)) * U + u
        r_start = jnp.clip(r - NA_KH // 2, 0, NA_ROWS - NA_KH)
        q_rows = pl.ds(pl.multiple_of(CTX_LEN + r * GRID_W, GRID_W), GRID_W)
        band = pl.ds(pl.multiple_of(CTX_LEN + r_start * GRID_W, GRID_W), NA_BAND)
        return e, q_rows, band, r - r_start

    def scores(g, sl_ref, sc_ref):
        qss = []
        for u in range(U):
            e, q_rows, band, row_class = row_slices(g, u)
            qss.append(_stack_heads(q_ref[e, q_rows, :]))
            sl_ref[u] = _dot_nt(qss[u], k_ref[e, band, :]) + bias_ref[0, row_class]
        sc_ref[...] = _dot_nt(jnp.concatenate(qss, axis=0), k_ref[e, 0:CTX_LEN, :])

    def finish(g, sl_ref, sc_ref):
        p_locs, p_ctxs, ls = [], [], []
        for u in range(U):
            s_loc = sl_ref[u]
            s_ctx = sc_ref[u * M:(u + 1) * M, :]
            m = jnp.maximum(jnp.max(s_loc, axis=-1, keepdims=True), jnp.max(s_ctx, axis=-1, keepdims=True))
            p_loc = jnp.exp2(s_loc - m)
            p_ctx = jnp.exp2(s_ctx - m)
            ls.append(jnp.sum(p_loc, axis=-1, keepdims=True) + jnp.sum(p_ctx, axis=-1, keepdims=True))
            p_locs.append(p_loc.astype(_BF16))
            p_ctxs.append(p_ctx.astype(_BF16))
        e = row_slices(g, 0)[0]
        o_ctx_all = _dot(jnp.concatenate(p_ctxs, axis=0), v_ref[e, 0:CTX_LEN, :])
        for u in range(U):
            _, q_rows, band, _ = row_slices(g, u)
            o = (_dot(p_locs[u], v_ref[e, band, :]) + o_ctx_all[u * M:(u + 1) * M]) / ls[u]
            out_ref[e, q_rows, :] = _unstack_heads(o, GRID_W).astype(_BF16)

    buf_a = (sl_a, sc_a)
    buf_b = (sl_b, sc_b)
    scores(0, *buf_a)

    def pair_body(i, carry):
        g = 2 * i
        scores(g + 1, *buf_b)
        finish(g, *buf_a)
        scores(g + 2, *buf_a)
        finish(g + 1, *buf_b)
        return carry

    lax.fori_loop(0, n_groups // 2 - 1, pair_body, 0)
    scores(n_groups - 1, *buf_b)
    finish(n_groups - 2, *buf_a)
    finish(n_groups - 1, *buf_b)


RPB_ROWS_PAD = 16


def _na_bias_kernel(rpb_ref, out_ref):
    c = lax.broadcasted_iota(jnp.int32, (GRID_W, LANES), 0)
    lane = lax.broadcasted_iota(jnp.int32, (GRID_W, LANES), 1)
    kc = lane & (GRID_W - 1)
    col_start = jnp.clip(c - NA_KW // 2, 0, GRID_W - NA_KW)
    valid = (kc >= col_start) & (kc < col_start + NA_KW)
    low_half = lane < GRID_W
    for a in range(2):
        toe = []
        for ro in range(2 * NA_KH - 1):
            w = jnp.broadcast_to(rpb_ref[0, a, ro:ro + 1, :], (GRID_W, LANES)) * LOG2E
            t_lo = pltpu.roll(w, LANES - (NA_KW - 1), 1, stride=1, stride_axis=0)
            t_hi = pltpu.roll(w, GRID_W - (NA_KW - 1), 1, stride=1, stride_axis=0)
            toe.append((t_lo, t_hi))
        for d in range(NA_KH):
            for jj in range(NA_KH // 2):
                ro = 2 * jj - d + NA_KH - 1
                tile = jnp.where(low_half, toe[ro][0], toe[ro + 1][1])
                out_ref[0, d, a * GRID_W:(a + 1) * GRID_W, jj * LANES:(jj + 1) * LANES] = (
                    jnp.where(valid, tile, NEG_BIG))


NA_PAIRS = NA_HEADS // 2


def _na_bias_tables(rpb):
    n_layers, _, n_ro, n_co = rpb.shape
    rpb = jnp.pad(rpb, ((0, 0), (0, 0), (0, RPB_ROWS_PAD - n_ro), (0, LANES - n_co)))
    n = n_layers * NA_PAIRS
    return pl.pallas_call(
        _na_bias_kernel,
        grid=(n,),
        in_specs=[pl.BlockSpec((1, 2, RPB_ROWS_PAD, LANES), lambda p: (p, 0, 0, 0))],
        out_specs=pl.BlockSpec((1, NA_KH, 2 * GRID_W, NA_BAND), lambda p: (p, 0, 0, 0)),
        out_shape=jax.ShapeDtypeStruct((n, NA_KH, 2 * GRID_W, NA_BAND), _F32),
        compiler_params=pltpu.CompilerParams(dimension_semantics=("arbitrary",)),
        name="na_bias_table",
    )(rpb.reshape(n, 2, RPB_ROWS_PAD, LANES))


def _na_mix(q, k, v, bias_tables, na_layer, ctx_queries):
    B, L, _ = q.shape
    n_pairs = NA_PAIRS
    U = NA_ROWS_PER_STEP
    spec = pl.BlockSpec((NA_BATCH_PER_STEP, L, NA_PAIR_W), lambda p, b: (b, 0, p))
    score_bufs = [pltpu.VMEM((U, 2 * GRID_W, NA_BAND), _F32),
                  pltpu.VMEM((U * 2 * GRID_W, CTX_LEN), _F32)]
    return pl.pallas_call(
        functools.partial(_na_mix_kernel, ctx_queries=ctx_queries),
        grid=(n_pairs, B // NA_BATCH_PER_STEP),
        in_specs=[spec, spec, spec,
                  pl.BlockSpec((1, NA_KH, 2 * GRID_W, NA_BAND),
                               lambda p, b: (na_layer * NA_PAIRS + p, 0, 0, 0))],
        out_specs=spec,
        out_shape=jax.ShapeDtypeStruct((B, L, BRANCH), _BF16),
        scratch_shapes=score_bufs + score_bufs,
        compiler_params=pltpu.CompilerParams(
            dimension_semantics=("arbitrary", "arbitrary"), vmem_limit_bytes=VMEM_LIMIT),
        name="na_mix",
    )(q, k, v, bias_tables)


FINAL_SUBS = 4


def _final_merge_kernel(*refs):
    n = FINAL_SUBS
    o_refs, g_refs, x_refs = refs[0:n], refs[n:2 * n], refs[2 * n:3 * n]
    mod_ref, w_ref, lng_ref, lnb_ref, out_ref = refs[3 * n:]
    gate = mod_ref[0][:, 2 * D_MODEL:3 * D_MODEL]
    for sub in range(n):
        out_ref[0, sub * TOKEN_TILE:(sub + 1) * TOKEN_TILE, :] = _residual_layer_norm(
            x_refs[sub][0], _gated_out_proj(o_refs[sub][0], g_refs[sub][0], w_ref), gate, lng_ref, lnb_ref)


def _final_merge(o, g, xs, mod_i, w_out, ln_g, ln_b):
    B, L, _ = xs.shape
    rows = FINAL_SUBS * TOKEN_TILE
    n_t = (L - CTX_LEN) // rows

    def sub_specs(width):
        return [pl.BlockSpec((1, TOKEN_TILE, width), lambda b, t, s=s: (b, FINAL_SUBS * t + s + 1, 0))
                for s in range(FINAL_SUBS)]

    return pl.pallas_call(
        _final_merge_kernel,
        grid=(B, n_t),
        in_specs=sub_specs(BRANCH) + sub_specs(BRANCH) + sub_specs(D_MODEL) + [
            _mod_specs()[0],
            _const_spec((BRANCH, D_MODEL)), _const_spec((1, D_MODEL)), _const_spec((1, D_MODEL)),
        ],
        out_specs=pl.BlockSpec((1, rows, D_MODEL), lambda b, t: (b, t, 0)),
        out_shape=jax.ShapeDtypeStruct((B, n_t * rows, D_MODEL), _F32),
        compiler_params=pltpu.CompilerParams(
            dimension_semantics=("arbitrary", "arbitrary"), vmem_limit_bytes=VMEM_LIMIT),
        name="final_merge",
    )(*([o] * FINAL_SUBS + [g] * FINAL_SUBS + [xs] * FINAL_SUBS), mod_i, w_out,
      ln_g.reshape(1, D_MODEL), ln_b.reshape(1, D_MODEL))


def _rope_tables(n_tokens):
    pos = jnp.arange(n_tokens, dtype=jnp.int32)
    row = (pos // GRID_W).astype(_F32)
    col = (pos % GRID_W).astype(_F32)
    quarter = GLA_DK // 4
    inv = ROPE_BASE ** (-jnp.arange(quarter, dtype=_F32) / quarter)
    ang = jnp.concatenate([row[:, None] * inv, col[:, None] * inv], -1)
    cos, sin = jnp.cos(ang), jnp.sin(ang)
    cos = jnp.concatenate([cos, cos], -1)
    sin = jnp.concatenate([-sin, sin], -1)
    return (jnp.concatenate([jnp.ones((CTX_LEN, GLA_DK), _F32), cos], 0),
            jnp.concatenate([jnp.zeros((CTX_LEN, GLA_DK), _F32), sin], 0))


def kernel(x, c, ctx, c_ctx, ada_w, ada_b, ln_g, ln_b, w_out, gla_w_in, gla_dec_w1, gla_dec_w2,
           gla_dec_b, gla_norm_g, na_w_in, na_rpb):
    B, S, D = x.shape
    assert (B, S, D) == (BATCH, SEQ_LEN, D_MODEL) and ctx.shape == (B, CTX_LEN, D)
    cond = jnp.zeros((COND_ROWS, D), _F32).at[0:B].set(c).at[CTX_COND_ROW].set(c_ctx)
    mod = _modulation(cond, ada_w, ada_b).reshape(DEPTH, COND_ROWS, 1, 3 * D_MODEL)
    cos_t, sin_t = _rope_tables(S)
    bias_tables = _na_bias_tables(na_rpb)

    layer_in = (ctx, x)
    merge = None
    for i in range(DEPTH):
        j = i // N_MIXERS
        if i % N_MIXERS == 0:
            w1 = jnp.concatenate([gla_dec_w1[j, 0], gla_dec_w1[j, 1]], axis=1)
            w1 = jnp.pad(w1, ((0, 0), (0, LOW_PAD - 2 * GLA_GATE_RANK)))
            w_cat = jnp.concatenate([gla_w_in[j], w1], axis=1).astype(_BF16)
            w2_cat = jnp.zeros((LOW_PAD, 2 * GLA_KEY_W), _F32)
            w2_cat = w2_cat.at[0:GLA_GATE_RANK, 0:GLA_KEY_W].set(gla_dec_w2[j, 0])
            w2_cat = w2_cat.at[GLA_GATE_RANK:2 * GLA_GATE_RANK, GLA_KEY_W:].set(gla_dec_w2[j, 1])
            dec_b = gla_dec_b[j].reshape(1, 2 * GLA_KEY_W) * LOG2E
            x_new, (qf, kf, qb, kb, v, g, dec) = _gla_project(
                layer_in, merge, mod[i], w_cat, (w2_cat * LOG2E).astype(_BF16), dec_b, cos_t, sin_t)
            o = _gla_mix(qf, kf, qb, kb, v, dec)
            head_gain = jnp.tile(gla_norm_g[j], GLA_HEADS)
        else:
            x_new, (q, k, v, g) = _na_project(layer_in, merge, mod[i], na_w_in[j].astype(_BF16))
            o = _na_mix(q, k, v, bias_tables, j, ctx_queries=i < DEPTH - 1)
            head_gain = jnp.ones((BRANCH,), _F32)
        if x_new is not None:
            layer_in = (None, x_new)
        merge = (o, g) + layer_in + (mod[i], head_gain, w_out[i].astype(_BF16), ln_g[i], ln_b[i])
    assert DEPTH % N_MIXERS == 0, "the standalone final merge assumes the last layer is an NA layer"
    o, g, _, xs, mod_last, _, w_o, lg, lb = merge
    return _final_merge(o, g, xs, mod_last, w_o, lg, lb)
```
